```python
import jax, jax.numpy as jnp
from jax import lax
import numpy as np

D_MODEL = 1024
BATCH = 16
SEQ = 2048
DEPTH = 1
DEC_BATCH = 128
DEC_SEQ = 4
PAST_LEN = 8192
PAGE_SIZE = 128

HEAD_DIM = 64
D_ATTN = D_MODEL // 2
N_HEADS = D_ATTN // HEAD_DIM
MOBA_BLOCK = 256
MOBA_TOPK = 3
Q_CHUNK = 32
D_RG = D_MODEL // 2
RG_GROUPS = 8
RG_BLOCK = D_RG // RG_GROUPS
CONV_W = 4
RG_C = 8.0
N_EXPERTS = 32
TOPK_E = 4
D_FF = D_MODEL
SWIGLU_ALPHA = 1.702
SWIGLU_LIMIT = 7.0
EPS = 1e-6
N_MOD = 6

kernel_name = 'hybrid_moba_rglru_moe_step'


def rmsnorm(x, g):
    xf = x.astype(jnp.float32)
    y = xf * lax.rsqrt(jnp.mean(xf * xf, axis=-1, keepdims=True) + EPS)
    return (y * g.astype(jnp.float32)).astype(x.dtype)


def ada_params(c, w_ada, b_ada):
    m = jnp.einsum('bd,de->be', jax.nn.silu(c), w_ada) + b_ada
    return m.reshape(c.shape[0], N_MOD, 1, D_MODEL)


def modulate(h, shift, scale):
    return h * (1.0 + scale) + shift


def moba_core(q, k_loc, v_loc, m_loc, k_sel, v_sel, m_sel):
    qf = q.astype(jnp.float32) * (HEAD_DIM ** -0.5)
    s_loc = jnp.einsum('bqhd,bhld->bqhl', qf, k_loc.astype(jnp.float32))
    s_loc = jnp.where(m_loc[None, :, None, :], s_loc, -jnp.inf)
    if k_sel is None:
        p = jax.nn.softmax(s_loc, axis=-1)
        out = jnp.einsum('bqhl,bhld->bqhd', p, v_loc.astype(jnp.float32))
    else:
        s_sel = jnp.einsum('bqhd,bqhsd->bqhs', qf, k_sel.astype(jnp.float32))
        s_sel = jnp.where(m_sel, s_sel, -jnp.inf)
        ns = s_sel.shape[-1]
        p = jax.nn.softmax(jnp.concatenate([s_sel, s_loc], axis=-1), axis=-1)
        out = (jnp.einsum('bqhs,bqhsd->bqhd', p[..., :ns], v_sel.astype(jnp.float32))
               + jnp.einsum('bqhl,bhld->bqhd', p[..., ns:], v_loc.astype(jnp.float32)))
    return out.astype(q.dtype)


def moba_prompt(q, k, v):
    B_, T, H, Dh = q.shape
    nbp = -(-T // MOBA_BLOCK)
    pad = nbp * MOBA_BLOCK - T
    kp = jnp.pad(k, ((0, 0), (0, pad), (0, 0), (0, 0)))
    vp = jnp.pad(v, ((0, 0), (0, pad), (0, 0), (0, 0)))
    kb = kp.reshape(B_, nbp, MOBA_BLOCK, H, Dh).transpose(0, 3, 1, 2, 4)
    vb = vp.reshape(B_, nbp, MOBA_BLOCK, H, Dh).transpose(0, 3, 1, 2, 4)
    n_cand = nbp - 1
    n_sel = min(MOBA_TOPK, n_cand)
    own = jnp.arange(T) // MOBA_BLOCK
    if n_sel > 0:
        kmean = jnp.mean(kb[:, :, :n_cand].astype(jnp.float32), axis=3)
        s = jnp.einsum('bthd,bhnd->bthn', q.astype(jnp.float32), kmean)
        cand_ok = jnp.arange(n_cand)[None, :] < own[:, None]
        s = jnp.where(cand_ok[None, :, None, :], s, -jnp.inf)
        _, idx = lax.top_k(s, n_sel)
        sel_ok = idx < own[None, :, None, None]
    b_ix = jnp.arange(B_)[:, None, None, None]
    h_ix = jnp.arange(H)[None, None, :, None]

    def chunk(c):
        t0 = c * Q_CHUNK
        q_c = lax.dynamic_slice_in_dim(q, t0, Q_CHUNK, axis=1)
        t_c = t0 + jnp.arange(Q_CHUNK)
        b_own = t0 // MOBA_BLOCK
        k_loc = lax.dynamic_index_in_dim(kb, b_own, axis=2, keepdims=False)
        v_loc = lax.dynamic_index_in_dim(vb, b_own, axis=2, keepdims=False)
        m_loc = (b_own * MOBA_BLOCK + jnp.arange(MOBA_BLOCK))[None, :] <= t_c[:, None]
        if n_sel > 0:
            idx_c = lax.dynamic_slice_in_dim(idx, t0, Q_CHUNK, axis=1)
            ok_c = lax.dynamic_slice_in_dim(sel_ok, t0, Q_CHUNK, axis=1)
            k_sel = kb[b_ix, h_ix, idx_c].reshape(B_, Q_CHUNK, H, n_sel * MOBA_BLOCK, Dh)
            v_sel = vb[b_ix, h_ix, idx_c].reshape(B_, Q_CHUNK, H, n_sel * MOBA_BLOCK, Dh)
            m_sel = jnp.repeat(ok_c, MOBA_BLOCK, axis=-1)
            return moba_core(q_c, k_loc, v_loc, m_loc, k_sel, v_sel, m_sel)
        return moba_core(q_c, k_loc, v_loc, m_loc, None, None, None)

    out = lax.map(chunk, jnp.arange(T // Q_CHUNK))
    return out.transpose(1, 0, 2, 3, 4).reshape(B_, T, H, Dh)


def moba_sample(q, k_new, v_new, cache_k, cache_v, page_table, layer):
    DB, T, H, Dh = q.shape
    ps = cache_k.shape[2]
    past = page_table.shape[1] * ps
    n_cand = past // MOBA_BLOCK
    n_sel = min(MOBA_TOPK, n_cand)
    start = n_cand * MOBA_BLOCK
    pos_loc = start + jnp.arange(past - start)
    phys_loc = page_table[:, pos_loc // ps]
    k_lp = cache_k[layer, phys_loc, pos_loc % ps]
    v_lp = cache_v[layer, phys_loc, pos_loc % ps]
    k_loc = jnp.concatenate([k_lp.astype(k_new.dtype), k_new], axis=1).transpose(0, 2, 1, 3)
    v_loc = jnp.concatenate([v_lp.astype(v_new.dtype), v_new], axis=1).transpose(0, 2, 1, 3)
    m_loc = jnp.concatenate([jnp.ones((T, past - start), bool), jnp.tril(jnp.ones((T, T), bool))], axis=1)
    if n_sel == 0:
        return moba_core(q, k_loc, v_loc, m_loc, None, None, None)
    pos_blk = jnp.arange(n_cand * MOBA_BLOCK)
    k_past = cache_k[layer, page_table[:, pos_blk // ps], pos_blk % ps]
    kmean = jnp.mean(k_past.astype(jnp.float32).reshape(DB, n_cand, MOBA_BLOCK, H, Dh), axis=2)
    s = jnp.einsum('bthd,bnhd->bthn', q.astype(jnp.float32), kmean)
    _, idx = lax.top_k(s, n_sel)
    pos = idx[..., None] * MOBA_BLOCK + jnp.arange(MOBA_BLOCK)
    b_ix = jnp.arange(DB)[:, None, None, None, None]
    h_ix = jnp.arange(H)[None, None, :, None, None]
    phys = page_table[b_ix, pos // ps]
    k_sel = cache_k[layer, phys, pos % ps, h_ix].reshape(DB, T, H, n_sel * MOBA_BLOCK, Dh)
    v_sel = cache_v[layer, phys, pos % ps, h_ix].reshape(DB, T, H, n_sel * MOBA_BLOCK, Dh)
    m_sel = jnp.ones((DB, T, H, n_sel * MOBA_BLOCK), bool)
    return moba_core(q, k_loc, v_loc, m_loc, k_sel, v_sel, m_sel)


def causal_conv(u, buf, w, b):
    T = u.shape[1]
    full = jnp.concatenate([buf.astype(u.dtype), u], axis=1)
    out = b + sum(full[:, j:j + T] * w[j] for j in range(CONV_W))
    return out, full[:, T:]


def rglru(xc, w_a, b_a, w_i, b_i, lam, h0):
    B_, T, C = xc.shape
    xf = xc.astype(jnp.float32)
    xg = xf.reshape(B_, T, RG_GROUPS, RG_BLOCK)
    r = jax.nn.sigmoid(jnp.einsum('btgi,gij->btgj', xg, w_a.astype(jnp.float32)).reshape(B_, T, C) + b_a)
    ig = jax.nn.sigmoid(jnp.einsum('btgi,gij->btgj', xg, w_i.astype(jnp.float32)).reshape(B_, T, C) + b_i)
    log_a = -RG_C * r * jax.nn.softplus(-lam.astype(jnp.float32))
    a = jnp.exp(log_a)
    bx = jnp.sqrt(-jnp.expm1(2.0 * log_a)) * (ig * xf)
    bx = bx.at[:, 0].add(a[:, 0] * h0.astype(jnp.float32))

    def combine(p, s):
        a1, b1 = p
        a2, b2 = s
        return a1 * a2, a2 * b1 + b2

    _, hs = lax.associative_scan(combine, (a, bx), axis=1)
    return hs.astype(xc.dtype), hs[:, -1].astype(xc.dtype)


def mixers(h, attend, conv_buf, h0, w_in, conv_w, conv_b, w_rg_a, b_rg_a, w_rg_i, b_rg_i, rg_lambda,
           w_proj_a, w_proj_b, w_merge, b_merge, w_out):
    B_, T, _ = h.shape
    proj = jnp.einsum('btd,de->bte', h, w_in)
    q, k, v, u, g = jnp.split(proj, [D_ATTN, 2 * D_ATTN, 3 * D_ATTN, 3 * D_ATTN + D_RG], axis=-1)
    q = q.reshape(B_, T, N_HEADS, HEAD_DIM)
    k = k.reshape(B_, T, N_HEADS, HEAD_DIM)
    v = v.reshape(B_, T, N_HEADS, HEAD_DIM)
    o_a = attend(q, k, v).reshape(B_, T, D_ATTN)
    uc, conv_new = causal_conv(u, conv_buf, conv_w, conv_b)
    rg, h_last = rglru(uc, w_rg_a, b_rg_a, w_rg_i, b_rg_i, rg_lambda, h0)
    o_b = rg * jax.nn.gelu(g)
    gates = jax.nn.sigmoid(jnp.einsum('btd,de->bte', h, w_merge) + b_merge)
    g_a, g_b = jnp.split(gates, 2, axis=-1)
    merged = g_a * jnp.einsum('btc,cd->btd', o_a, w_proj_a) + g_b * jnp.einsum('btc,cd->btd', o_b, w_proj_b)
    mix = jnp.einsum('btd,de->bte', merged, w_out)
    return mix, k, v, h_last, conv_new


def moe(x, w_router, b_router, w_up, b_up, w_down, b_down):
    logits = (x @ w_router + b_router).astype(jnp.float32)
    top_v, top_i = lax.top_k(logits, TOPK_E)
    probs = jax.nn.softmax(top_v, axis=-1)
    gate = jnp.einsum('nk,nke->ne', probs, jax.nn.one_hot(top_i, N_EXPERTS, dtype=jnp.float32)).astype(x.dtype)
    y = jnp.zeros_like(x)
    for e in range(N_EXPERTS):
        hu = x @ w_up[e] + b_up[e]
        glu = jnp.minimum(hu[:, 0::2], SWIGLU_LIMIT)
        lin = jnp.clip(hu[:, 1::2], -SWIGLU_LIMIT, SWIGLU_LIMIT)
        act = glu * jax.nn.sigmoid(SWIGLU_ALPHA * glu) * (lin + 1.0)
        y = y + gate[:, e:e + 1] * (act @ w_down[e] + b_down[e])
    return y


def setup_inputs(seed: int = 0) -> dict:
    key = jax.random.key(seed)
    ks = jax.random.split(key, 40)
    f32 = jnp.float32
    n_pages = PAST_LEN // PAGE_SIZE
    n_phys = (DEC_BATCH * n_pages * 5) // 4

    def nrm(k, shape, s):
        return jax.random.normal(k, shape, f32) * s

    a_base = jax.random.uniform(ks[20], (DEPTH, D_RG), f32, 0.9, 0.999) ** (1.0 / RG_C)
    return {
        'x_prompt': nrm(ks[0], (BATCH, SEQ, D_MODEL), 1.0),
        'x_sample': nrm(ks[1], (DEC_BATCH, DEC_SEQ, D_MODEL), 1.0),
        'cache_k': nrm(ks[2], (DEPTH, n_phys, PAGE_SIZE, N_HEADS, HEAD_DIM), 1.0),
        'cache_v': nrm(ks[3], (DEPTH, n_phys, PAGE_SIZE, N_HEADS, HEAD_DIM), 1.0),
        'state_h': nrm(ks[4], (DEPTH, DEC_BATCH, D_RG), 0.5),
        'state_conv': nrm(ks[5], (DEPTH, DEC_BATCH, CONV_W - 1, D_RG), 1.0),
        'page_table': jax.random.permutation(ks[6], n_phys)[:DEC_BATCH * n_pages].reshape(DEC_BATCH, n_pages).astype(jnp.int32),
        'c_prompt': nrm(ks[7], (BATCH, D_MODEL), 1.0),
        'c_sample': nrm(ks[8], (DEC_BATCH, D_MODEL), 1.0),
        'w_ada': nrm(ks[9], (DEPTH, D_MODEL, N_MOD * D_MODEL), 0.5 * D_MODEL ** -0.5),
        'b_ada': nrm(ks[10], (DEPTH, N_MOD * D_MODEL), 0.02),
        'norm1': 1.0 + nrm(ks[11], (DEPTH, D_MODEL), 0.02),
        'norm2': 1.0 + nrm(ks[12], (DEPTH, D_MODEL), 0.02),
        'w_in': nrm(ks[13], (DEPTH, D_MODEL, 3 * D_ATTN + 2 * D_RG), D_MODEL ** -0.5),
        'conv_w': nrm(ks[14], (DEPTH, CONV_W, D_RG), CONV_W ** -0.5),
        'conv_b': nrm(ks[15], (DEPTH, D_RG), 0.02),
        'w_rg_a': nrm(ks[16], (DEPTH, RG_GROUPS, RG_BLOCK, RG_BLOCK), RG_BLOCK ** -0.5),
        'b_rg_a': nrm(ks[17], (DEPTH, D_RG), 0.02),
        'w_rg_i': nrm(ks[18], (DEPTH, RG_GROUPS, RG_BLOCK, RG_BLOCK), RG_BLOCK ** -0.5),
        'b_rg_i': nrm(ks[19], (DEPTH, D_RG), 0.02),
        'rg_lambda': jnp.log(a_base) - jnp.log1p(-a_base),
        'w_proj_a': nrm(ks[21], (DEPTH, D_ATTN, D_MODEL), D_ATTN ** -0.5),
        'w_proj_b': nrm(ks[22], (DEPTH, D_RG, D_MODEL), D_RG ** -0.5),
        'w_merge': nrm(ks[23], (DEPTH, D_MODEL, 2 * D_MODEL), D_MODEL ** -0.5),
        'b_merge': nrm(ks[24], (DEPTH, 2 * D_MODEL), 0.02),
        'w_out': nrm(ks[25], (DEPTH, D_MODEL, D_MODEL), D_MODEL ** -0.5),
        'w_router': nrm(ks[26], (DEPTH, D_MODEL, N_EXPERTS), D_MODEL ** -0.5),
        'b_router': nrm(ks[27], (DEPTH, N_EXPERTS), 0.01),
        'w_up': nrm(ks[28], (DEPTH, N_EXPERTS, D_MODEL, 2 * D_FF), D_MODEL ** -0.5),
        'b_up': nrm(ks[29], (DEPTH, N_EXPERTS, 2 * D_FF), 0.02),
        'w_down': nrm(ks[30], (DEPTH, N_EXPERTS, D_FF, D_MODEL), D_FF ** -0.5),
        'b_down': nrm(ks[31], (DEPTH, N_EXPERTS, D_MODEL), 0.02),
        'norm_f': 1.0 + nrm(ks[32], (D_MODEL,), 0.02),
    }


def reference(x_prompt, x_sample, cache_k, cache_v, state_h, state_conv, page_table, c_prompt, c_sample,
              w_ada, b_ada, norm1, norm2, w_in, conv_w, conv_b, w_rg_a, b_rg_a, w_rg_i, b_rg_i, rg_lambda,
              w_proj_a, w_proj_b, w_merge, b_merge, w_out, w_router, b_router, w_up, b_up, w_down, b_down, norm_f):
    xp, xs = x_prompt, x_sample
    Bp, T, _ = xp.shape
    Bs, Ts, _ = xs.shape
    kp_l, vp_l, hp_l, cp_l, ks_l, vs_l, hs_l, cs_l = [], [], [], [], [], [], [], []
    for l in range(DEPTH):
        mp = ada_params(c_prompt, w_ada[l], b_ada[l])
        ms = ada_params(c_sample, w_ada[l], b_ada[l])
        lw = (w_in[l], conv_w[l], conv_b[l], w_rg_a[l], b_rg_a[l], w_rg_i[l], b_rg_i[l], rg_lambda[l],
              w_proj_a[l], w_proj_b[l], w_merge[l], b_merge[l], w_out[l])
        hp = modulate(rmsnorm(xp, norm1[l]), mp[:, 0], mp[:, 1])
        hs = modulate(rmsnorm(xs, norm1[l]), ms[:, 0], ms[:, 1])
        mix_p, k_p, v_p, h_p, c_p = mixers(
            hp, moba_prompt, jnp.zeros((Bp, CONV_W - 1, D_RG), xp.dtype), jnp.zeros((Bp, D_RG), jnp.float32), *lw)
        attend_s = lambda q, k, v, l=l: moba_sample(q, k, v, cache_k, cache_v, page_table, l)
        mix_s, k_s, v_s, h_s, c_s = mixers(hs, attend_s, state_conv[l], state_h[l], *lw)
        xp = xp + mp[:, 2] * mix_p
        xs = xs + ms[:, 2] * mix_s
        hp2 = modulate(rmsnorm(xp, norm2[l]), mp[:, 3], mp[:, 4])
        hs2 = modulate(rmsnorm(xs, norm2[l]), ms[:, 3], ms[:, 4])
        tokens = jnp.concatenate([hp2.reshape(Bp * T, D_MODEL), hs2.reshape(Bs * Ts, D_MODEL)], axis=0)
        ff = moe(tokens, w_router[l], b_router[l], w_up[l], b_up[l], w_down[l], b_down[l])
        xp = xp + mp[:, 5] * ff[:Bp * T].reshape(Bp, T, D_MODEL)
        xs = xs + ms[:, 5] * ff[Bp * T:].reshape(Bs, Ts, D_MODEL)
        kp_l.append(k_p); vp_l.append(v_p); hp_l.append(h_p); cp_l.append(c_p)
        ks_l.append(k_s); vs_l.append(v_s); hs_l.append(h_s); cs_l.append(c_s)
    y_prompt = rmsnorm(xp, norm_f)
    y_sample = rmsnorm(xs, norm_f)
    k_prompt = jnp.stack(kp_l)
    v_prompt = jnp.stack(vp_l)
    h_prompt = jnp.stack(hp_l)
    conv_prompt = jnp.stack(cp_l)
    k_sample = jnp.stack(ks_l)
    v_sample = jnp.stack(vs_l)
    h_sample = jnp.stack(hs_l)
    conv_sample = jnp.stack(cs_l)
    return (y_prompt, y_sample, k_prompt, v_prompt, h_prompt, conv_prompt, k_sample, v_sample, h_sample, conv_sample)
```

```python
import functools

import jax
import jax.numpy as jnp
from jax import lax
from jax.experimental import pallas as pl
from jax.experimental.pallas import tpu as pltpu

F32 = jnp.float32
BF16 = jnp.bfloat16
I32 = jnp.int32

HEAD_DIM = 64
MOBA_BLOCK = 256
MOBA_TOPK = 3
RG_C = 8.0
TOPK_E = 4
SWIGLU_ALPHA = 1.702
SWIGLU_LIMIT = 7.0
EPS = 1e-6
N_MOD = 6

LANES = 128
TOK_TILE = 512
RG_TILE = 256
MOE_TILE = 256
ROW_TILE = 256
VMEM_LIMIT = 56 * 1024 * 1024


def _cparams(sem):
    return pltpu.CompilerParams(dimension_semantics=sem, vmem_limit_bytes=VMEM_LIMIT)


def _dot(a, b):
    return jnp.dot(a.astype(BF16), b.astype(BF16), preferred_element_type=F32)


def _dot_t(a, b):
    return lax.dot_general(a.astype(BF16), b.astype(BF16), (((1,), (1,)), ((), ())),
                           preferred_element_type=F32)


def _split(a):
    hi = a.astype(BF16)
    lo = (a - hi.astype(F32)).astype(BF16)
    return hi, lo


def _dot3(a, b):
    ah, al = _split(a)
    bh, bl = _split(b)
    return (jnp.dot(ah, bh, preferred_element_type=F32)
            + jnp.dot(al, bh, preferred_element_type=F32)
            + jnp.dot(ah, bl, preferred_element_type=F32))


def _dot3_t(a, b):
    ah, al = _split(a)
    bh, bl = _split(b)
    dn = (((1,), (1,)), ((), ()))
    return (lax.dot_general(ah, bh, dn, preferred_element_type=F32)
            + lax.dot_general(al, bh, dn, preferred_element_type=F32)
            + lax.dot_general(ah, bl, dn, preferred_element_type=F32))


def _sigmoid(x):
    return 1.0 / (1.0 + jnp.exp(-x))


def _rms(x, g):
    return x * lax.rsqrt(jnp.mean(x * x, axis=-1, keepdims=True) + EPS) * g


def _norm_mod(x, g, shift, scale):
    return _rms(x, g) * (1.0 + scale) + shift


def _gelu_tanh(x):
    return 0.5 * x * (1.0 + jnp.tanh(0.7978845608028654 * (x + 0.044715 * (x * x * x))))


def _ada_body(c_ref, w_ref, b_ref, o_ref):
    c = c_ref[...]
    o_ref[...] = _dot3(c * _sigmoid(c), w_ref[...]) + b_ref[...]


def _ada(c, w, b):
    n, d = c.shape
    e = w.shape[1]
    te = 1024
    return pl.pallas_call(
        _ada_body,
        grid=(e // te,),
        in_specs=[pl.BlockSpec((n, d), lambda j: (0, 0)),
                  pl.BlockSpec((d, te), lambda j: (0, j)),
                  pl.BlockSpec((1, te), lambda j: (0, j))],
        out_specs=pl.BlockSpec((n, te), lambda j: (0, j)),
        out_shape=jax.ShapeDtypeStruct((n, e), F32),
        compiler_params=_cparams(("arbitrary",)),
        name="ada",
    )(c, w, b.reshape(1, e))


def _proj_body(x_ref, sh_ref, sc_ref, g1_ref, wq_ref, wkvt_ref, wug_ref,
               q_ref, kt_ref, vt_ref, u_ref, g_ref, *, precise):
    h = _norm_mod(x_ref[...], g1_ref[...], sh_ref[0], sc_ref[0])
    mm, mmt = (_dot3, _dot3_t) if precise else (_dot, _dot_t)
    da = q_ref.shape[1]
    q_ref[...] = mm(h, wq_ref[...])
    kvt = mmt(wkvt_ref[...], h)
    kt_ref[0] = kvt[:da]
    vt_ref[0] = kvt[da:]
    ug = mm(h, wug_ref[...])
    dr = u_ref.shape[1]
    u_ref[...] = ug[:, :dr]
    g_ref[...] = ug[:, dr:]


def _proj(x2d, shift, scale, g1, wq, wkvt, wug, *, nb, t, precise):
    n, d = x2d.shape
    tm = min(TOK_TILE, t)
    nt = t // tm
    da = wq.shape[1]
    dr = wug.shape[1] // 2
    mrows = shift.shape[1]
    if mrows == 1:
        mod_spec = pl.BlockSpec((1, 1, d), lambda b, j: (b, 0, 0))
    else:
        mod_spec = pl.BlockSpec((1, tm, d), lambda b, j: (b, j, 0))
    full = lambda a: pl.BlockSpec(a.shape, lambda b, j: (0,) * a.ndim)
    tok = lambda w: pl.BlockSpec((tm, w), lambda b, j: (b * nt + j, 0))
    return pl.pallas_call(
        functools.partial(_proj_body, precise=precise),
        grid=(nb, nt),
        in_specs=[tok(d), mod_spec, mod_spec, full(g1), full(wq), full(wkvt), full(wug)],
        out_specs=[tok(da),
                   pl.BlockSpec((1, da, tm), lambda b, j: (b, 0, j)),
                   pl.BlockSpec((1, da, tm), lambda b, j: (b, 0, j)),
                   tok(dr), tok(dr)],
        out_shape=[jax.ShapeDtypeStruct((n, da), F32),
                   jax.ShapeDtypeStruct((nb, da, t), F32),
                   jax.ShapeDtypeStruct((nb, da, t), F32),
                   jax.ShapeDtypeStruct((n, dr), F32),
                   jax.ShapeDtypeStruct((n, dr), F32)],
        compiler_params=_cparams(("arbitrary", "arbitrary")),
        name="proj_s" if precise else "proj_p",
    )(x2d, shift, scale, g1, wq, wkvt, wug)


def _attn_p_body(q_ref, kt_ref, vt_ref, o_ref, kb, vb, kmean, *, nblk):
    i = pl.program_id(2)
    bs = MOBA_BLOCK

    @pl.when(i == 0)
    def _():
        k = kt_ref[0]
        v = vt_ref[0]
        cols = []
        for n in range(nblk):
            kn = k[:, n * bs:(n + 1) * bs]
            kb[n] = kn.astype(BF16)
            vb[n] = v[:, n * bs:(n + 1) * bs].astype(BF16)
            cols.append(jnp.sum(kn, axis=1, keepdims=True) * (1.0 / bs))
        kmean[...] = jnp.concatenate(cols, axis=1)

    row = lax.broadcasted_iota(I32, (bs, bs), 0)
    colk = lax.broadcasted_iota(I32, (bs, bs), 1)
    outs = []
    for hh in range(LANES // HEAD_DIM):
        hs = slice(hh * HEAD_DIM, (hh + 1) * HEAD_DIM)
        qh = q_ref[:, hs]
        sc = _dot3(qh, kmean[hs, :])
        col = lax.broadcasted_iota(I32, sc.shape, 1)
        valid = col < i
        sv = jnp.where(valid, sc, -jnp.inf)
        cnt = jnp.zeros_like(sc)
        for m in range(nblk - 1):
            sm = sv[:, m:m + 1]
            beats = (sm > sv) | ((sm == sv) & (m < col))
            cnt = cnt + jnp.where(beats, 1.0, 0.0)
        self = jnp.where(valid & (cnt < MOBA_TOPK), 1.0, 0.0)

        qb = (qh * (HEAD_DIM ** -0.5)).astype(BF16)
        s = jnp.dot(qb, kb[i, hs, :], preferred_element_type=F32)
        s = jnp.where(colk <= row, s, -jnp.inf)
        m0 = jnp.max(s, axis=1, keepdims=True)
        p = jnp.exp(s - m0)
        l0 = jnp.sum(p, axis=1, keepdims=True)
        a0 = _dot_t(p, vb[i, hs, :])

        def body(j, carry, hs=hs, qb=qb, self=self, col=col):
            m_, l_, acc = carry
            s = jnp.dot(qb, kb[j, hs, :], preferred_element_type=F32)
            selc = jnp.sum(jnp.where(col == j, self, 0.0), axis=1, keepdims=True)
            s = jnp.where(selc > 0.0, s, -jnp.inf)
            mn = jnp.maximum(m_, jnp.max(s, axis=1, keepdims=True))
            alpha = jnp.exp(m_ - mn)
            p = jnp.exp(s - mn)
            l_ = alpha * l_ + jnp.sum(p, axis=1, keepdims=True)
            acc = alpha * acc + _dot_t(p, vb[j, hs, :])
            return mn, l_, acc

        _, l1, a1 = lax.fori_loop(0, i, body, (m0, l0, a0))
        outs.append(a1 / l1)
    o_ref[...] = jnp.concatenate(outs, axis=1).astype(o_ref.dtype)


def _attn_p(q, kt, vt, *, nb, t):
    n, da = q.shape
    nblk = t // MOBA_BLOCK
    nhp = da // LANES
    return pl.pallas_call(
        functools.partial(_attn_p_body, nblk=nblk),
        grid=(nb, nhp, nblk),
        in_specs=[pl.BlockSpec((MOBA_BLOCK, LANES), lambda b, h, i: (b * nblk + i, h)),
                  pl.BlockSpec((1, LANES, t), lambda b, h, i: (b, h, 0)),
                  pl.BlockSpec((1, LANES, t), lambda b, h, i: (b, h, 0))],
        out_specs=pl.BlockSpec((MOBA_BLOCK, LANES), lambda b, h, i: (b * nblk + i, h)),
        out_shape=jax.ShapeDtypeStruct((n, da), BF16),
        scratch_shapes=[pltpu.VMEM((nblk, LANES, MOBA_BLOCK), BF16),
                        pltpu.VMEM((nblk, LANES, MOBA_BLOCK), BF16),
                        pltpu.VMEM((LANES, nblk), F32)],
        compiler_params=_cparams(("arbitrary", "arbitrary", "arbitrary")),
        name="attn_p",
    )(q, kt, vt)


def _rg_gates(uc, wa_ref, ba_ref, wi_ref, bi_ref, lam_ref):
    r = _sigmoid(_dot3(uc, wa_ref[...]) + ba_ref[...])
    ig = _sigmoid(_dot3(uc, wi_ref[...]) + bi_ref[...])
    nl = -lam_ref[...]
    softplus = jnp.maximum(nl, 0.0) + jnp.log(1.0 + jnp.exp(-jnp.abs(nl)))
    log_a = -RG_C * r * softplus
    a = jnp.exp(log_a)
    bx = jnp.sqrt(1.0 - jnp.exp(2.0 * log_a)) * (ig * uc)
    return a, bx


def _rg_p_body(u_ref, g_ref, cw_ref, cb_ref, wa_ref, ba_ref, wi_ref, bi_ref, lam_ref,
               o_ref, hl_ref, cn_ref, ubuf, hc):
    j = pl.program_id(1)
    tt = u_ref.shape[0]
    pad = 8

    @pl.when(j == 0)
    def _():
        ubuf[0:pad] = jnp.zeros((pad, ubuf.shape[1]), F32)
        hc[...] = jnp.zeros_like(hc)

    u = u_ref[...]
    ubuf[pad:pad + tt] = u
    cw = cw_ref[...]
    kw = cw.shape[0]
    uc = cb_ref[...] + u * cw[kw - 1:kw]
    for jj in range(kw - 1):
        off = pad - (kw - 1) + jj
        uc = uc + ubuf[off:off + tt] * cw[jj:jj + 1]
    a, bx = _rg_gates(uc, wa_ref, ba_ref, wi_ref, bi_ref, lam_ref)

    row = lax.broadcasted_iota(I32, a.shape, 0)
    d = 1
    while d < tt:
        keep = row >= d
        a_sh = jnp.where(keep, pltpu.roll(a, d, 0), 1.0)
        b_sh = jnp.where(keep, pltpu.roll(bx, d, 0), 0.0)
        bx = a * b_sh + bx
        a = a * a_sh
        d *= 2
    h = bx + a * hc[...]
    hc[...] = h[tt - 1:tt]
    hl_ref[0] = h[tt - 1:tt]
    cn_ref[0] = u[tt - (kw - 1):tt]
    ubuf[0:pad] = u[tt - pad:tt]
    o_ref[...] = (h * _gelu_tanh(g_ref[...])).astype(o_ref.dtype)


def _rg_p(u, g, cw, cb, wa, ba, wi, bi, lam, *, nb, t):
    n, dr = u.shape
    tt = min(RG_TILE, t)
    nt = t // tt
    kw = cw.shape[0]
    full = lambda a: pl.BlockSpec(a.shape, lambda b, j: (0,) * a.ndim)
    tok = pl.BlockSpec((tt, dr), lambda b, j: (b * nt + j, 0))
    return pl.pallas_call(
        _rg_p_body,
        grid=(nb, nt),
        in_specs=[tok, tok, full(cw), full(cb), full(wa), full(ba), full(wi), full(bi), full(lam)],
        out_specs=[tok,
                   pl.BlockSpec((1, 1, dr), lambda b, j: (b, 0, 0)),
                   pl.BlockSpec((1, kw - 1, dr), lambda b, j: (b, 0, 0))],
        out_shape=[jax.ShapeDtypeStruct((n, dr), BF16),
                   jax.ShapeDtypeStruct((nb, 1, dr), F32),
                   jax.ShapeDtypeStruct((nb, kw - 1, dr), F32)],
        scratch_shapes=[pltpu.VMEM((8 + tt, dr), F32), pltpu.VMEM((1, dr), F32)],
        compiler_params=_cparams(("arbitrary", "arbitrary")),
        name="rg_p",
    )(u, g, cw, cb, wa, ba, wi, bi, lam)


def _rg_s_body(u_ref, g_ref, cs_ref, h0_ref, cw_ref, cb_ref, wa_ref, ba_ref, wi_ref, bi_ref, lam_ref,
               o_ref, hl_ref, cn_ref):
    ts = u_ref.shape[0]
    cw = cw_ref[...]
    kw = cw.shape[0]
    full = [cs_ref[jj] for jj in range(kw - 1)] + [u_ref[tt] for tt in range(ts)]
    h = h0_ref[...]
    for tt in range(ts):
        uc = cb_ref[...]
        for jj in range(kw):
            uc = uc + full[tt + jj] * cw[jj:jj + 1]
        a, bx = _rg_gates(uc, wa_ref, ba_ref, wi_ref, bi_ref, lam_ref)
        h = a * h + bx
        o_ref[tt] = h * _gelu_tanh(g_ref[tt])
    hl_ref[...] = h
    for jj in range(kw - 1):
        cn_ref[jj] = full[ts + jj]


def _rg_s(u, g, cs, h0, cw, cb, wa, ba, wi, bi, lam):
    ts, db, dr = u.shape
    kw = cw.shape[0]
    return pl.pallas_call(
        _rg_s_body,
        out_shape=[jax.ShapeDtypeStruct((ts, db, dr), F32),
                   jax.ShapeDtypeStruct((db, dr), F32),
                   jax.ShapeDtypeStruct((kw - 1, db, dr), F32)],
        compiler_params=pltpu.CompilerParams(vmem_limit_bytes=VMEM_LIMIT),
        name="rg_s",
    )(u, g, cs, h0, cw, cb, wa, ba, wi, bi, lam)


def _kmean_s_body(pt_ref, *refs, npg):
    del pt_ref
    pages, o_ref = refs[:npg], refs[npg]
    cols = []
    for n in range(npg // 2):
        blk = pages[2 * n][0] + pages[2 * n + 1][0]
        cols.append(jnp.sum(blk, axis=1, keepdims=True) * (1.0 / MOBA_BLOCK))
    o_ref[0, 0] = jnp.concatenate(cols, axis=1)


def _kmean_s(pt_flat, kt3, *, db, n_pages):
    _, hd, ps = kt3.shape
    npg = 16
    nchunk = n_pages // npg
    specs = [pl.BlockSpec((1, hd, ps), functools.partial(
        lambda b, c, pt, p: (pt[b * n_pages + c * npg + p], 0, 0), p=p)) for p in range(npg)]
    return pl.pallas_call(
        functools.partial(_kmean_s_body, npg=npg),
        grid_spec=pltpu.PrefetchScalarGridSpec(
            num_scalar_prefetch=1,
            grid=(db, nchunk),
            in_specs=specs,
            out_specs=pl.BlockSpec((1, 1, hd, npg // 2), lambda b, c, pt: (b, c, 0, 0)),
        ),
        out_shape=jax.ShapeDtypeStruct((db, nchunk, hd, npg // 2), F32),
        compiler_params=_cparams(("arbitrary", "arbitrary")),
        name="kmean_s",
    )(pt_flat, *([kt3] * npg))


def _sel_s_body(qt_ref, km_ref, o_ref, *, nh):
    qt = qt_ref[0]
    km = km_ref[0]
    ts = qt.shape[1]
    nc = km.shape[1]
    lane = lax.broadcasted_iota(I32, (nh, nc), 1).astype(F32)
    outs = []
    for t in range(ts):
        prod = qt[:, t:t + 1] * km
        s = jnp.sum(prod.reshape(nh, HEAD_DIM, nc), axis=1)
        for _ in range(MOBA_TOPK):
            m = jnp.max(s, axis=1, keepdims=True)
            idx = jnp.min(jnp.where(s == m, lane, float(nc)), axis=1, keepdims=True)
            outs.append(idx)
            s = jnp.where(lane == idx, -jnp.inf, s)
    o_ref[0] = jnp.concatenate(outs, axis=1).astype(I32)


def _sel_s(qt, km):
    db, hd, ts = qt.shape
    nc = km.shape[2]
    nh = hd // HEAD_DIM
    return pl.pallas_call(
        functools.partial(_sel_s_body, nh=nh),
        grid=(db,),
        in_specs=[pl.BlockSpec((1, hd, ts), lambda b: (b, 0, 0)),
                  pl.BlockSpec((1, hd, nc), lambda b: (b, 0, 0))],
        out_specs=pl.BlockSpec((1, nh, ts * MOBA_TOPK), lambda b: (b, 0, 0)),
        out_shape=jax.ShapeDtypeStruct((db, nh, ts * MOBA_TOPK), I32),
        compiler_params=_cparams(("arbitrary",)),
        name="sel_s",
    )(qt, km)


def _attn_s_body(idx_ref, pt_ref, qt_ref, knt_ref, vnt_ref, kc_ref, vc_ref, o_ref,
                 kbuf, vbuf, sem, *, nh, ts, n_pages, ppb):
    b = pl.program_id(0)
    nb = pl.num_programs(0)
    nsl = nh * ts * MOBA_TOPK
    ps = kc_ref.shape[3]

    def copies(bb, slot, s):
        h = s // (ts * MOBA_TOPK)
        blk = idx_ref[bb * nsl + s]
        out = []
        for pg in range(ppb):
            page = pt_ref[bb * n_pages + blk * ppb + pg]
            out.append(pltpu.make_async_copy(
                kc_ref.at[page, h], kbuf.at[slot, s, :, pl.ds(pg * ps, ps)], sem.at[slot]))
            out.append(pltpu.make_async_copy(
                vc_ref.at[page, h], vbuf.at[slot, s, :, pl.ds(pg * ps, ps)], sem.at[slot]))
        return out

    def start_all(bb, slot):
        def f(s, c):
            for cp in copies(bb, slot, s):
                cp.start()
            return c
        lax.fori_loop(0, nsl, f, 0)

    def wait_all(bb, slot):
        def f(s, c):
            for cp in copies(bb, slot, s):
                cp.wait()
            return c
        lax.fori_loop(0, nsl, f, 0)

    @pl.when(b == 0)
    def _():
        start_all(0, 0)

    @pl.when(b + 1 < nb)
    def _():
        start_all(b + 1, (b + 1) % 2)

    slot = b % 2
    wait_all(b, slot)

    scale = HEAD_DIM ** -0.5
    tcol = lax.broadcasted_iota(I32, (1, ts), 1)

    def head(h, c):
        r0 = pl.multiple_of(h * HEAD_DIM, HEAD_DIM)
        qh = qt_ref[0, pl.ds(r0, HEAD_DIM), :] * scale
        knh = knt_ref[0, pl.ds(r0, HEAD_DIM), :]
        vnh = vnt_ref[0, pl.ds(r0, HEAD_DIM), :]
        cols = []
        for t in range(ts):
            qc = qh[:, t:t + 1]
            s_new = jnp.sum(qc * knh, axis=0, keepdims=True)
            s_new = jnp.where(tcol <= t, s_new, -jnp.inf)
            s_sel = []
            for j in range(MOBA_TOPK):
                sl = (h * ts + t) * MOBA_TOPK + j
                s_sel.append(jnp.sum(qc * kbuf[slot, sl], axis=0, keepdims=True))
            m = jnp.max(s_new, axis=1, keepdims=True)
            for sj in s_sel:
                m = jnp.maximum(m, jnp.max(sj, axis=1, keepdims=True))
            p_new = jnp.exp(s_new - m)
            l = jnp.sum(p_new, axis=1, keepdims=True)
            acc = jnp.sum(p_new * vnh, axis=1, keepdims=True)
            for j in range(MOBA_TOPK):
                sl = (h * ts + t) * MOBA_TOPK + j
                pj = jnp.exp(s_sel[j] - m)
                l = l + jnp.sum(pj, axis=1, keepdims=True)
                acc = acc + jnp.sum(pj * vbuf[slot, sl], axis=1, keepdims=True)
            cols.append(acc / l)
        o_ref[0, pl.ds(r0, HEAD_DIM), :] = jnp.concatenate(cols, axis=1)
        return c

    lax.fori_loop(0, nh, head, 0)


def _attn_s(idx_flat, pt_flat, qt, knt, vnt, kc4, vc4, *, n_pages):
    db, hd, ts = qt.shape
    _, nh, dh, ps = kc4.shape
    ppb = MOBA_BLOCK // ps
    nsl = nh * ts * MOBA_TOPK
    vec = pl.BlockSpec((1, hd, ts), lambda b, idx, pt: (b, 0, 0))
    return pl.pallas_call(
        functools.partial(_attn_s_body, nh=nh, ts=ts, n_pages=n_pages, ppb=ppb),
        grid_spec=pltpu.PrefetchScalarGridSpec(
            num_scalar_prefetch=2,
            grid=(db,),
            in_specs=[vec, vec, vec, pl.BlockSpec(memory_space=pl.ANY), pl.BlockSpec(memory_space=pl.ANY)],
            out_specs=vec,
            scratch_shapes=[pltpu.VMEM((2, nsl, dh, MOBA_BLOCK), F32),
                            pltpu.VMEM((2, nsl, dh, MOBA_BLOCK), F32),
                            pltpu.SemaphoreType.DMA((2,))],
        ),
        out_shape=jax.ShapeDtypeStruct((db, hd, ts), F32),
        compiler_params=_cparams(("arbitrary",)),
        name="attn_s",
    )(idx_flat, pt_flat, qt, knt, vnt, kc4, vc4)


def _post_body(x_ref, sh1_ref, sc1_ref, gt1_ref, sh2_ref, sc2_ref, oa_ref, ob_ref,
               g1_ref, g2_ref, wm_ref, bm_ref, wpa_ref, wpb_ref, wo_ref, wr_ref, br_ref,
               x1_ref, tok_ref, gate_ref, cnt_ref):
    x = x_ref[...]
    d = x.shape[1]
    h = _norm_mod(x, g1_ref[...], sh1_ref[0], sc1_ref[0])
    gates = _sigmoid(_dot(h, wm_ref[...]) + bm_ref[...])
    merged = gates[:, :d] * _dot(oa_ref[...], wpa_ref[...]) + gates[:, d:] * _dot(ob_ref[...], wpb_ref[...])
    x1 = x + gt1_ref[0] * _dot(merged, wo_ref[...])
    x1_ref[...] = x1
    h2 = _norm_mod(x1, g2_ref[...], sh2_ref[0], sc2_ref[0])
    tok_ref[...] = h2
    logits = _dot3(h2, wr_ref[...]) + br_ref[...]
    ne = logits.shape[1]
    lane = lax.broadcasted_iota(I32, logits.shape, 1).astype(F32)
    rem = logits
    vals, hots = [], []
    for _ in range(TOPK_E):
        m = jnp.max(rem, axis=1, keepdims=True)
        idx = jnp.min(jnp.where(rem == m, lane, float(ne)), axis=1, keepdims=True)
        hot = lane == idx
        vals.append(m)
        hots.append(hot)
        rem = jnp.where(hot, -jnp.inf, rem)
    es = [jnp.exp(v - vals[0]) for v in vals]
    den = es[0]
    for e in es[1:]:
        den = den + e
    gate = jnp.zeros_like(logits)
    sel = jnp.zeros_like(logits)
    for e, hot in zip(es, hots):
        gate = gate + jnp.where(hot, e / den, 0.0)
        sel = sel + jnp.where(hot, 1.0, 0.0)
    gate_ref[...] = jnp.concatenate([gate, sel], axis=1)
    cnt_ref[0] = jnp.sum(sel, axis=0, keepdims=True)


def _post(x2d, mods, oa, ob, g1, g2, wm, bm, wpa, wpb, wo, wr, br, *, nb, t):
    n, d = x2d.shape
    tm = min(TOK_TILE, t)
    nt = t // tm
    ne = wr.shape[1]
    mrows = mods[0].shape[1]
    if mrows == 1:
        mod_spec = pl.BlockSpec((1, 1, d), lambda b, j: (b, 0, 0))
    else:
        mod_spec = pl.BlockSpec((1, tm, d), lambda b, j: (b, j, 0))
    full = lambda a: pl.BlockSpec(a.shape, lambda b, j: (0,) * a.ndim)
    tok = lambda w: pl.BlockSpec((tm, w), lambda b, j: (b * nt + j, 0))
    ws = (g1, g2, wm, bm, wpa, wpb, wo, wr, br)
    return pl.pallas_call(
        _post_body,
        grid=(nb, nt),
        in_specs=[tok(d)] + [mod_spec] * 5 + [tok(oa.shape[1]), tok(ob.shape[1])] + [full(w) for w in ws],
        out_specs=[tok(d), tok(d), tok(2 * ne),
                   pl.BlockSpec((1, 1, ne), lambda b, j: (b * nt + j, 0, 0))],
        out_shape=[jax.ShapeDtypeStruct((n, d), F32),
                   jax.ShapeDtypeStruct((n, d), F32),
                   jax.ShapeDtypeStruct((n, 2 * ne), F32),
                   jax.ShapeDtypeStruct((nb * nt, 1, ne), F32)],
        compiler_params=_cparams(("arbitrary", "arbitrary")),
        name="post",
    )(x2d, *mods, oa, ob, *ws)


def _route_body(gs_ref, base_ref, pos_ref, g4_ref):
    gs = gs_ref[...]
    ne = gs.shape[1] // 2
    gate, sel = gs[:, :ne], gs[:, ne:]
    tm = gs.shape[0]
    r = lax.broadcasted_iota(I32, (tm, tm), 0)
    c = lax.broadcasted_iota(I32, (tm, tm), 1)
    tri = jnp.where(c < r, 1.0, 0.0).astype(BF16)
    rank = jnp.dot(tri, sel.astype(BF16), preferred_element_type=F32)
    posm = base_ref[0] + rank
    lane = lax.broadcasted_iota(I32, sel.shape, 1).astype(F32)
    rem = sel > 0.0
    ps, gv = [], []
    for _ in range(TOPK_E):
        idx = jnp.min(jnp.where(rem, lane, float(ne)), axis=1, keepdims=True)
        hot = lane == idx
        ps.append(jnp.sum(jnp.where(hot, posm, 0.0), axis=1, keepdims=True))
        gv.append(jnp.sum(jnp.where(hot, gate, 0.0), axis=1, keepdims=True))
        rem = rem & jnp.logical_not(hot)
    pos_ref[...] = jnp.concatenate(ps, axis=1).astype(I32)
    g4_ref[...] = jnp.concatenate(gv, axis=1)


def _route(gs, base):
    n, ne2 = gs.shape
    tm = TOK_TILE
    return pl.pallas_call(
        _route_body,
        grid=(n // tm,),
        in_specs=[pl.BlockSpec((tm, ne2), lambda j: (j, 0)),
                  pl.BlockSpec((1, 1, ne2 // 2), lambda j: (j, 0, 0))],
        out_specs=[pl.BlockSpec((tm, TOPK_E), lambda j: (j, 0)),
                   pl.BlockSpec((tm, TOPK_E), lambda j: (j, 0))],
        out_shape=[jax.ShapeDtypeStruct((n, TOPK_E), I32),
                   jax.ShapeDtypeStruct((n, TOPK_E), F32)],
        compiler_params=_cparams(("arbitrary",)),
        name="route",
    )(gs, base)


def _dispatch_body(pos_ref, tok_ref, xs_in, xs_ref, sem):
    del xs_in
    tm = tok_ref.shape[0]

    def copy(r, k):
        return pltpu.make_async_copy(tok_ref.at[pl.ds(r, 1)], xs_ref.at[pl.ds(pos_ref[r * TOPK_E + k], 1)], sem)

    def issue(r, c):
        for k in range(TOPK_E):
            copy(r, k).start()
        return c

    def drain(r, c):
        for k in range(TOPK_E):
            copy(r, k).wait()
        return c

    lax.fori_loop(0, tm, issue, 0)
    lax.fori_loop(0, tm, drain, 0)


def _dispatch(pos_flat, tok, xs):
    n, d = tok.shape
    tm = ROW_TILE
    return pl.pallas_call(
        _dispatch_body,
        grid=(n // tm,),
        in_specs=[pl.BlockSpec((tm * TOPK_E,), lambda j: (j,), memory_space=pltpu.SMEM),
                  pl.BlockSpec((tm, d), lambda j: (j, 0)),
                  pl.BlockSpec(memory_space=pl.ANY)],
        out_specs=pl.BlockSpec(memory_space=pl.ANY),
        out_shape=jax.ShapeDtypeStruct(xs.shape, xs.dtype),
        scratch_shapes=[pltpu.SemaphoreType.DMA(())],
        input_output_aliases={2: 0},
        compiler_params=_cparams(("arbitrary",)),
        name="dispatch",
    )(pos_flat, tok, xs)


def _moe_body(te_ref, nt_ref, x_ref, wg_ref, wl_ref, bg_ref, bl_ref, wd_ref, bd_ref, o_ref):
    del te_ref

    @pl.when(pl.program_id(0) < nt_ref[0])
    def _():
        x = x_ref[...].astype(BF16)
        glu = jnp.minimum(jnp.dot(x, wg_ref[0], preferred_element_type=F32) + bg_ref[0], SWIGLU_LIMIT)
        lin = jnp.clip(jnp.dot(x, wl_ref[0], preferred_element_type=F32) + bl_ref[0], -SWIGLU_LIMIT, SWIGLU_LIMIT)
        act = glu * _sigmoid(SWIGLU_ALPHA * glu) * (lin + 1.0)
        o_ref[...] = jnp.dot(act.astype(BF16), wd_ref[0], preferred_element_type=F32) + bd_ref[0]

    @pl.when(pl.program_id(0) >= nt_ref[0])
    def _():
        o_ref[...] = jnp.zeros_like(o_ref)


def _moe(tile_expert, n_tiles, xs, wg, wl, bg, bl, wd, bd):
    r, d = xs.shape
    tm = MOE_TILE
    dff = wg.shape[2]
    row = lambda g, te, nt: (jnp.minimum(g, nt[0] - 1), 0)
    wsp = lambda a: pl.BlockSpec((1,) + a.shape[1:], lambda g, te, nt: (te[g], 0, 0))
    return pl.pallas_call(
        _moe_body,
        grid_spec=pltpu.PrefetchScalarGridSpec(
            num_scalar_prefetch=2,
            grid=(r // tm,),
            in_specs=[pl.BlockSpec((tm, d), row), wsp(wg), wsp(wl), wsp(bg), wsp(bl), wsp(wd), wsp(bd)],
            out_specs=pl.BlockSpec((tm, d), lambda g, te, nt: (g, 0)),
        ),
        out_shape=jax.ShapeDtypeStruct((r, d), F32),
        compiler_params=_cparams(("arbitrary",)),
        name="moe",
    )(tile_expert, n_tiles, xs, wg, wl, bg, bl, wd, bd)


def _combine_body(pos_ref, g4_ref, x1_ref, gt2_ref, nf_ref, ys_ref, y_ref, buf, sem):
    tm = x1_ref.shape[0]

    def copy(r, k):
        return pltpu.make_async_copy(ys_ref.at[pl.ds(pos_ref[r * TOPK_E + k], 1)], buf.at[k, pl.ds(r, 1)], sem)

    def issue(r, c):
        for k in range(TOPK_E):
            copy(r, k).start()
        return c

    def drain(r, c):
        for k in range(TOPK_E):
            copy(r, k).wait()
        return c

    lax.fori_loop(0, tm, issue, 0)
    lax.fori_loop(0, tm, drain, 0)
    g4 = g4_ref[...]
    ff = g4[:, 0:1] * buf[0]
    for k in range(1, TOPK_E):
        ff = ff + g4[:, k:k + 1] * buf[k]
    y_ref[...] = _rms(x1_ref[...] + gt2_ref[0] * ff, nf_ref[...])


def _combine(pos_flat, g4, x1, gt2, nf, ys, *, nb, t):
    n, d = x1.shape
    tm = min(ROW_TILE, t)
    nt = t // tm
    if gt2.shape[1] == 1:
        mod_spec = pl.BlockSpec((1, 1, d), lambda j: (j // nt, 0, 0))
    else:
        mod_spec = pl.BlockSpec((1, tm, d), lambda j: (j // nt, j % nt, 0))
    return pl.pallas_call(
        _combine_body,
        grid=(n // tm,),
        in_specs=[pl.BlockSpec((tm * TOPK_E,), lambda j: (j,), memory_space=pltpu.SMEM),
                  pl.BlockSpec((tm, TOPK_E), lambda j: (j, 0)),
                  pl.BlockSpec((tm, d), lambda j: (j, 0)),
                  mod_spec,
                  pl.BlockSpec((1, d), lambda j: (0, 0)),
                  pl.BlockSpec(memory_space=pl.ANY)],
        out_specs=pl.BlockSpec((tm, d), lambda j: (j, 0)),
        out_shape=jax.ShapeDtypeStruct((n, d), F32),
        scratch_shapes=[pltpu.VMEM((TOPK_E, tm, d), F32), pltpu.SemaphoreType.DMA(())],
        compiler_params=_cparams(("arbitrary",)),
        name="combine",
    )(pos_flat, g4, x1, gt2, nf, ys)


def _block_diag(w):
    g, a, b = w.shape
    eye = jnp.eye(g, dtype=w.dtype)
    return (eye[:, None, :, None] * w[:, :, None, :]).reshape(g * a, g * b)


def kernel(x_prompt, x_sample, cache_k, cache_v, state_h, state_conv, page_table, c_prompt, c_sample,
           w_ada, b_ada, norm1, norm2, w_in, conv_w, conv_b, w_rg_a, b_rg_a, w_rg_i, b_rg_i, rg_lambda,
           w_proj_a, w_proj_b, w_merge, b_merge, w_out, w_router, b_router, w_up, b_up, w_down, b_down, norm_f):
    assert w_ada.shape[0] == 1, "single trunk layer"
    bp, t, d = x_prompt.shape
    db, ts, _ = x_sample.shape
    nh = cache_k.shape[3]
    hd = nh * HEAD_DIM
    da = hd
    dr = conv_w.shape[2]
    n_pages = page_table.shape[1]
    ps = cache_k.shape[2]
    ne = w_router.shape[2]
    np_, ns_ = bp * t, db * ts
    assert t % TOK_TILE == 0 and ns_ == TOK_TILE and (n_pages * ps) % MOBA_BLOCK == 0 and n_pages % 16 == 0

    mod = _ada(jnp.concatenate([c_prompt, c_sample], axis=0), w_ada[0], b_ada[0]).reshape(bp + db, N_MOD, d)
    mp = [mod[:bp, i][:, None, :] for i in range(N_MOD)]
    ms = [jnp.tile(mod[bp:, i], (ts, 1))[None] for i in range(N_MOD)]

    g1, g2, nf = norm1[0][None], norm2[0][None], norm_f[None]
    w_in0 = w_in[0]
    wq, wkv, wug = w_in0[:, :da], w_in0[:, da:3 * da], w_in0[:, 3 * da:]
    wkvt = wkv.T

    xp2 = x_prompt.reshape(np_, d)
    xs2 = jnp.swapaxes(x_sample, 0, 1).reshape(ns_, d)

    q_p, kt_p, vt_p, u_p, g_p = _proj(xp2, mp[0], mp[1], g1, wq.astype(BF16), wkvt.astype(BF16), wug.astype(BF16),
                                      nb=bp, t=t, precise=False)
    q_s, kt_s, vt_s, u_s, g_s = _proj(xs2, ms[0], ms[1], g1, wq, wkvt, wug, nb=1, t=ns_, precise=True)

    oa_p = _attn_p(q_p, kt_p, vt_p, nb=bp, t=t)

    kc4 = jnp.transpose(cache_k[0], (0, 2, 3, 1))
    vc4 = jnp.transpose(cache_v[0], (0, 2, 3, 1))
    pt_flat = page_table.reshape(-1)
    km = _kmean_s(pt_flat, kc4.reshape(-1, hd, ps), db=db, n_pages=n_pages)
    km = jnp.transpose(km, (0, 2, 1, 3)).reshape(db, hd, -1)
    tmaj = lambda a: jnp.transpose(a.reshape(hd, ts, db), (2, 0, 1))
    qt_s = jnp.transpose(q_s.reshape(ts, db, hd), (1, 2, 0))
    knt_s, vnt_s = tmaj(kt_s[0]), tmaj(vt_s[0])
    idx = _sel_s(qt_s, km)
    ot_s = _attn_s(idx.reshape(-1), pt_flat, qt_s, knt_s, vnt_s, kc4, vc4, n_pages=n_pages)
    oa_s = jnp.transpose(ot_s, (2, 0, 1)).reshape(ns_, hd)

    cw, cb = conv_w[0], conv_b[0][None]
    wa, wi = _block_diag(w_rg_a[0]), _block_diag(w_rg_i[0])
    ba, bi, lam = b_rg_a[0][None], b_rg_i[0][None], rg_lambda[0][None]
    ob_p, hl_p, cn_p = _rg_p(u_p, g_p, cw, cb, wa, ba, wi, bi, lam, nb=bp, t=t)
    ob_s, hl_s, cn_s = _rg_s(u_s.reshape(ts, db, dr), g_s.reshape(ts, db, dr),
                             jnp.swapaxes(state_conv[0], 0, 1), state_h[0], cw, cb, wa, ba, wi, bi, lam)

    wsm = (g1, g2, w_merge[0].astype(BF16), b_merge[0][None], w_proj_a[0].astype(BF16), w_proj_b[0].astype(BF16),
           w_out[0].astype(BF16), w_router[0], b_router[0][None])
    x1_p, tok_p, gs_p, cnt_p = _post(xp2, mp[:5], oa_p, ob_p, *wsm, nb=bp, t=t)
    x1_s, tok_s, gs_s, cnt_s = _post(xs2, ms[:5], oa_s.astype(BF16), ob_s.reshape(ns_, dr).astype(BF16), *wsm,
                                     nb=1, t=ns_)

    gs = jnp.concatenate([gs_p, gs_s], axis=0)
    cnt = jnp.concatenate([cnt_p, cnt_s], axis=0)[:, 0, :].astype(I32)
    tot = jnp.sum(cnt, axis=0)
    tiles_e = (tot + MOE_TILE - 1) // MOE_TILE
    tile_end = jnp.cumsum(tiles_e)
    poff = (tile_end - tiles_e) * MOE_TILE
    base = poff[None, :] + jnp.cumsum(cnt, axis=0) - cnt
    n_all = np_ + ns_
    g_max = (n_all * TOPK_E) // MOE_TILE + ne
    n_tiles = tile_end[-1:].astype(I32)
    te = jnp.searchsorted(tile_end, jnp.arange(g_max, dtype=I32), side="right").astype(I32)
    te = jnp.minimum(te, te[jnp.maximum(n_tiles[0] - 1, 0)])
    pos, g4 = _route(gs, base.astype(F32)[:, None, :])
    pos_flat = pos.reshape(-1)

    xs = jnp.zeros((g_max * MOE_TILE, d), F32)
    xs = _dispatch(pos_flat[:np_ * TOPK_E], tok_p, xs)
    xs = _dispatch(pos_flat[np_ * TOPK_E:], tok_s, xs)
    w_up0 = w_up[0]
    ys = _moe(te, n_tiles, xs,
              w_up0[:, :, 0::2].astype(BF16), w_up0[:, :, 1::2].astype(BF16),
              b_up[0][:, None, 0::2], b_up[0][:, None, 1::2],
              w_down[0].astype(BF16), b_down[0][:, None, :])
    y_p = _combine(pos_flat[:np_ * TOPK_E], g4[:np_], x1_p, mp[5], nf, ys, nb=bp, t=t)
    y_s = _combine(pos_flat[np_ * TOPK_E:], g4[np_:], x1_s, ms[5], nf, ys, nb=1, t=ns_)

    y_prompt = y_p.reshape(bp, t, d)
    y_sample = jnp.swapaxes(y_s.reshape(ts, db, d), 0, 1)
    to5 = lambda a: jnp.transpose(a.reshape(bp, nh, HEAD_DIM, t), (0, 3, 1, 2))[None]
    k_prompt, v_prompt = to5(kt_p), to5(vt_p)
    to5s = lambda a: jnp.transpose(a.reshape(nh, HEAD_DIM, ts, db), (3, 2, 0, 1))[None]
    k_sample, v_sample = to5s(kt_s[0]), to5s(vt_s[0])
    h_prompt = hl_p.reshape(1, bp, dr)
    conv_prompt = cn_p[None]
    h_sample = hl_s[None]
    conv_sample = jnp.swapaxes(cn_s, 0, 1)[None]
    return (y_prompt, y_sample, k_prompt, v_prompt, h_prompt, conv_prompt,
            k_sample, v_sample, h_sample, conv_sample)
```

```python
import functools

import jax
import jax.numpy as jnp
from jax import lax
from jax.experimental import pallas as pl
from jax.experimental.pallas import tpu as pltpu

F32 = jnp.float32
BF16 = jnp.bfloat16
I32 = jnp.int32

HEAD_DIM = 64
MOBA_BLOCK = 256
MOBA_TOPK = 3
RG_C = 8.0
TOPK_E = 4
SWIGLU_ALPHA = 1.702
SWIGLU_LIMIT = 7.0
EPS = 1e-6
N_MOD = 6

LANES = 128
TOK_TILE = 512
ATTN_HEADS = 4
RG_TILE = 256
MOE_TILE = 512
ROW_TILE = 256
VMEM_LIMIT = 56 * 1024 * 1024


def _cparams(sem):
    return pltpu.CompilerParams(dimension_semantics=sem, vmem_limit_bytes=VMEM_LIMIT)


def _dot(a, b):
    return jnp.dot(a.astype(BF16), b.astype(BF16), preferred_element_type=F32)


def _dot_t(a, b):
    return lax.dot_general(a.astype(BF16), b.astype(BF16), (((1,), (1,)), ((), ())),
                           preferred_element_type=F32)


def _split(a):
    hi = a.astype(BF16)
    lo = (a - hi.astype(F32)).astype(BF16)
    return hi, lo


def _dot3(a, b):
    ah, al = _split(a)
    bh, bl = _split(b)
    return (jnp.dot(ah, bh, preferred_element_type=F32)
            + jnp.dot(al, bh, preferred_element_type=F32)
            + jnp.dot(ah, bl, preferred_element_type=F32))


def _dot3_t(a, b):
    ah, al = _split(a)
    bh, bl = _split(b)
    dn = (((1,), (1,)), ((), ()))
    return (lax.dot_general(ah, bh, dn, preferred_element_type=F32)
            + lax.dot_general(al, bh, dn, preferred_element_type=F32)
            + lax.dot_general(ah, bl, dn, preferred_element_type=F32))


def _sigmoid(x):
    return 1.0 / (1.0 + jnp.exp(-x))


def _rms(x, g):
    return x * lax.rsqrt(jnp.mean(x * x, axis=-1, keepdims=True) + EPS) * g


def _norm_mod(x, g, shift, scale):
    return _rms(x, g) * (1.0 + scale) + shift


def _gelu_tanh(x):
    return 0.5 * x * (1.0 + jnp.tanh(0.7978845608028654 * (x + 0.044715 * (x * x * x))))


def _ada_body(c_ref, w_ref, b_ref, o_ref):
    c = c_ref[...]
    o_ref[...] = _dot3(c * _sigmoid(c), w_ref[...]) + b_ref[...]


def _ada(c, w, b):
    n, d = c.shape
    e = w.shape[1]
    te = 1024
    return pl.pallas_call(
        _ada_body,
        grid=(e // te,),
        in_specs=[pl.BlockSpec((n, d), lambda j: (0, 0)),
                  pl.BlockSpec((d, te), lambda j: (0, j)),
                  pl.BlockSpec((1, te), lambda j: (0, j))],
        out_specs=pl.BlockSpec((n, te), lambda j: (0, j)),
        out_shape=jax.ShapeDtypeStruct((n, e), F32),
        compiler_params=_cparams(("arbitrary",)),
        name="ada",
    )(c, w, b.reshape(1, e))


def _proj_body(x_ref, sh_ref, sc_ref, g1_ref, wqkvt_ref, wug_ref, *rest, prompt):
    h = _norm_mod(x_ref[...], g1_ref[...], sh_ref[0], sc_ref[0])
    mm, mmt = (_dot, _dot_t) if prompt else (_dot3, _dot3_t)
    if prompt:
        wk_ref, qt_ref, kt_ref, vt_ref, u_ref, g_ref, kb_ref, vtb_ref, km_ref = rest
    else:
        qt_ref, kt_ref, vt_ref, u_ref, g_ref = rest
    da = qt_ref.shape[1]
    qkvt = mmt(wqkvt_ref[...], h)
    qt_ref[0] = qkvt[:da]
    kt_ref[0] = qkvt[da:2 * da]
    vt = qkvt[2 * da:]
    vt_ref[0] = vt
    ug = mm(h, wug_ref[...])
    dr = u_ref.shape[1]
    u_ref[...] = ug[:, :dr]
    g_ref[...] = ug[:, dr:]
    if prompt:
        k = mm(h, wk_ref[...])
        kb_ref[...] = k.astype(BF16)
        vtb_ref[0] = vt.astype(BF16)
        means = [jnp.sum(k[n * MOBA_BLOCK:(n + 1) * MOBA_BLOCK], axis=0, keepdims=True) * (1.0 / MOBA_BLOCK)
                 for n in range(k.shape[0] // MOBA_BLOCK)]
        km_ref[0] = jnp.concatenate(means, axis=0)


def _proj(x2d, shift, scale, g1, wqkvt, wug, wk=None, *, nb, t):
    prompt = wk is not None
    n, d = x2d.shape
    tm = min(TOK_TILE, t)
    nt = t // tm
    da = wqkvt.shape[0] // 3
    dr = wug.shape[1] // 2
    mrows = shift.shape[1]
    if mrows == 1:
        mod_spec = pl.BlockSpec((1, 1, d), lambda b, j: (b, 0, 0))
    else:
        mod_spec = pl.BlockSpec((1, tm, d), lambda b, j: (b, j, 0))
    full = lambda a: pl.BlockSpec(a.shape, lambda b, j: (0,) * a.ndim)
    tok = lambda w: pl.BlockSpec((tm, w), lambda b, j: (b * nt + j, 0))
    featmaj = pl.BlockSpec((1, da, tm), lambda b, j: (b, 0, j))
    in_specs = [tok(d), mod_spec, mod_spec, full(g1), full(wqkvt), full(wug)]
    out_specs = [featmaj, featmaj, featmaj, tok(dr), tok(dr)]
    out_shape = [jax.ShapeDtypeStruct((nb, da, t), F32)] * 3 + [jax.ShapeDtypeStruct((n, dr), F32)] * 2
    args = [x2d, shift, scale, g1, wqkvt, wug]
    if prompt:
        bpt = tm // MOBA_BLOCK
        in_specs.append(full(wk))
        args.append(wk)
        out_specs += [tok(da), featmaj,
                      pl.BlockSpec((1, bpt, da), lambda b, j: (b * nt + j, 0, 0))]
        out_shape += [jax.ShapeDtypeStruct((n, da), BF16),
                      jax.ShapeDtypeStruct((nb, da, t), BF16),
                      jax.ShapeDtypeStruct((nb * nt, bpt, da), F32)]
    return pl.pallas_call(
        functools.partial(_proj_body, prompt=prompt),
        grid=(nb, nt),
        in_specs=in_specs,
        out_specs=out_specs,
        out_shape=out_shape,
        compiler_params=_cparams(("arbitrary", "arbitrary")),
        name="proj_p" if prompt else "proj_s",
    )(*args)


def _attn_p_body(qt_ref, kb_ref, kd_ref, vtb_ref, vd_ref, km_ref, o_ref, sel_ref, *, nblk):
    i = pl.program_id(2)
    bs = MOBA_BLOCK
    krow = lax.broadcasted_iota(I32, (bs, bs), 0)
    qcol = lax.broadcasted_iota(I32, (bs, bs), 1)
    blk = lax.broadcasted_iota(I32, (nblk, bs), 0)
    valid = blk < i
    nhh = qt_ref.shape[1] // HEAD_DIM
    heads = [slice(hh * HEAD_DIM, (hh + 1) * HEAD_DIM) for hh in range(nhh)]

    def run(npast):
        outs = []
        for hh, hs in enumerate(heads):
            qt = qt_ref[0, hs, :]
            sv = jnp.where(valid, _dot3(km_ref[0, :, hs], qt), -jnp.inf)
            cnt = jnp.zeros_like(sv)
            for m in range(nblk - 1):
                sm = sv[m:m + 1, :]
                beats = (sm > sv) | ((sm == sv) & (m < blk))
                cnt = cnt + jnp.where(beats, 1.0, 0.0)
            sel_ref[hh] = jnp.where(valid & (cnt < MOBA_TOPK), 1.0, 0.0)

            qb = (qt * (HEAD_DIM ** -0.5)).astype(BF16)
            sd = jnp.dot(kd_ref[:, hs], qb, preferred_element_type=F32)
            sd = jnp.where(krow <= qcol, sd, -jnp.inf)
            sp = jnp.dot(kb_ref[0:npast * bs, hs], qb, preferred_element_type=F32)
            sps = [jnp.where(sel_ref[hh, j:j + 1, :] > 0.0, sp[j * bs:(j + 1) * bs], -jnp.inf)
                   for j in range(npast)]
            mx = jnp.max(sd, axis=0, keepdims=True)
            for sj in sps:
                mx = jnp.maximum(mx, jnp.max(sj, axis=0, keepdims=True))
            pd = jnp.exp(sd - mx)
            den = jnp.sum(pd, axis=0, keepdims=True)
            pps = []
            for sj in sps:
                pj = jnp.exp(sj - mx)
                den = den + jnp.sum(pj, axis=0, keepdims=True)
                pps.append(pj.astype(BF16))
            acc = jnp.dot(vd_ref[0, hs, :], pd.astype(BF16), preferred_element_type=F32)
            acc = acc + jnp.dot(vtb_ref[0, hs, 0:npast * bs], jnp.concatenate(pps, axis=0),
                                preferred_element_type=F32)
            outs.append(acc / den)
        o_ref[...] = jnp.concatenate(outs, axis=0).T.astype(o_ref.dtype)

    few = (nblk - 1) // 2

    @pl.when(i <= few)
    def _():
        run(few)

    @pl.when(i > few)
    def _():
        run(nblk - 1)


def _attn_p(qt, kb, vtb, km, *, nb, t):
    da = qt.shape[1]
    nblk = t // MOBA_BLOCK
    hw = ATTN_HEADS * HEAD_DIM
    nhp = da // hw
    return pl.pallas_call(
        functools.partial(_attn_p_body, nblk=nblk),
        grid=(nb, nhp, nblk),
        in_specs=[pl.BlockSpec((1, hw, MOBA_BLOCK), lambda b, h, i: (b, h, i)),
                  pl.BlockSpec((t, hw), lambda b, h, i: (b, h)),
                  pl.BlockSpec((MOBA_BLOCK, hw), lambda b, h, i: (b * nblk + i, h)),
                  pl.BlockSpec((1, hw, t), lambda b, h, i: (b, h, 0)),
                  pl.BlockSpec((1, hw, MOBA_BLOCK), lambda b, h, i: (b, h, i)),
                  pl.BlockSpec((1, nblk, hw), lambda b, h, i: (b, 0, h))],
        out_specs=pl.BlockSpec((MOBA_BLOCK, hw), lambda b, h, i: (b * nblk + i, h)),
        out_shape=jax.ShapeDtypeStruct((nb * t, da), BF16),
        scratch_shapes=[pltpu.VMEM((ATTN_HEADS, nblk, MOBA_BLOCK), F32)],
        compiler_params=_cparams(("arbitrary", "arbitrary", "arbitrary")),
        name="attn_p",
    )(qt, kb, kb, vtb, vtb, km)


def _rg_gates(uc, wa_ref, ba_ref, wi_ref, bi_ref, lam_ref):
    r = _sigmoid(_dot3(uc, wa_ref[...]) + ba_ref[...])
    ig = _sigmoid(_dot3(uc, wi_ref[...]) + bi_ref[...])
    nl = -lam_ref[...]
    softplus = jnp.maximum(nl, 0.0) + jnp.log(1.0 + jnp.exp(-jnp.abs(nl)))
    log_a = -RG_C * r * softplus
    a = jnp.exp(log_a)
    bx = jnp.sqrt(1.0 - jnp.exp(2.0 * log_a)) * (ig * uc)
    return a, bx


def _rg_p_body(u_ref, g_ref, cw_ref, cb_ref, wa_ref, ba_ref, wi_ref, bi_ref, lam_ref,
               o_ref, hl_ref, cn_ref, ubuf, hc):
    j = pl.program_id(1)
    tt = u_ref.shape[0]
    pad = 8

    @pl.when(j == 0)
    def _():
        ubuf[0:pad] = jnp.zeros((pad, ubuf.shape[1]), F32)
        hc[...] = jnp.zeros_like(hc)

    u = u_ref[...]
    ubuf[pad:pad + tt] = u
    cw = cw_ref[...]
    kw = cw.shape[0]
    uc = cb_ref[...] + u * cw[kw - 1:kw]
    for jj in range(kw - 1):
        off = pad - (kw - 1) + jj
        uc = uc + ubuf[off:off + tt] * cw[jj:jj + 1]
    a, bx = _rg_gates(uc, wa_ref, ba_ref, wi_ref, bi_ref, lam_ref)

    row = lax.broadcasted_iota(I32, a.shape, 0)
    d = 1
    while d < tt:
        keep = row >= d
        a_sh = jnp.where(keep, pltpu.roll(a, d, 0), 1.0)
        b_sh = jnp.where(keep, pltpu.roll(bx, d, 0), 0.0)
        bx = a * b_sh + bx
        a = a * a_sh
        d *= 2
    h = bx + a * hc[...]
    hc[...] = h[tt - 1:tt]
    hl_ref[0] = h[tt - 1:tt]
    cn_ref[0] = u[tt - (kw - 1):tt]
    ubuf[0:pad] = u[tt - pad:tt]
    o_ref[...] = (h * _gelu_tanh(g_ref[...])).astype(o_ref.dtype)


def _rg_p(u, g, cw, cb, wa, ba, wi, bi, lam, *, nb, t):
    n, dr = u.shape
    tt = min(RG_TILE, t)
    nt = t // tt
    kw = cw.shape[0]
    full = lambda a: pl.BlockSpec(a.shape, lambda b, j: (0,) * a.ndim)
    tok = pl.BlockSpec((tt, dr), lambda b, j: (b * nt + j, 0))
    return pl.pallas_call(
        _rg_p_body,
        grid=(nb, nt),
        in_specs=[tok, tok, full(cw), full(cb), full(wa), full(ba), full(wi), full(bi), full(lam)],
        out_specs=[tok,
                   pl.BlockSpec((1, 1, dr), lambda b, j: (b, 0, 0)),
                   pl.BlockSpec((1, kw - 1, dr), lambda b, j: (b, 0, 0))],
        out_shape=[jax.ShapeDtypeStruct((n, dr), BF16),
                   jax.ShapeDtypeStruct((nb, 1, dr), F32),
                   jax.ShapeDtypeStruct((nb, kw - 1, dr), F32)],
        scratch_shapes=[pltpu.VMEM((8 + tt, dr), F32), pltpu.VMEM((1, dr), F32)],
        compiler_params=_cparams(("arbitrary", "arbitrary")),
        name="rg_p",
    )(u, g, cw, cb, wa, ba, wi, bi, lam)


def _rg_s_body(u_ref, g_ref, cs_ref, h0_ref, cw_ref, cb_ref, wa_ref, ba_ref, wi_ref, bi_ref, lam_ref,
               o_ref, hl_ref, cn_ref):
    ts = u_ref.shape[0]
    cw = cw_ref[...]
    kw = cw.shape[0]
    full = [cs_ref[jj] for jj in range(kw - 1)] + [u_ref[tt] for tt in range(ts)]
    h = h0_ref[...]
    for tt in range(ts):
        uc = cb_ref[...]
        for jj in range(kw):
            uc = uc + full[tt + jj] * cw[jj:jj + 1]
        a, bx = _rg_gates(uc, wa_ref, ba_ref, wi_ref, bi_ref, lam_ref)
        h = a * h + bx
        o_ref[tt] = h * _gelu_tanh(g_ref[tt])
    hl_ref[...] = h
    for jj in range(kw - 1):
        cn_ref[jj] = full[ts + jj]


def _rg_s(u, g, cs, h0, cw, cb, wa, ba, wi, bi, lam):
    ts, db, dr = u.shape
    kw = cw.shape[0]
    return pl.pallas_call(
        _rg_s_body,
        out_shape=[jax.ShapeDtypeStruct((ts, db, dr), F32),
                   jax.ShapeDtypeStruct((db, dr), F32),
                   jax.ShapeDtypeStruct((kw - 1, db, dr), F32)],
        compiler_params=pltpu.CompilerParams(vmem_limit_bytes=VMEM_LIMIT),
        name="rg_s",
    )(u, g, cs, h0, cw, cb, wa, ba, wi, bi, lam)


def _kmean_s_body(pt_ref, *refs, npg):
    del pt_ref
    pages, o_ref = refs[:npg], refs[npg]
    cols = []
    for n in range(npg // 2):
        blk = pages[2 * n][0] + pages[2 * n + 1][0]
        cols.append(jnp.sum(blk, axis=1, keepdims=True) * (1.0 / MOBA_BLOCK))
    o_ref[0, 0] = jnp.concatenate(cols, axis=1)


def _kmean_s(pt_flat, kt3, *, db, n_pages):
    _, hd, ps = kt3.shape
    npg = 16
    nchunk = n_pages // npg
    specs = [pl.BlockSpec((1, hd, ps), functools.partial(
        lambda b, c, pt, p: (pt[b * n_pages + c * npg + p], 0, 0), p=p)) for p in range(npg)]
    return pl.pallas_call(
        functools.partial(_kmean_s_body, npg=npg),
        grid_spec=pltpu.PrefetchScalarGridSpec(
            num_scalar_prefetch=1,
            grid=(db, nchunk),
            in_specs=specs,
            out_specs=pl.BlockSpec((1, 1, hd, npg // 2), lambda b, c, pt: (b, c, 0, 0)),
        ),
        out_shape=jax.ShapeDtypeStruct((db, nchunk, hd, npg // 2), F32),
        compiler_params=_cparams(("arbitrary", "arbitrary")),
        name="kmean_s",
    )(pt_flat, *([kt3] * npg))


def _sel_s_body(qt_ref, km_ref, o_ref, *, nh):
    qt = qt_ref[0]
    km = km_ref[0]
    ts = qt.shape[1]
    nc = km.shape[1]
    lane = lax.broadcasted_iota(I32, (nh, nc), 1).astype(F32)
    outs = []
    for t in range(ts):
        prod = qt[:, t:t + 1] * km
        s = jnp.sum(prod.reshape(nh, HEAD_DIM, nc), axis=1)
        for _ in range(MOBA_TOPK):
            m = jnp.max(s, axis=1, keepdims=True)
            idx = jnp.min(jnp.where(s == m, lane, float(nc)), axis=1, keepdims=True)
            outs.append(idx)
            s = jnp.where(lane == idx, -jnp.inf, s)
    o_ref[0] = jnp.concatenate(outs, axis=1).astype(I32)


def _sel_s(qt, km):
    db, hd, ts = qt.shape
    nc = km.shape[2]
    nh = hd // HEAD_DIM
    return pl.pallas_call(
        functools.partial(_sel_s_body, nh=nh),
        grid=(db,),
        in_specs=[pl.BlockSpec((1, hd, ts), lambda b: (b, 0, 0)),
                  pl.BlockSpec((1, hd, nc), lambda b: (b, 0, 0))],
        out_specs=pl.BlockSpec((1, nh, ts * MOBA_TOPK), lambda b: (b, 0, 0)),
        out_shape=jax.ShapeDtypeStruct((db, nh, ts * MOBA_TOPK), I32),
        compiler_params=_cparams(("arbitrary",)),
        name="sel_s",
    )(qt, km)


def _attn_s_body(idx_ref, pt_ref, qt_ref, knt_ref, vnt_ref, kc_ref, vc_ref, o_ref,
                 kbuf, vbuf, sem, *, nh, ts, n_pages, ppb):
    b = pl.program_id(0)
    nb = pl.num_programs(0)
    nsl = nh * ts * MOBA_TOPK
    ps = kc_ref.shape[3]

    def copies(bb, slot, s):
        h = s // (ts * MOBA_TOPK)
        blk = idx_ref[bb * nsl + s]
        out = []
        for pg in range(ppb):
            page = pt_ref[bb * n_pages + blk * ppb + pg]
            out.append(pltpu.make_async_copy(
                kc_ref.at[page, h], kbuf.at[slot, s, :, pl.ds(pg * ps, ps)], sem.at[slot]))
            out.append(pltpu.make_async_copy(
                vc_ref.at[page, h], vbuf.at[slot, s, :, pl.ds(pg * ps, ps)], sem.at[slot]))
        return out

    def start_all(bb, slot):
        def f(s, c):
            for cp in copies(bb, slot, s):
                cp.start()
            return c
        lax.fori_loop(0, nsl, f, 0)

    def wait_all(bb, slot):
        def f(s, c):
            for cp in copies(bb, slot, s):
                cp.wait()
            return c
        lax.fori_loop(0, nsl, f, 0)

    @pl.when(b == 0)
    def _():
        start_all(0, 0)

    @pl.when(b + 1 < nb)
    def _():
        start_all(b + 1, (b + 1) % 2)

    slot = b % 2
    wait_all(b, slot)

    scale = HEAD_DIM ** -0.5
    tcol = lax.broadcasted_iota(I32, (1, ts), 1)

    def head(h, c):
        r0 = pl.multiple_of(h * HEAD_DIM, HEAD_DIM)
        qh = qt_ref[0, pl.ds(r0, HEAD_DIM), :] * scale
        knh = knt_ref[0, pl.ds(r0, HEAD_DIM), :]
        vnh = vnt_ref[0, pl.ds(r0, HEAD_DIM), :]
        cols = []
        for t in range(ts):
            qc = qh[:, t:t + 1]
            s_new = jnp.sum(qc * knh, axis=0, keepdims=True)
            s_new = jnp.where(tcol <= t, s_new, -jnp.inf)
            s_sel = []
            for j in range(MOBA_TOPK):
                sl = (h * ts + t) * MOBA_TOPK + j
                s_sel.append(jnp.sum(qc * kbuf[slot, sl], axis=0, keepdims=True))
            m = jnp.max(s_new, axis=1, keepdims=True)
            for sj in s_sel:
                m = jnp.maximum(m, jnp.max(sj, axis=1, keepdims=True))
            p_new = jnp.exp(s_new - m)
            l = jnp.sum(p_new, axis=1, keepdims=True)
            acc = jnp.sum(p_new * vnh, axis=1, keepdims=True)
            for j in range(MOBA_TOPK):
                sl = (h * ts + t) * MOBA_TOPK + j
                pj = jnp.exp(s_sel[j] - m)
                l = l + jnp.sum(pj, axis=1, keepdims=True)
                acc = acc + jnp.sum(pj * vbuf[slot, sl], axis=1, keepdims=True)
            cols.append(acc / l)
        o_ref[0, pl.ds(r0, HEAD_DIM), :] = jnp.concatenate(cols, axis=1)
        return c

    lax.fori_loop(0, nh, head, 0)


def _attn_s(idx_flat, pt_flat, qt, knt, vnt, kc4, vc4, *, n_pages):
    db, hd, ts = qt.shape
    _, nh, dh, ps = kc4.shape
    ppb = MOBA_BLOCK // ps
    nsl = nh * ts * MOBA_TOPK
    vec = pl.BlockSpec((1, hd, ts), lambda b, idx, pt: (b, 0, 0))
    return pl.pallas_call(
        functools.partial(_attn_s_body, nh=nh, ts=ts, n_pages=n_pages, ppb=ppb),
        grid_spec=pltpu.PrefetchScalarGridSpec(
            num_scalar_prefetch=2,
            grid=(db,),
            in_specs=[vec, vec, vec, pl.BlockSpec(memory_space=pl.ANY), pl.BlockSpec(memory_space=pl.ANY)],
            out_specs=vec,
            scratch_shapes=[pltpu.VMEM((2, nsl, dh, MOBA_BLOCK), F32),
                            pltpu.VMEM((2, nsl, dh, MOBA_BLOCK), F32),
                            pltpu.SemaphoreType.DMA((2,))],
        ),
        out_shape=jax.ShapeDtypeStruct((db, hd, ts), F32),
        compiler_params=_cparams(("arbitrary",)),
        name="attn_s",
    )(idx_flat, pt_flat, qt, knt, vnt, kc4, vc4)


def _post_body(x_ref, sh1_ref, sc1_ref, gt1_ref, sh2_ref, sc2_ref, oa_ref, ob_ref,
               g1_ref, g2_ref, wm_ref, bm_ref, wpa_ref, wpb_ref, wo_ref, wr_ref, br_ref,
               x1_ref, tok_ref, gate_ref, cnt_ref):
    x = x_ref[...]
    d = x.shape[1]
    h = _norm_mod(x, g1_ref[...], sh1_ref[0], sc1_ref[0])
    gates = _sigmoid(_dot(h, wm_ref[...]) + bm_ref[...])
    merged = gates[:, :d] * _dot(oa_ref[...], wpa_ref[...]) + gates[:, d:] * _dot(ob_ref[...], wpb_ref[...])
    x1 = x + gt1_ref[0] * _dot(merged, wo_ref[...])
    x1_ref[...] = x1
    h2 = _norm_mod(x1, g2_ref[...], sh2_ref[0], sc2_ref[0])
    tok_ref[...] = h2
    logits = _dot3(h2, wr_ref[...]) + br_ref[...]
    ne = logits.shape[1]
    lane = lax.broadcasted_iota(I32, logits.shape, 1).astype(F32)
    rem = logits
    vals, hots = [], []
    for _ in range(TOPK_E):
        m = jnp.max(rem, axis=1, keepdims=True)
        idx = jnp.min(jnp.where(rem == m, lane, float(ne)), axis=1, keepdims=True)
        hot = lane == idx
        vals.append(m)
        hots.append(hot)
        rem = jnp.where(hot, -jnp.inf, rem)
    es = [jnp.exp(v - vals[0]) for v in vals]
    den = es[0]
    for e in es[1:]:
        den = den + e
    gate = jnp.zeros_like(logits)
    sel = jnp.zeros_like(logits)
    for e, hot in zip(es, hots):
        gate = gate + jnp.where(hot, e / den, 0.0)
        sel = sel + jnp.where(hot, 1.0, 0.0)
    gate_ref[...] = jnp.concatenate([gate, sel], axis=1)
    cnt_ref[0] = jnp.sum(sel, axis=0, keepdims=True)


def _post(x2d, mods, oa, ob, g1, g2, wm, bm, wpa, wpb, wo, wr, br, *, nb, t):
    n, d = x2d.shape
    tm = min(TOK_TILE, t)
    nt = t // tm
    ne = wr.shape[1]
    mrows = mods[0].shape[1]
    if mrows == 1:
        mod_spec = pl.BlockSpec((1, 1, d), lambda b, j: (b, 0, 0))
    else:
        mod_spec = pl.BlockSpec((1, tm, d), lambda b, j: (b, j, 0))
    full = lambda a: pl.BlockSpec(a.shape, lambda b, j: (0,) * a.ndim)
    tok = lambda w: pl.BlockSpec((tm, w), lambda b, j: (b * nt + j, 0))
    ws = (g1, g2, wm, bm, wpa, wpb, wo, wr, br)
    return pl.pallas_call(
        _post_body,
        grid=(nb, nt),
        in_specs=[tok(d)] + [mod_spec] * 5 + [tok(oa.shape[1]), tok(ob.shape[1])] + [full(w) for w in ws],
        out_specs=[tok(d), tok(d), tok(2 * ne),
                   pl.BlockSpec((1, 1, ne), lambda b, j: (b * nt + j, 0, 0))],
        out_shape=[jax.ShapeDtypeStruct((n, d), F32),
                   jax.ShapeDtypeStruct((n, d), F32),
                   jax.ShapeDtypeStruct((n, 2 * ne), F32),
                   jax.ShapeDtypeStruct((nb * nt, 1, ne), F32)],
        compiler_params=_cparams(("arbitrary", "arbitrary")),
        name="post",
    )(x2d, *mods, oa, ob, *ws)


def _route_body(gs_ref, base_ref, pos_ref, g4_ref):
    gs = gs_ref[...]
    ne = gs.shape[1] // 2
    gate, sel = gs[:, :ne], gs[:, ne:]
    tm = gs.shape[0]
    r = lax.broadcasted_iota(I32, (tm, tm), 0)
    c = lax.broadcasted_iota(I32, (tm, tm), 1)
    tri = jnp.where(c < r, 1.0, 0.0).astype(BF16)
    rank = jnp.dot(tri, sel.astype(BF16), preferred_element_type=F32)
    posm = base_ref[0] + rank
    lane = lax.broadcasted_iota(I32, sel.shape, 1).astype(F32)
    rem = sel > 0.0
    ps, gv = [], []
    for _ in range(TOPK_E):
        idx = jnp.min(jnp.where(rem, lane, float(ne)), axis=1, keepdims=True)
        hot = lane == idx
        ps.append(jnp.sum(jnp.where(hot, posm, 0.0), axis=1, keepdims=True))
        gv.append(jnp.sum(jnp.where(hot, gate, 0.0), axis=1, keepdims=True))
        rem = rem & jnp.logical_not(hot)
    pos_ref[...] = jnp.concatenate(ps, axis=1).astype(I32)
    g4_ref[...] = jnp.concatenate(gv, axis=1)


def _route(gs, base):
    n, ne2 = gs.shape
    tm = TOK_TILE
    return pl.pallas_call(
        _route_body,
        grid=(n // tm,),
        in_specs=[pl.BlockSpec((tm, ne2), lambda j: (j, 0)),
                  pl.BlockSpec((1, 1, ne2 // 2), lambda j: (j, 0, 0))],
        out_specs=[pl.BlockSpec((tm, TOPK_E), lambda j: (j, 0)),
                   pl.BlockSpec((tm, TOPK_E), lambda j: (j, 0))],
        out_shape=[jax.ShapeDtypeStruct((n, TOPK_E), I32),
                   jax.ShapeDtypeStruct((n, TOPK_E), F32)],
        compiler_params=_cparams(("arbitrary",)),
        name="route",
    )(gs, base)


def _dispatch_body(pos_ref, tok_ref, xs_in, xs_ref, sem):
    del xs_in
    tm = tok_ref.shape[0]

    def copy(r, k):
        return pltpu.make_async_copy(tok_ref.at[pl.ds(r, 1)], xs_ref.at[pl.ds(pos_ref[r * TOPK_E + k], 1)], sem)

    def issue(r, c):
        for k in range(TOPK_E):
            copy(r, k).start()
        return c

    def drain(r, c):
        for k in range(TOPK_E):
            copy(r, k).wait()
        return c

    lax.fori_loop(0, tm, issue, 0)
    lax.fori_loop(0, tm, drain, 0)


def _dispatch(pos_flat, tok, xs):
    n, d = tok.shape
    tm = ROW_TILE
    return pl.pallas_call(
        _dispatch_body,
        grid=(n // tm,),
        in_specs=[pl.BlockSpec((tm * TOPK_E,), lambda j: (j,), memory_space=pltpu.SMEM),
                  pl.BlockSpec((tm, d), lambda j: (j, 0)),
                  pl.BlockSpec(memory_space=pl.ANY)],
        out_specs=pl.BlockSpec(memory_space=pl.ANY),
        out_shape=jax.ShapeDtypeStruct(xs.shape, xs.dtype),
        scratch_shapes=[pltpu.SemaphoreType.DMA(())],
        input_output_aliases={2: 0},
        compiler_params=_cparams(("arbitrary",)),
        name="dispatch",
    )(pos_flat, tok, xs)


def _moe_body(te_ref, nt_ref, x_ref, wg_ref, wl_ref, bg_ref, bl_ref, wd_ref, bd_ref, o_ref):
    del te_ref

    @pl.when(pl.program_id(0) < nt_ref[0])
    def _():
        x = x_ref[...].astype(BF16)
        glu = jnp.minimum(jnp.dot(x, wg_ref[0], preferred_element_type=F32) + bg_ref[0], SWIGLU_LIMIT)
        lin = jnp.clip(jnp.dot(x, wl_ref[0], preferred_element_type=F32) + bl_ref[0], -SWIGLU_LIMIT, SWIGLU_LIMIT)
        act = glu * _sigmoid(SWIGLU_ALPHA * glu) * (lin + 1.0)
        o_ref[...] = jnp.dot(act.astype(BF16), wd_ref[0], preferred_element_type=F32) + bd_ref[0]

    @pl.when(pl.program_id(0) >= nt_ref[0])
    def _():
        o_ref[...] = jnp.zeros_like(o_ref)


def _moe(tile_expert, n_tiles, xs, wg, wl, bg, bl, wd, bd):
    r, d = xs.shape
    tm = MOE_TILE
    dff = wg.shape[2]
    row = lambda g, te, nt: (jnp.minimum(g, nt[0] - 1), 0)
    wsp = lambda a: pl.BlockSpec((1,) + a.shape[1:], lambda g, te, nt: (te[g], 0, 0))
    return pl.pallas_call(
        _moe_body,
        grid_spec=pltpu.PrefetchScalarGridSpec(
            num_scalar_prefetch=2,
            grid=(r // tm,),
            in_specs=[pl.BlockSpec((tm, d), row), wsp(wg), wsp(wl), wsp(bg), wsp(bl), wsp(wd), wsp(bd)],
            out_specs=pl.BlockSpec((tm, d), lambda g, te, nt: (g, 0)),
        ),
        out_shape=jax.ShapeDtypeStruct((r, d), F32),
        compiler_params=_cparams(("arbitrary",)),
        name="moe",
    )(tile_expert, n_tiles, xs, wg, wl, bg, bl, wd, bd)


def _combine_body(pos_ref, g4_ref, x1_ref, gt2_ref, nf_ref, ys_ref, y_ref, buf, sem):
    tm = x1_ref.shape[0]

    def copy(r, k):
        return pltpu.make_async_copy(ys_ref.at[pl.ds(pos_ref[r * TOPK_E + k], 1)], buf.at[k, pl.ds(r, 1)], sem)

    def issue(r, c):
        for k in range(TOPK_E):
            copy(r, k).start()
        return c

    def drain(r, c):
        for k in range(TOPK_E):
            copy(r, k).wait()
        return c

    lax.fori_loop(0, tm, issue, 0)
    lax.fori_loop(0, tm, drain, 0)
    g4 = g4_ref[...]
    ff = g4[:, 0:1] * buf[0]
    for k in range(1, TOPK_E):
        ff = ff + g4[:, k:k + 1] * buf[k]
    y_ref[...] = _rms(x1_ref[...] + gt2_ref[0] * ff, nf_ref[...])


def _combine(pos_flat, g4, x1, gt2, nf, ys, *, nb, t):
    n, d = x1.shape
    tm = min(ROW_TILE, t)
    nt = t // tm
    if gt2.shape[1] == 1:
        mod_spec = pl.BlockSpec((1, 1, d), lambda j: (j // nt, 0, 0))
    else:
        mod_spec = pl.BlockSpec((1, tm, d), lambda j: (j // nt, j % nt, 0))
    return pl.pallas_call(
        _combine_body,
        grid=(n // tm,),
        in_specs=[pl.BlockSpec((tm * TOPK_E,), lambda j: (j,), memory_space=pltpu.SMEM),
                  pl.BlockSpec((tm, TOPK_E), lambda j: (j, 0)),
                  pl.BlockSpec((tm, d), lambda j: (j, 0)),
                  mod_spec,
                  pl.BlockSpec((1, d), lambda j: (0, 0)),
                  pl.BlockSpec(memory_space=pl.ANY)],
        out_specs=pl.BlockSpec((tm, d), lambda j: (j, 0)),
        out_shape=jax.ShapeDtypeStruct((n, d), F32),
        scratch_shapes=[pltpu.VMEM((TOPK_E, tm, d), F32), pltpu.SemaphoreType.DMA(())],
        compiler_params=_cparams(("arbitrary",)),
        name="combine",
    )(pos_flat, g4, x1, gt2, nf, ys)


def _block_diag(w):
    g, a, b = w.shape
    eye = jnp.eye(g, dtype=w.dtype)
    return (eye[:, None, :, None] * w[:, :, None, :]).reshape(g * a, g * b)


def _layer0(a):
    return a.reshape(a.shape[1:])

def kernel(x_prompt, x_sample, cache_k, cache_v, state_h, state_conv, page_table, c_prompt, c_sample,
           w_ada, b_ada, norm1, norm2, w_in, conv_w, conv_b, w_rg_a, b_rg_a, w_rg_i, b_rg_i, rg_lambda,
           w_proj_a, w_proj_b, w_merge, b_merge, w_out, w_router, b_router, w_up, b_up, w_down, b_down, norm_f):
    assert w_ada.shape[0] == 1, "single trunk layer"
    bp, t, d = x_prompt.shape
    db, ts, _ = x_sample.shape
    _, n_phys, ps, nh, _ = cache_k.shape
    hd = nh * HEAD_DIM
    da = hd
    dr = conv_w.shape[2]
    n_pages = page_table.shape[1]
    ne = w_router.shape[2]
    np_, ns_ = bp * t, db * ts
    nblk = t // MOBA_BLOCK
    assert t % TOK_TILE == 0 and ns_ == TOK_TILE and n_pages % 16 == 0
    assert (n_pages * ps) % MOBA_BLOCK == 0 and (n_pages * ps) // MOBA_BLOCK >= MOBA_TOPK

    mod = _ada(jnp.concatenate([c_prompt, c_sample], axis=0), _layer0(w_ada), _layer0(b_ada))
    mod = mod.reshape(bp + db, N_MOD, d)
    mp = [mod[:bp, i][:, None, :] for i in range(N_MOD)]
    ms = [jnp.tile(mod[bp:, i], (ts, 1))[None] for i in range(N_MOD)]

    g1, g2, nf = norm1, norm2, norm_f[None]
    w_in0 = _layer0(w_in)
    wqkvt, wk, wug = w_in0[:, :3 * da].T, w_in0[:, da:2 * da], w_in0[:, 3 * da:]

    xp2 = x_prompt.reshape(np_, d)
    xs2 = jnp.swapaxes(x_sample, 0, 1).reshape(ns_, d)

    qt_p, kt_p, vt_p, u_p, g_p, kb_p, vtb_p, km_p = _proj(
        xp2, mp[0], mp[1], g1, wqkvt.astype(BF16), wug.astype(BF16), wk.astype(BF16), nb=bp, t=t)
    qt_s, kt_s, vt_s, u_s, g_s = _proj(xs2, ms[0], ms[1], g1, wqkvt, wug, nb=1, t=ns_)

    oa_p = _attn_p(qt_p, kb_p, vtb_p, km_p.reshape(bp, nblk, da), nb=bp, t=t)

    kc4 = jnp.transpose(cache_k.reshape(n_phys, ps, nh, HEAD_DIM), (0, 2, 3, 1))
    vc4 = jnp.transpose(cache_v.reshape(n_phys, ps, nh, HEAD_DIM), (0, 2, 3, 1))
    pt_flat = page_table.reshape(-1)
    km = _kmean_s(pt_flat, kc4.reshape(n_phys, hd, ps), db=db, n_pages=n_pages)
    km = jnp.transpose(km, (0, 2, 1, 3)).reshape(db, hd, -1)
    tmaj = lambda a: jnp.transpose(a.reshape(hd, ts, db), (2, 0, 1))
    qs3, kn3, vn3 = tmaj(qt_s), tmaj(kt_s), tmaj(vt_s)
    idx = _sel_s(qs3, km)
    ot_s = _attn_s(idx.reshape(-1), pt_flat, qs3, kn3, vn3, kc4, vc4, n_pages=n_pages)
    oa_s = jnp.transpose(ot_s, (2, 0, 1)).reshape(ns_, hd)

    cw, cb = _layer0(conv_w), conv_b
    wa, wi = _block_diag(_layer0(w_rg_a)), _block_diag(_layer0(w_rg_i))
    ba, bi, lam = b_rg_a, b_rg_i, rg_lambda
    ob_p, hl_p, cn_p = _rg_p(u_p, g_p, cw, cb, wa, ba, wi, bi, lam, nb=bp, t=t)
    ob_s, hl_s, cn_s = _rg_s(u_s.reshape(ts, db, dr), g_s.reshape(ts, db, dr),
                             jnp.swapaxes(_layer0(state_conv), 0, 1), _layer0(state_h),
                             cw, cb, wa, ba, wi, bi, lam)

    wsm = (g1, g2, _layer0(w_merge).astype(BF16), b_merge, _layer0(w_proj_a).astype(BF16),
           _layer0(w_proj_b).astype(BF16), _layer0(w_out).astype(BF16), _layer0(w_router), b_router)
    x1_p, tok_p, gs_p, cnt_p = _post(xp2, mp[:5], oa_p, ob_p, *wsm, nb=bp, t=t)
    x1_s, tok_s, gs_s, cnt_s = _post(xs2, ms[:5], oa_s.astype(BF16), ob_s.reshape(ns_, dr).astype(BF16), *wsm,
                                     nb=1, t=ns_)

    gs = jnp.concatenate([gs_p, gs_s], axis=0)
    cnt = jnp.concatenate([cnt_p, cnt_s], axis=0).reshape(-1, ne).astype(I32)
    tot = jnp.sum(cnt, axis=0)
    tiles_e = (tot + MOE_TILE - 1) // MOE_TILE
    tile_end = jnp.cumsum(tiles_e)
    poff = (tile_end - tiles_e) * MOE_TILE
    base = poff[None, :] + jnp.cumsum(cnt, axis=0) - cnt
    n_all = np_ + ns_
    g_max = (n_all * TOPK_E) // MOE_TILE + ne
    n_tiles = tile_end[-1:].astype(I32)
    tile_id = jnp.minimum(jnp.arange(g_max, dtype=I32), n_tiles - 1)
    te = jnp.sum((tile_end[None, :] <= tile_id[:, None]).astype(I32), axis=1)
    pos, g4 = _route(gs, base.astype(F32)[:, None, :])
    pos_flat = pos.reshape(-1)

    xs = jnp.zeros((g_max * MOE_TILE, d), F32)
    xs = _dispatch(pos_flat[:np_ * TOPK_E], tok_p, xs)
    xs = _dispatch(pos_flat[np_ * TOPK_E:], tok_s, xs)
    w_up0 = _layer0(w_up)
    b_up0 = _layer0(b_up)
    ys = _moe(te, n_tiles, xs,
              w_up0[:, :, 0::2].astype(BF16), w_up0[:, :, 1::2].astype(BF16),
              b_up0[:, None, 0::2], b_up0[:, None, 1::2],
              _layer0(w_down).astype(BF16), _layer0(b_down)[:, None, :])
    y_p = _combine(pos_flat[:np_ * TOPK_E], g4[:np_], x1_p, mp[5], nf, ys, nb=bp, t=t)
    y_s = _combine(pos_flat[np_ * TOPK_E:], g4[np_:], x1_s, ms[5], nf, ys, nb=1, t=ns_)

    y_prompt = y_p.reshape(bp, t, d)
    y_sample = jnp.swapaxes(y_s.reshape(ts, db, d), 0, 1)
    to5 = lambda a: jnp.transpose(a.reshape(bp, nh, HEAD_DIM, t), (0, 3, 1, 2))[None]
    k_prompt, v_prompt = to5(kt_p), to5(vt_p)
    to5s = lambda a: jnp.transpose(a.reshape(nh, HEAD_DIM, ts, db), (3, 2, 0, 1))[None]
    k_sample, v_sample = to5s(kt_s), to5s(vt_s)
    h_prompt = hl_p.reshape(1, bp, dr)
    conv_prompt = cn_p[None]
    h_sample = hl_s[None]
    conv_sample = jnp.swapaxes(cn_s, 0, 1)[None]
    return (y_prompt, y_sample, k_prompt, v_prompt, h_prompt, conv_prompt,
            k_sample, v_sample, h_sample, conv_sample)
```

```python
import functools

import jax
import jax.numpy as jnp
from jax import lax
from jax.experimental import pallas as pl
from jax.experimental.pallas import tpu as pltpu

F32 = jnp.float32
BF16 = jnp.bfloat16
I32 = jnp.int32

HEAD_DIM = 64
MOBA_BLOCK = 256
MOBA_TOPK = 3
RG_C = 8.0
TOPK_E = 4
SWIGLU_ALPHA = 1.702
SWIGLU_LIMIT = 7.0
EPS = 1e-6
N_MOD = 6

LANES = 128
TOK_TILE = 512
ATTN_HEADS = 4
RG_TILE = 256
MOE_TILE = 512
ROW_TILE = 256
VMEM_LIMIT = 56 * 1024 * 1024


def _cparams(sem):
    return pltpu.CompilerParams(dimension_semantics=sem, vmem_limit_bytes=VMEM_LIMIT)


def _dot(a, b):
    return jnp.dot(a.astype(BF16), b.astype(BF16), preferred_element_type=F32)


def _dot_t(a, b):
    return lax.dot_general(a.astype(BF16), b.astype(BF16), (((1,), (1,)), ((), ())),
                           preferred_element_type=F32)


def _split(a):
    hi = a.astype(BF16)
    lo = (a - hi.astype(F32)).astype(BF16)
    return hi, lo


def _dot3(a, b):
    ah, al = _split(a)
    bh, bl = _split(b)
    return (jnp.dot(ah, bh, preferred_element_type=F32)
            + jnp.dot(al, bh, preferred_element_type=F32)
            + jnp.dot(ah, bl, preferred_element_type=F32))


def _dot3_t(a, b):
    ah, al = _split(a)
    bh, bl = _split(b)
    dn = (((1,), (1,)), ((), ()))
    return (lax.dot_general(ah, bh, dn, preferred_element_type=F32)
            + lax.dot_general(al, bh, dn, preferred_element_type=F32)
            + lax.dot_general(ah, bl, dn, preferred_element_type=F32))


def _sigmoid(x):
    return 1.0 / (1.0 + jnp.exp(-x))


def _rms(x, g):
    return x * lax.rsqrt(jnp.mean(x * x, axis=-1, keepdims=True) + EPS) * g


def _norm_mod(x, g, shift, scale):
    return _rms(x, g) * (1.0 + scale) + shift


def _gelu_tanh(x):
    return 0.5 * x * (1.0 + jnp.tanh(0.7978845608028654 * (x + 0.044715 * (x * x * x))))


def _ada_body(c_ref, w_ref, b_ref, o_ref):
    c = c_ref[...]
    o_ref[...] = _dot3(c * _sigmoid(c), w_ref[...]) + b_ref[...]


def _ada(c, w, b):
    n, d = c.shape
    e = w.shape[1]
    te = 1024
    return pl.pallas_call(
        _ada_body,
        grid=(e // te,),
        in_specs=[pl.BlockSpec((n, d), lambda j: (0, 0)),
                  pl.BlockSpec((d, te), lambda j: (0, j)),
                  pl.BlockSpec((1, te), lambda j: (0, j))],
        out_specs=pl.BlockSpec((n, te), lambda j: (0, j)),
        out_shape=jax.ShapeDtypeStruct((n, e), F32),
        compiler_params=_cparams(("arbitrary",)),
        name="ada",
    )(c, w, b.reshape(1, e))


def _proj_body(x_ref, sh_ref, sc_ref, g1_ref, wqkvt_ref, wug_ref, *rest, prompt):
    h = _norm_mod(x_ref[...], g1_ref[...], sh_ref[0], sc_ref[0])
    mm, mmt = (_dot, _dot_t) if prompt else (_dot3, _dot3_t)
    if prompt:
        wk_ref, qt_ref, kt_ref, vt_ref, u_ref, g_ref, kb_ref, vtb_ref, km_ref = rest
    else:
        qt_ref, kt_ref, vt_ref, u_ref, g_ref = rest
    da = qt_ref.shape[1]
    qkvt = mmt(wqkvt_ref[...], h)
    qt_ref[0] = qkvt[:da]
    kt_ref[0] = qkvt[da:2 * da]
    vt = qkvt[2 * da:]
    vt_ref[0] = vt
    ug = mm(h, wug_ref[...])
    dr = u_ref.shape[1]
    u_ref[...] = ug[:, :dr]
    g_ref[...] = ug[:, dr:]
    if prompt:
        k = mm(h, wk_ref[...])
        kb_ref[...] = k.astype(BF16)
        vtb_ref[0] = vt.astype(BF16)
        means = [jnp.sum(k[n * MOBA_BLOCK:(n + 1) * MOBA_BLOCK], axis=0, keepdims=True) * (1.0 / MOBA_BLOCK)
                 for n in range(k.shape[0] // MOBA_BLOCK)]
        km_ref[0] = jnp.concatenate(means, axis=0)


def _proj(x2d, shift, scale, g1, wqkvt, wug, wk=None, *, nb, t):
    prompt = wk is not None
    n, d = x2d.shape
    tm = min(TOK_TILE, t)
    nt = t // tm
    da = wqkvt.shape[0] // 3
    dr = wug.shape[1] // 2
    mrows = shift.shape[1]
    if mrows == 1:
        mod_spec = pl.BlockSpec((1, 1, d), lambda b, j: (b, 0, 0))
    else:
        mod_spec = pl.BlockSpec((1, tm, d), lambda b, j: (b, j, 0))
    full = lambda a: pl.BlockSpec(a.shape, lambda b, j: (0,) * a.ndim)
    tok = lambda w: pl.BlockSpec((tm, w), lambda b, j: (b * nt + j, 0))
    featmaj = pl.BlockSpec((1, da, tm), lambda b, j: (b, 0, j))
    in_specs = [tok(d), mod_spec, mod_spec, full(g1), full(wqkvt), full(wug)]
    out_specs = [featmaj, featmaj, featmaj, tok(dr), tok(dr)]
    out_shape = [jax.ShapeDtypeStruct((nb, da, t), F32)] * 3 + [jax.ShapeDtypeStruct((n, dr), F32)] * 2
    args = [x2d, shift, scale, g1, wqkvt, wug]
    if prompt:
        bpt = tm // MOBA_BLOCK
        in_specs.append(full(wk))
        args.append(wk)
        out_specs += [tok(da), featmaj,
                      pl.BlockSpec((1, bpt, da), lambda b, j: (b * nt + j, 0, 0))]
        out_shape += [jax.ShapeDtypeStruct((n, da), BF16),
                      jax.ShapeDtypeStruct((nb, da, t), BF16),
                      jax.ShapeDtypeStruct((nb * nt, bpt, da), F32)]
    return pl.pallas_call(
        functools.partial(_proj_body, prompt=prompt),
        grid=(nb, nt),
        in_specs=in_specs,
        out_specs=out_specs,
        out_shape=out_shape,
        compiler_params=_cparams(("arbitrary", "arbitrary")),
        name="proj_p" if prompt else "proj_s",
    )(*args)


def _attn_p_body(qt_ref, kb_ref, kd_ref, vtb_ref, vd_ref, km_ref, o_ref, sel_ref, *, nblk):
    i = pl.program_id(2)
    bs = MOBA_BLOCK
    krow = lax.broadcasted_iota(I32, (bs, bs), 0)
    qcol = lax.broadcasted_iota(I32, (bs, bs), 1)
    blk = lax.broadcasted_iota(I32, (nblk, bs), 0)
    valid = blk < i
    nhh = qt_ref.shape[1] // HEAD_DIM
    heads = [slice(hh * HEAD_DIM, (hh + 1) * HEAD_DIM) for hh in range(nhh)]

    def run(npast):
        outs = []
        for hh, hs in enumerate(heads):
            qt = qt_ref[0, hs, :]
            sv = jnp.where(valid, _dot3(km_ref[0, :, hs], qt), -jnp.inf)
            cnt = jnp.zeros_like(sv)
            for m in range(nblk - 1):
                sm = sv[m:m + 1, :]
                beats = (sm > sv) | ((sm == sv) & (m < blk))
                cnt = cnt + jnp.where(beats, 1.0, 0.0)
            sel_ref[hh] = jnp.where(valid & (cnt < MOBA_TOPK), 1.0, 0.0)

            qb = (qt * (HEAD_DIM ** -0.5)).astype(BF16)
            sd = jnp.dot(kd_ref[:, hs], qb, preferred_element_type=F32)
            sd = jnp.where(krow <= qcol, sd, -jnp.inf)
            sp = jnp.dot(kb_ref[0:npast * bs, hs], qb, preferred_element_type=F32)
            sps = [jnp.where(sel_ref[hh, j:j + 1, :] > 0.0, sp[j * bs:(j + 1) * bs], -jnp.inf)
                   for j in range(npast)]
            mx = jnp.max(sd, axis=0, keepdims=True)
            for sj in sps:
                mx = jnp.maximum(mx, jnp.max(sj, axis=0, keepdims=True))
            pd = jnp.exp(sd - mx)
            den = jnp.sum(pd, axis=0, keepdims=True)
            pps = []
            for sj in sps:
                pj = jnp.exp(sj - mx)
                den = den + jnp.sum(pj, axis=0, keepdims=True)
                pps.append(pj.astype(BF16))
            acc = jnp.dot(vd_ref[0, hs, :], pd.astype(BF16), preferred_element_type=F32)
            acc = acc + jnp.dot(vtb_ref[0, hs, 0:npast * bs], jnp.concatenate(pps, axis=0),
                                preferred_element_type=F32)
            outs.append(acc / den)
        o_ref[...] = jnp.concatenate(outs, axis=0).T.astype(o_ref.dtype)

    few = (nblk - 1) // 2

    @pl.when(i <= few)
    def _():
        run(few)

    @pl.when(i > few)
    def _():
        run(nblk - 1)


def _attn_p(qt, kb, vtb, km, *, nb, t):
    da = qt.shape[1]
    nblk = t // MOBA_BLOCK
    hw = ATTN_HEADS * HEAD_DIM
    nhp = da // hw
    return pl.pallas_call(
        functools.partial(_attn_p_body, nblk=nblk),
        grid=(nb, nhp, nblk),
        in_specs=[pl.BlockSpec((1, hw, MOBA_BLOCK), lambda b, h, i: (b, h, i)),
                  pl.BlockSpec((t, hw), lambda b, h, i: (b, h)),
                  pl.BlockSpec((MOBA_BLOCK, hw), lambda b, h, i: (b * nblk + i, h)),
                  pl.BlockSpec((1, hw, t), lambda b, h, i: (b, h, 0)),
                  pl.BlockSpec((1, hw, MOBA_BLOCK), lambda b, h, i: (b, h, i)),
                  pl.BlockSpec((1, nblk, hw), lambda b, h, i: (b, 0, h))],
        out_specs=pl.BlockSpec((MOBA_BLOCK, hw), lambda b, h, i: (b * nblk + i, h)),
        out_shape=jax.ShapeDtypeStruct((nb * t, da), BF16),
        scratch_shapes=[pltpu.VMEM((ATTN_HEADS, nblk, MOBA_BLOCK), F32)],
        compiler_params=_cparams(("arbitrary", "arbitrary", "arbitrary")),
        name="attn_p",
    )(qt, kb, kb, vtb, vtb, km)


def _rg_gates(uc, wa_ref, ba_ref, wi_ref, bi_ref, lam_ref):
    r = _sigmoid(_dot3(uc, wa_ref[...]) + ba_ref[...])
    ig = _sigmoid(_dot3(uc, wi_ref[...]) + bi_ref[...])
    nl = -lam_ref[...]
    softplus = jnp.maximum(nl, 0.0) + jnp.log(1.0 + jnp.exp(-jnp.abs(nl)))
    log_a = -RG_C * r * softplus
    a = jnp.exp(log_a)
    bx = jnp.sqrt(1.0 - jnp.exp(2.0 * log_a)) * (ig * uc)
    return a, bx


def _rg_p_body(u_ref, g_ref, cw_ref, cb_ref, wa_ref, ba_ref, wi_ref, bi_ref, lam_ref,
               o_ref, hl_ref, cn_ref, ubuf, hc):
    j = pl.program_id(1)
    tt = u_ref.shape[0]
    pad = 8

    @pl.when(j == 0)
    def _():
        ubuf[0:pad] = jnp.zeros((pad, ubuf.shape[1]), F32)
        hc[...] = jnp.zeros_like(hc)

    u = u_ref[...]
    ubuf[pad:pad + tt] = u
    cw = cw_ref[...]
    kw = cw.shape[0]
    uc = cb_ref[...] + u * cw[kw - 1:kw]
    for jj in range(kw - 1):
        off = pad - (kw - 1) + jj
        uc = uc + ubuf[off:off + tt] * cw[jj:jj + 1]
    a, bx = _rg_gates(uc, wa_ref, ba_ref, wi_ref, bi_ref, lam_ref)

    row = lax.broadcasted_iota(I32, a.shape, 0)
    d = 1
    while d < tt:
        keep = row >= d
        a_sh = jnp.where(keep, pltpu.roll(a, d, 0), 1.0)
        b_sh = jnp.where(keep, pltpu.roll(bx, d, 0), 0.0)
        bx = a * b_sh + bx
        a = a * a_sh
        d *= 2
    h = bx + a * hc[...]
    hc[...] = h[tt - 1:tt]
    hl_ref[0] = h[tt - 1:tt]
    cn_ref[0] = u[tt - (kw - 1):tt]
    ubuf[0:pad] = u[tt - pad:tt]
    o_ref[...] = (h * _gelu_tanh(g_ref[...])).astype(o_ref.dtype)


def _rg_p(u, g, cw, cb, wa, ba, wi, bi, lam, *, nb, t):
    n, dr = u.shape
    tt = min(RG_TILE, t)
    nt = t // tt
    kw = cw.shape[0]
    full = lambda a: pl.BlockSpec(a.shape, lambda b, j: (0,) * a.ndim)
    tok = pl.BlockSpec((tt, dr), lambda b, j: (b * nt + j, 0))
    return pl.pallas_call(
        _rg_p_body,
        grid=(nb, nt),
        in_specs=[tok, tok, full(cw), full(cb), full(wa), full(ba), full(wi), full(bi), full(lam)],
        out_specs=[tok,
                   pl.BlockSpec((1, 1, dr), lambda b, j: (b, 0, 0)),
                   pl.BlockSpec((1, kw - 1, dr), lambda b, j: (b, 0, 0))],
        out_shape=[jax.ShapeDtypeStruct((n, dr), BF16),
                   jax.ShapeDtypeStruct((nb, 1, dr), F32),
                   jax.ShapeDtypeStruct((nb, kw - 1, dr), F32)],
        scratch_shapes=[pltpu.VMEM((8 + tt, dr), F32), pltpu.VMEM((1, dr), F32)],
        compiler_params=_cparams(("arbitrary", "arbitrary")),
        name="rg_p",
    )(u, g, cw, cb, wa, ba, wi, bi, lam)


def _rg_s_body(u_ref, g_ref, cs_ref, h0_ref, cw_ref, cb_ref, wa_ref, ba_ref, wi_ref, bi_ref, lam_ref,
               o_ref, hl_ref, cn_ref):
    ts = u_ref.shape[0]
    cw = cw_ref[...]
    kw = cw.shape[0]
    full = [cs_ref[jj] for jj in range(kw - 1)] + [u_ref[tt] for tt in range(ts)]
    h = h0_ref[...]
    for tt in range(ts):
        uc = cb_ref[...]
        for jj in range(kw):
            uc = uc + full[tt + jj] * cw[jj:jj + 1]
        a, bx = _rg_gates(uc, wa_ref, ba_ref, wi_ref, bi_ref, lam_ref)
        h = a * h + bx
        o_ref[tt] = h * _gelu_tanh(g_ref[tt])
    hl_ref[...] = h
    for jj in range(kw - 1):
        cn_ref[jj] = full[ts + jj]


def _rg_s(u, g, cs, h0, cw, cb, wa, ba, wi, bi, lam):
    ts, db, dr = u.shape
    kw = cw.shape[0]
    return pl.pallas_call(
        _rg_s_body,
        out_shape=[jax.ShapeDtypeStruct((ts, db, dr), F32),
                   jax.ShapeDtypeStruct((db, dr), F32),
                   jax.ShapeDtypeStruct((kw - 1, db, dr), F32)],
        compiler_params=pltpu.CompilerParams(vmem_limit_bytes=VMEM_LIMIT),
        name="rg_s",
    )(u, g, cs, h0, cw, cb, wa, ba, wi, bi, lam)


def _kmean_s_body(pt_ref, *refs, npg):
    del pt_ref
    pages, o_ref = refs[:npg], refs[npg]
    cols = []
    for n in range(npg // 2):
        blk = pages[2 * n][0] + pages[2 * n + 1][0]
        cols.append(jnp.sum(blk, axis=1, keepdims=True) * (1.0 / MOBA_BLOCK))
    o_ref[0, 0] = jnp.concatenate(cols, axis=1)


def _kmean_s(pt_flat, kt3, *, db, n_pages):
    _, hd, ps = kt3.shape
    npg = 16
    nchunk = n_pages // npg
    specs = [pl.BlockSpec((1, hd, ps), functools.partial(
        lambda b, c, pt, p: (pt[b * n_pages + c * npg + p], 0, 0), p=p)) for p in range(npg)]
    return pl.pallas_call(
        functools.partial(_kmean_s_body, npg=npg),
        grid_spec=pltpu.PrefetchScalarGridSpec(
            num_scalar_prefetch=1,
            grid=(db, nchunk),
            in_specs=specs,
            out_specs=pl.BlockSpec((1, 1, hd, npg // 2), lambda b, c, pt: (b, c, 0, 0)),
        ),
        out_shape=jax.ShapeDtypeStruct((db, nchunk, hd, npg // 2), F32),
        compiler_params=_cparams(("arbitrary", "arbitrary")),
        name="kmean_s",
    )(pt_flat, *([kt3] * npg))


def _sel_s_body(qt_ref, km_ref, o_ref, *, nh):
    qt = qt_ref[0]
    km = km_ref[0]
    ts = qt.shape[1]
    nc = km.shape[1]
    lane = lax.broadcasted_iota(I32, (nh, nc), 1).astype(F32)
    outs = []
    for t in range(ts):
        prod = qt[:, t:t + 1] * km
        s = jnp.sum(prod.reshape(nh, HEAD_DIM, nc), axis=1)
        for _ in range(MOBA_TOPK):
            m = jnp.max(s, axis=1, keepdims=True)
            idx = jnp.min(jnp.where(s == m, lane, float(nc)), axis=1, keepdims=True)
            outs.append(idx)
            s = jnp.where(lane == idx, -jnp.inf, s)
    o_ref[0] = jnp.concatenate(outs, axis=1).astype(I32)


def _sel_s(qt, km):
    db, hd, ts = qt.shape
    nc = km.shape[2]
    nh = hd // HEAD_DIM
    return pl.pallas_call(
        functools.partial(_sel_s_body, nh=nh),
        grid=(db,),
        in_specs=[pl.BlockSpec((1, hd, ts), lambda b: (b, 0, 0)),
                  pl.BlockSpec((1, hd, nc), lambda b: (b, 0, 0))],
        out_specs=pl.BlockSpec((1, nh, ts * MOBA_TOPK), lambda b: (b, 0, 0)),
        out_shape=jax.ShapeDtypeStruct((db, nh, ts * MOBA_TOPK), I32),
        compiler_params=_cparams(("arbitrary",)),
        name="sel_s",
    )(qt, km)


def _attn_s_body(idx_ref, pt_ref, qt_ref, knt_ref, vnt_ref, kc_ref, vc_ref, o_ref,
                 kbuf, vbuf, sem, *, nh, ts, n_pages, ppb):
    b = pl.program_id(0)
    nb = pl.num_programs(0)
    nsl = nh * ts * MOBA_TOPK
    ps = kc_ref.shape[3]

    def copies(bb, slot, s):
        h = s // (ts * MOBA_TOPK)
        blk = idx_ref[bb * nsl + s]
        out = []
        for pg in range(ppb):
            page = pt_ref[bb * n_pages + blk * ppb + pg]
            out.append(pltpu.make_async_copy(
                kc_ref.at[page, h], kbuf.at[slot, s, :, pl.ds(pg * ps, ps)], sem.at[slot]))
            out.append(pltpu.make_async_copy(
                vc_ref.at[page, h], vbuf.at[slot, s, :, pl.ds(pg * ps, ps)], sem.at[slot]))
        return out

    def start_all(bb, slot):
        def f(s, c):
            for cp in copies(bb, slot, s):
                cp.start()
            return c
        lax.fori_loop(0, nsl, f, 0)

    def wait_all(bb, slot):
        def f(s, c):
            for cp in copies(bb, slot, s):
                cp.wait()
            return c
        lax.fori_loop(0, nsl, f, 0)

    @pl.when(b == 0)
    def _():
        start_all(0, 0)

    @pl.when(b + 1 < nb)
    def _():
        start_all(b + 1, (b + 1) % 2)

    slot = b % 2
    wait_all(b, slot)

    scale = HEAD_DIM ** -0.5
    tcol = lax.broadcasted_iota(I32, (1, ts), 1)

    def head(h, c):
        r0 = pl.multiple_of(h * HEAD_DIM, HEAD_DIM)
        qh = qt_ref[0, pl.ds(r0, HEAD_DIM), :] * scale
        knh = knt_ref[0, pl.ds(r0, HEAD_DIM), :]
        vnh = vnt_ref[0, pl.ds(r0, HEAD_DIM), :]
        cols = []
        for t in range(ts):
            qc = qh[:, t:t + 1]
            s_new = jnp.sum(qc * knh, axis=0, keepdims=True)
            s_new = jnp.where(tcol <= t, s_new, -jnp.inf)
            s_sel = []
            for j in range(MOBA_TOPK):
                sl = (h * ts + t) * MOBA_TOPK + j
                s_sel.append(jnp.sum(qc * kbuf[slot, sl], axis=0, keepdims=True))
            m = jnp.max(s_new, axis=1, keepdims=True)
            for sj in s_sel:
                m = jnp.maximum(m, jnp.max(sj, axis=1, keepdims=True))
            p_new = jnp.exp(s_new - m)
            l = jnp.sum(p_new, axis=1, keepdims=True)
            acc = jnp.sum(p_new * vnh, axis=1, keepdims=True)
            pv = None
            for j in range(MOBA_TOPK):
                sl = (h * ts + t) * MOBA_TOPK + j
                pj = jnp.exp(s_sel[j] - m)
                l = l + jnp.sum(pj, axis=1, keepdims=True)
                term = pj * vbuf[slot, sl]
                pv = term if pv is None else pv + term
            acc = acc + jnp.sum(pv, axis=1, keepdims=True)
            cols.append(acc / l)
        o_ref[0, pl.ds(r0, HEAD_DIM), :] = jnp.concatenate(cols, axis=1)
        return c

    lax.fori_loop(0, nh, head, 0)


def _attn_s(idx_flat, pt_flat, qt, knt, vnt, kc4, vc4, *, n_pages):
    db, hd, ts = qt.shape
    _, nh, dh, ps = kc4.shape
    ppb = MOBA_BLOCK // ps
    nsl = nh * ts * MOBA_TOPK
    vec = pl.BlockSpec((1, hd, ts), lambda b, idx, pt: (b, 0, 0))
    return pl.pallas_call(
        functools.partial(_attn_s_body, nh=nh, ts=ts, n_pages=n_pages, ppb=ppb),
        grid_spec=pltpu.PrefetchScalarGridSpec(
            num_scalar_prefetch=2,
            grid=(db,),
            in_specs=[vec, vec, vec, pl.BlockSpec(memory_space=pl.ANY), pl.BlockSpec(memory_space=pl.ANY)],
            out_specs=vec,
            scratch_shapes=[pltpu.VMEM((2, nsl, dh, MOBA_BLOCK), F32),
                            pltpu.VMEM((2, nsl, dh, MOBA_BLOCK), F32),
                            pltpu.SemaphoreType.DMA((2,))],
        ),
        out_shape=jax.ShapeDtypeStruct((db, hd, ts), F32),
        compiler_params=_cparams(("arbitrary",)),
        name="attn_s",
    )(idx_flat, pt_flat, qt, knt, vnt, kc4, vc4)


def _post_body(x_ref, sh1_ref, sc1_ref, gt1_ref, sh2_ref, sc2_ref, oa_ref, ob_ref,
               g1_ref, g2_ref, wm_ref, bm_ref, wpa_ref, wpb_ref, wo_ref, wr_ref, br_ref,
               x1_ref, tok_ref, gate_ref, cnt_ref):
    x = x_ref[...]
    d = x.shape[1]
    h = _norm_mod(x, g1_ref[...], sh1_ref[0], sc1_ref[0])
    gates = _sigmoid(_dot(h, wm_ref[...]) + bm_ref[...])
    merged = gates[:, :d] * _dot(oa_ref[...], wpa_ref[...]) + gates[:, d:] * _dot(ob_ref[...], wpb_ref[...])
    x1 = x + gt1_ref[0] * _dot(merged, wo_ref[...])
    x1_ref[...] = x1
    h2 = _norm_mod(x1, g2_ref[...], sh2_ref[0], sc2_ref[0])
    tok_ref[...] = h2
    logits = _dot3(h2, wr_ref[...]) + br_ref[...]
    ne = logits.shape[1]
    lane = lax.broadcasted_iota(I32, logits.shape, 1).astype(F32)
    rem = logits
    vals, hots = [], []
    for _ in range(TOPK_E):
        m = jnp.max(rem, axis=1, keepdims=True)
        idx = jnp.min(jnp.where(rem == m, lane, float(ne)), axis=1, keepdims=True)
        hot = lane == idx
        vals.append(m)
        hots.append(hot)
        rem = jnp.where(hot, -jnp.inf, rem)
    es = [jnp.exp(v - vals[0]) for v in vals]
    den = es[0]
    for e in es[1:]:
        den = den + e
    gate = jnp.zeros_like(logits)
    sel = jnp.zeros_like(logits)
    for e, hot in zip(es, hots):
        gate = gate + jnp.where(hot, e / den, 0.0)
        sel = sel + jnp.where(hot, 1.0, 0.0)
    gate_ref[...] = jnp.concatenate([gate, sel], axis=1)
    cnt_ref[0] = jnp.sum(sel, axis=0, keepdims=True)


def _post(x2d, mods, oa, ob, g1, g2, wm, bm, wpa, wpb, wo, wr, br, *, nb, t):
    n, d = x2d.shape
    tm = min(TOK_TILE, t)
    nt = t // tm
    ne = wr.shape[1]
    mrows = mods[0].shape[1]
    if mrows == 1:
        mod_spec = pl.BlockSpec((1, 1, d), lambda b, j: (b, 0, 0))
    else:
        mod_spec = pl.BlockSpec((1, tm, d), lambda b, j: (b, j, 0))
    full = lambda a: pl.BlockSpec(a.shape, lambda b, j: (0,) * a.ndim)
    tok = lambda w: pl.BlockSpec((tm, w), lambda b, j: (b * nt + j, 0))
    ws = (g1, g2, wm, bm, wpa, wpb, wo, wr, br)
    return pl.pallas_call(
        _post_body,
        grid=(nb, nt),
        in_specs=[tok(d)] + [mod_spec] * 5 + [tok(oa.shape[1]), tok(ob.shape[1])] + [full(w) for w in ws],
        out_specs=[tok(d), tok(d), tok(2 * ne),
                   pl.BlockSpec((1, 1, ne), lambda b, j: (b * nt + j, 0, 0))],
        out_shape=[jax.ShapeDtypeStruct((n, d), F32),
                   jax.ShapeDtypeStruct((n, d), F32),
                   jax.ShapeDtypeStruct((n, 2 * ne), F32),
                   jax.ShapeDtypeStruct((nb * nt, 1, ne), F32)],
        compiler_params=_cparams(("arbitrary", "arbitrary")),
        name="post",
    )(x2d, *mods, oa, ob, *ws)


def _route_body(gs_ref, base_ref, pos_ref, g4_ref):
    gs = gs_ref[...]
    ne = gs.shape[1] // 2
    gate, sel = gs[:, :ne], gs[:, ne:]
    tm = gs.shape[0]
    r = lax.broadcasted_iota(I32, (tm, tm), 0)
    c = lax.broadcasted_iota(I32, (tm, tm), 1)
    tri = jnp.where(c < r, 1.0, 0.0).astype(BF16)
    rank = jnp.dot(tri, sel.astype(BF16), preferred_element_type=F32)
    posm = base_ref[0] + rank
    lane = lax.broadcasted_iota(I32, sel.shape, 1).astype(F32)
    rem = sel > 0.0
    ps, gv = [], []
    for _ in range(TOPK_E):
        idx = jnp.min(jnp.where(rem, lane, float(ne)), axis=1, keepdims=True)
        hot = lane == idx
        ps.append(jnp.sum(jnp.where(hot, posm, 0.0), axis=1, keepdims=True))
        gv.append(jnp.sum(jnp.where(hot, gate, 0.0), axis=1, keepdims=True))
        rem = rem & jnp.logical_not(hot)
    pos_ref[...] = jnp.concatenate(ps, axis=1).astype(I32)
    g4_ref[...] = jnp.concatenate(gv, axis=1)


def _route(gs, base):
    n, ne2 = gs.shape
    tm = TOK_TILE
    return pl.pallas_call(
        _route_body,
        grid=(n // tm,),
        in_specs=[pl.BlockSpec((tm, ne2), lambda j: (j, 0)),
                  pl.BlockSpec((1, 1, ne2 // 2), lambda j: (j, 0, 0))],
        out_specs=[pl.BlockSpec((tm, TOPK_E), lambda j: (j, 0)),
                   pl.BlockSpec((tm, TOPK_E), lambda j: (j, 0))],
        out_shape=[jax.ShapeDtypeStruct((n, TOPK_E), I32),
                   jax.ShapeDtypeStruct((n, TOPK_E), F32)],
        compiler_params=_cparams(("arbitrary",)),
        name="route",
    )(gs, base)


def _dispatch_body(pos_ref, tok_ref, xs_in, xs_ref, sem):
    del xs_in
    tm = tok_ref.shape[0]

    def copy(r, k):
        return pltpu.make_async_copy(tok_ref.at[pl.ds(r, 1)], xs_ref.at[pl.ds(pos_ref[r * TOPK_E + k], 1)], sem)

    def issue(r, c):
        for k in range(TOPK_E):
            copy(r, k).start()
        return c

    def drain(r, c):
        for k in range(TOPK_E):
            copy(r, k).wait()
        return c

    lax.fori_loop(0, tm, issue, 0)
    lax.fori_loop(0, tm, drain, 0)


def _dispatch(pos_flat, tok, xs):
    n, d = tok.shape
    tm = ROW_TILE
    return pl.pallas_call(
        _dispatch_body,
        grid=(n // tm,),
        in_specs=[pl.BlockSpec((tm * TOPK_E,), lambda j: (j,), memory_space=pltpu.SMEM),
                  pl.BlockSpec((tm, d), lambda j: (j, 0)),
                  pl.BlockSpec(memory_space=pl.ANY)],
        out_specs=pl.BlockSpec(memory_space=pl.ANY),
        out_shape=jax.ShapeDtypeStruct(xs.shape, xs.dtype),
        scratch_shapes=[pltpu.SemaphoreType.DMA(())],
        input_output_aliases={2: 0},
        compiler_params=_cparams(("arbitrary",)),
        name="dispatch",
    )(pos_flat, tok, xs)


def _moe_body(te_ref, nt_ref, x_ref, wu_ref, bg_ref, bl_ref, wd_ref, bd_ref, o_ref, wg_s, wl_s, wd_s):
    g = pl.program_id(0)

    @pl.when((g == 0) | (te_ref[g] != te_ref[jnp.maximum(g - 1, 0)]))
    def _():
        w2 = 2 * LANES
        r = lax.broadcasted_iota(I32, (w2, w2), 0)
        c = lax.broadcasted_iota(I32, (w2, w2), 1)
        src = jnp.where(c < LANES, 2 * c, 2 * (c - LANES) + 1)
        perm = jnp.where(r == src, 1.0, 0.0).astype(BF16)
        for k in range(wu_ref.shape[2] // w2):
            pair = jnp.dot(wu_ref[0, :, k * w2:(k + 1) * w2].astype(BF16), perm, preferred_element_type=F32)
            wg_s[:, k * LANES:(k + 1) * LANES] = pair[:, :LANES].astype(BF16)
            wl_s[:, k * LANES:(k + 1) * LANES] = pair[:, LANES:].astype(BF16)
        wd_s[...] = wd_ref[0].astype(BF16)

    @pl.when(g < nt_ref[0])
    def _():
        x = x_ref[...].astype(BF16)
        glu = jnp.minimum(jnp.dot(x, wg_s[...], preferred_element_type=F32) + bg_ref[0], SWIGLU_LIMIT)
        lin = jnp.clip(jnp.dot(x, wl_s[...], preferred_element_type=F32) + bl_ref[0], -SWIGLU_LIMIT, SWIGLU_LIMIT)
        act = glu * _sigmoid(SWIGLU_ALPHA * glu) * (lin + 1.0)
        o_ref[...] = jnp.dot(act.astype(BF16), wd_s[...], preferred_element_type=F32) + bd_ref[0]

    @pl.when(g >= nt_ref[0])
    def _():
        o_ref[...] = jnp.zeros_like(o_ref)


def _moe(tile_expert, n_tiles, xs, wu, bg, bl, wd, bd):
    r, d = xs.shape
    tm = MOE_TILE
    dff = wd.shape[1]
    row = lambda g, te, nt: (jnp.minimum(g, nt[0] - 1), 0)
    wsp = lambda a: pl.BlockSpec((1,) + a.shape[1:], lambda g, te, nt: (te[g], 0, 0))
    return pl.pallas_call(
        _moe_body,
        grid_spec=pltpu.PrefetchScalarGridSpec(
            num_scalar_prefetch=2,
            grid=(r // tm,),
            in_specs=[pl.BlockSpec((tm, d), row), wsp(wu), wsp(bg), wsp(bl), wsp(wd), wsp(bd)],
            out_specs=pl.BlockSpec((tm, d), lambda g, te, nt: (g, 0)),
            scratch_shapes=[pltpu.VMEM((d, dff), BF16), pltpu.VMEM((d, dff), BF16), pltpu.VMEM((dff, d), BF16)],
        ),
        out_shape=jax.ShapeDtypeStruct((r, d), F32),
        compiler_params=_cparams(("arbitrary",)),
        name="moe",
    )(tile_expert, n_tiles, xs, wu, bg, bl, wd, bd)


def _combine_body(pos_ref, g4_ref, x1_ref, gt2_ref, nf_ref, ys_ref, y_ref, buf, sem):
    tm = x1_ref.shape[0]

    def copy(r, k):
        return pltpu.make_async_copy(ys_ref.at[pl.ds(pos_ref[r * TOPK_E + k], 1)], buf.at[k, pl.ds(r, 1)], sem)

    def issue(r, c):
        for k in range(TOPK_E):
            copy(r, k).start()
        return c

    def drain(r, c):
        for k in range(TOPK_E):
            copy(r, k).wait()
        return c

    lax.fori_loop(0, tm, issue, 0)
    lax.fori_loop(0, tm, drain, 0)
    g4 = g4_ref[...]
    ff = g4[:, 0:1] * buf[0]
    for k in range(1, TOPK_E):
        ff = ff + g4[:, k:k + 1] * buf[k]
    y_ref[...] = _rms(x1_ref[...] + gt2_ref[0] * ff, nf_ref[...])


def _combine(pos_flat, g4, x1, gt2, nf, ys, *, nb, t):
    n, d = x1.shape
    tm = min(ROW_TILE, t)
    nt = t // tm
    if gt2.shape[1] == 1:
        mod_spec = pl.BlockSpec((1, 1, d), lambda j: (j // nt, 0, 0))
    else:
        mod_spec = pl.BlockSpec((1, tm, d), lambda j: (j // nt, j % nt, 0))
    return pl.pallas_call(
        _combine_body,
        grid=(n // tm,),
        in_specs=[pl.BlockSpec((tm * TOPK_E,), lambda j: (j,), memory_space=pltpu.SMEM),
                  pl.BlockSpec((tm, TOPK_E), lambda j: (j, 0)),
                  pl.BlockSpec((tm, d), lambda j: (j, 0)),
                  mod_spec,
                  pl.BlockSpec((1, d), lambda j: (0, 0)),
                  pl.BlockSpec(memory_space=pl.ANY)],
        out_specs=pl.BlockSpec((tm, d), lambda j: (j, 0)),
        out_shape=jax.ShapeDtypeStruct((n, d), F32),
        scratch_shapes=[pltpu.VMEM((TOPK_E, tm, d), F32), pltpu.SemaphoreType.DMA(())],
        compiler_params=_cparams(("arbitrary",)),
        name="combine",
    )(pos_flat, g4, x1, gt2, nf, ys)


def _block_diag(w):
    g, a, b = w.shape
    eye = jnp.eye(g, dtype=w.dtype)
    return (eye[:, None, :, None] * w[:, :, None, :]).reshape(g * a, g * b)


def _layer0(a):
    return a.reshape(a.shape[1:])

def kernel(x_prompt, x_sample, cache_k, cache_v, state_h, state_conv, page_table, c_prompt, c_sample,
           w_ada, b_ada, norm1, norm2, w_in, conv_w, conv_b, w_rg_a, b_rg_a, w_rg_i, b_rg_i, rg_lambda,
           w_proj_a, w_proj_b, w_merge, b_merge, w_out, w_router, b_router, w_up, b_up, w_down, b_down, norm_f):
    assert w_ada.shape[0] == 1, "single trunk layer"
    bp, t, d = x_prompt.shape
    db, ts, _ = x_sample.shape
    _, n_phys, ps, nh, _ = cache_k.shape
    hd = nh * HEAD_DIM
    da = hd
    dr = conv_w.shape[2]
    n_pages = page_table.shape[1]
    ne = w_router.shape[2]
    np_, ns_ = bp * t, db * ts
    nblk = t // MOBA_BLOCK
    assert t % TOK_TILE == 0 and ns_ == TOK_TILE and n_pages % 16 == 0
    assert (n_pages * ps) % MOBA_BLOCK == 0 and (n_pages * ps) // MOBA_BLOCK >= MOBA_TOPK

    mod = _ada(jnp.concatenate([c_prompt, c_sample], axis=0), _layer0(w_ada), _layer0(b_ada))
    mod = mod.reshape(bp + db, N_MOD, d)
    mp = [mod[:bp, i][:, None, :] for i in range(N_MOD)]
    ms = [jnp.tile(mod[bp:, i], (ts, 1))[None] for i in range(N_MOD)]

    g1, g2, nf = norm1, norm2, norm_f[None]
    w_in0 = _layer0(w_in)
    wqkvt, wk, wug = w_in0[:, :3 * da].T, w_in0[:, da:2 * da], w_in0[:, 3 * da:]

    xp2 = x_prompt.reshape(np_, d)
    xs2 = jnp.swapaxes(x_sample, 0, 1).reshape(ns_, d)

    qt_p, kt_p, vt_p, u_p, g_p, kb_p, vtb_p, km_p = _proj(
        xp2, mp[0], mp[1], g1, wqkvt.astype(BF16), wug.astype(BF16), wk.astype(BF16), nb=bp, t=t)
    qt_s, kt_s, vt_s, u_s, g_s = _proj(xs2, ms[0], ms[1], g1, wqkvt, wug, nb=1, t=ns_)

    oa_p = _attn_p(qt_p, kb_p, vtb_p, km_p.reshape(bp, nblk, da), nb=bp, t=t)

    kc4 = jnp.transpose(cache_k.reshape(n_phys, ps, nh, HEAD_DIM), (0, 2, 3, 1))
    vc4 = jnp.transpose(cache_v.reshape(n_phys, ps, nh, HEAD_DIM), (0, 2, 3, 1))
    pt_flat = page_table.reshape(-1)
    km = _kmean_s(pt_flat, kc4.reshape(n_phys, hd, ps), db=db, n_pages=n_pages)
    km = jnp.transpose(km, (0, 2, 1, 3)).reshape(db, hd, -1)
    tmaj = lambda a: jnp.transpose(a.reshape(hd, ts, db), (2, 0, 1))
    qs3, kn3, vn3 = tmaj(qt_s), tmaj(kt_s), tmaj(vt_s)
    idx = _sel_s(qs3, km)
    ot_s = _attn_s(idx.reshape(-1), pt_flat, qs3, kn3, vn3, kc4, vc4, n_pages=n_pages)
    oa_s = jnp.transpose(ot_s, (2, 0, 1)).reshape(ns_, hd)

    cw, cb = _layer0(conv_w), conv_b
    wa, wi = _block_diag(_layer0(w_rg_a)), _block_diag(_layer0(w_rg_i))
    ba, bi, lam = b_rg_a, b_rg_i, rg_lambda
    ob_p, hl_p, cn_p = _rg_p(u_p, g_p, cw, cb, wa, ba, wi, bi, lam, nb=bp, t=t)
    ob_s, hl_s, cn_s = _rg_s(u_s.reshape(ts, db, dr), g_s.reshape(ts, db, dr),
                             jnp.swapaxes(_layer0(state_conv), 0, 1), _layer0(state_h),
                             cw, cb, wa, ba, wi, bi, lam)

    wsm = (g1, g2, _layer0(w_merge).astype(BF16), b_merge, _layer0(w_proj_a).astype(BF16),
           _layer0(w_proj_b).astype(BF16), _layer0(w_out).astype(BF16), _layer0(w_router), b_router)
    x1_p, tok_p, gs_p, cnt_p = _post(xp2, mp[:5], oa_p, ob_p, *wsm, nb=bp, t=t)
    x1_s, tok_s, gs_s, cnt_s = _post(xs2, ms[:5], oa_s.astype(BF16), ob_s.reshape(ns_, dr).astype(BF16), *wsm,
                                     nb=1, t=ns_)

    gs = jnp.concatenate([gs_p, gs_s], axis=0)
    cnt = jnp.concatenate([cnt_p, cnt_s], axis=0).reshape(-1, ne).astype(I32)
    tot = jnp.sum(cnt, axis=0)
    tiles_e = (tot + MOE_TILE - 1) // MOE_TILE
    tile_end = jnp.cumsum(tiles_e)
    poff = (tile_end - tiles_e) * MOE_TILE
    base = poff[None, :] + jnp.cumsum(cnt, axis=0) - cnt
    n_all = np_ + ns_
    g_max = (n_all * TOPK_E) // MOE_TILE + ne
    n_tiles = tile_end[-1:].astype(I32)
    tile_id = jnp.minimum(jnp.arange(g_max, dtype=I32), n_tiles - 1)
    te = jnp.sum((tile_end[None, :] <= tile_id[:, None]).astype(I32), axis=1)
    pos, g4 = _route(gs, base.astype(F32)[:, None, :])
    pos_flat = pos.reshape(-1)

    xs = jnp.zeros((g_max * MOE_TILE, d), F32)
    xs = _dispatch(pos_flat[:np_ * TOPK_E], tok_p, xs)
    xs = _dispatch(pos_flat[np_ * TOPK_E:], tok_s, xs)
    b_up0 = _layer0(b_up)
    ys = _moe(te, n_tiles, xs, _layer0(w_up), b_up0[:, None, 0::2], b_up0[:, None, 1::2],
              _layer0(w_down), _layer0(b_down)[:, None, :])
    y_p = _combine(pos_flat[:np_ * TOPK_E], g4[:np_], x1_p, mp[5], nf, ys, nb=bp, t=t)
    y_s = _combine(pos_flat[np_ * TOPK_E:], g4[np_:], x1_s, ms[5], nf, ys, nb=1, t=ns_)

    y_prompt = y_p.reshape(bp, t, d)
    y_sample = jnp.swapaxes(y_s.reshape(ts, db, d), 0, 1)
    to5 = lambda a: jnp.transpose(a.reshape(bp, nh, HEAD_DIM, t), (0, 3, 1, 2))[None]
    k_prompt, v_prompt = to5(kt_p), to5(vt_p)
    to5s = lambda a: jnp.transpose(a.reshape(nh, HEAD_DIM, ts, db), (3, 2, 0, 1))[None]
    k_sample, v_sample = to5s(kt_s), to5s(vt_s)
    h_prompt = hl_p.reshape(1, bp, dr)
    conv_prompt = cn_p[None]
    h_sample = hl_s[None]
    conv_sample = jnp.swapaxes(cn_s, 0, 1)[None]
    return (y_prompt, y_sample, k_prompt, v_prompt, h_prompt, conv_prompt,
            k_sample, v_sample, h_sample, conv_sample)
```

```python
import functools

import jax
import jax.numpy as jnp
from jax import lax
from jax.experimental import pallas as pl
from jax.experimental.pallas import tpu as pltpu

F32 = jnp.float32
BF16 = jnp.bfloat16
I32 = jnp.int32

HEAD_DIM = 64
MOBA_BLOCK = 256
MOBA_TOPK = 3
RG_C = 8.0
TOPK_E = 4
SWIGLU_ALPHA = 1.702
SWIGLU_LIMIT = 7.0
EPS = 1e-6
N_MOD = 6

LANES = 128
TOK_TILE = 512
ATTN_HEADS = 4
RG_TILE = 256
MOE_TILE = 512
ROW_TILE = 256
CHUNK = 8
VMEM_LIMIT = 56 * 1024 * 1024


def _cparams(sem):
    return pltpu.CompilerParams(dimension_semantics=sem, vmem_limit_bytes=VMEM_LIMIT)


def _dot(a, b):
    return jnp.dot(a.astype(BF16), b.astype(BF16), preferred_element_type=F32)


def _dot_t(a, b):
    return lax.dot_general(a.astype(BF16), b.astype(BF16), (((1,), (1,)), ((), ())),
                           preferred_element_type=F32)


def _split(a):
    hi = a.astype(BF16)
    lo = (a - hi.astype(F32)).astype(BF16)
    return hi, lo


def _dot3(a, b):
    ah, al = _split(a)
    bh, bl = _split(b)
    return (jnp.dot(ah, bh, preferred_element_type=F32)
            + jnp.dot(al, bh, preferred_element_type=F32)
            + jnp.dot(ah, bl, preferred_element_type=F32))


def _dot3_t(a, b):
    ah, al = _split(a)
    bh, bl = _split(b)
    dn = (((1,), (1,)), ((), ()))
    return (lax.dot_general(ah, bh, dn, preferred_element_type=F32)
            + lax.dot_general(al, bh, dn, preferred_element_type=F32)
            + lax.dot_general(ah, bl, dn, preferred_element_type=F32))


def _sigmoid(x):
    return 1.0 / (1.0 + jnp.exp(-x))


def _rms(x, g):
    return x * lax.rsqrt(jnp.mean(x * x, axis=-1, keepdims=True) + EPS) * g


def _norm_mod(x, g, shift, scale):
    return _rms(x, g) * (1.0 + scale) + shift


def _gelu_tanh(x):
    return 0.5 * x * (1.0 + jnp.tanh(0.7978845608028654 * (x + 0.044715 * (x * x * x))))


def _ada_body(c_ref, w_ref, b_ref, o_ref):
    c = c_ref[...]
    o_ref[...] = _dot3(c * _sigmoid(c), w_ref[...]) + b_ref[...]


def _ada(c, w, b):
    n, d = c.shape
    e = w.shape[1]
    te = 1024
    return pl.pallas_call(
        _ada_body,
        grid=(e // te,),
        in_specs=[pl.BlockSpec((n, d), lambda j: (0, 0)),
                  pl.BlockSpec((d, te), lambda j: (0, j)),
                  pl.BlockSpec((1, te), lambda j: (0, j))],
        out_specs=pl.BlockSpec((n, te), lambda j: (0, j)),
        out_shape=jax.ShapeDtypeStruct((n, e), F32),
        compiler_params=_cparams(("arbitrary",)),
        name="ada",
    )(c, w, b.reshape(1, e))


def _proj_body(x_ref, sh_ref, sc_ref, g1_ref, wqkvt_ref, wug_ref, *rest, prompt):
    h = _norm_mod(x_ref[...], g1_ref[...], sh_ref[0], sc_ref[0])
    mm, mmt = (_dot, _dot_t) if prompt else (_dot3, _dot3_t)
    if prompt:
        wk_ref, qt_ref, kt_ref, vt_ref, u_ref, g_ref, kb_ref, vtb_ref, km_ref = rest
    else:
        qt_ref, kt_ref, vt_ref, u_ref, g_ref = rest
    da = qt_ref.shape[1]
    qkvt = mmt(wqkvt_ref[...], h)
    qt_ref[0] = qkvt[:da]
    kt_ref[0] = qkvt[da:2 * da]
    vt = qkvt[2 * da:]
    vt_ref[0] = vt
    ug = mm(h, wug_ref[...])
    dr = u_ref.shape[1]
    u_ref[...] = ug[:, :dr]
    g_ref[...] = ug[:, dr:]
    if prompt:
        k = mm(h, wk_ref[...])
        kb_ref[...] = k.astype(BF16)
        vtb_ref[0] = vt.astype(BF16)
        means = [jnp.sum(k[n * MOBA_BLOCK:(n + 1) * MOBA_BLOCK], axis=0, keepdims=True) * (1.0 / MOBA_BLOCK)
                 for n in range(k.shape[0] // MOBA_BLOCK)]
        km_ref[0] = jnp.concatenate(means, axis=0)


def _proj(x2d, shift, scale, g1, wqkvt, wug, wk=None, *, nb, t):
    prompt = wk is not None
    n, d = x2d.shape
    tm = min(TOK_TILE, t)
    nt = t // tm
    da = wqkvt.shape[0] // 3
    dr = wug.shape[1] // 2
    mrows = shift.shape[1]
    if mrows == 1:
        mod_spec = pl.BlockSpec((1, 1, d), lambda b, j: (b, 0, 0))
    else:
        mod_spec = pl.BlockSpec((1, tm, d), lambda b, j: (b, j, 0))
    full = lambda a: pl.BlockSpec(a.shape, lambda b, j: (0,) * a.ndim)
    tok = lambda w: pl.BlockSpec((tm, w), lambda b, j: (b * nt + j, 0))
    featmaj = pl.BlockSpec((1, da, tm), lambda b, j: (b, 0, j))
    in_specs = [tok(d), mod_spec, mod_spec, full(g1), full(wqkvt), full(wug)]
    out_specs = [featmaj, featmaj, featmaj, tok(dr), tok(dr)]
    out_shape = [jax.ShapeDtypeStruct((nb, da, t), F32)] * 3 + [jax.ShapeDtypeStruct((n, dr), F32)] * 2
    args = [x2d, shift, scale, g1, wqkvt, wug]
    if prompt:
        bpt = tm // MOBA_BLOCK
        in_specs.append(full(wk))
        args.append(wk)
        out_specs += [tok(da), featmaj,
                      pl.BlockSpec((1, bpt, da), lambda b, j: (b * nt + j, 0, 0))]
        out_shape += [jax.ShapeDtypeStruct((n, da), BF16),
                      jax.ShapeDtypeStruct((nb, da, t), BF16),
                      jax.ShapeDtypeStruct((nb * nt, bpt, da), F32)]
    return pl.pallas_call(
        functools.partial(_proj_body, prompt=prompt),
        grid=(nb, nt),
        in_specs=in_specs,
        out_specs=out_specs,
        out_shape=out_shape,
        compiler_params=_cparams(("arbitrary", "arbitrary")),
        name="proj_p" if prompt else "proj_s",
    )(*args)


def _attn_p_body(qt_ref, kb_ref, kd_ref, vtb_ref, vd_ref, km_ref, o_ref, sel_ref, *, nblk):
    i = pl.program_id(2)
    bs = MOBA_BLOCK
    krow = lax.broadcasted_iota(I32, (bs, bs), 0)
    qcol = lax.broadcasted_iota(I32, (bs, bs), 1)
    blk = lax.broadcasted_iota(I32, (nblk, bs), 0)
    valid = blk < i
    nhh = qt_ref.shape[1] // HEAD_DIM
    heads = [slice(hh * HEAD_DIM, (hh + 1) * HEAD_DIM) for hh in range(nhh)]

    def run(npast):
        outs = []
        for hh, hs in enumerate(heads):
            qt = qt_ref[0, hs, :]
            sv = jnp.where(valid, _dot3(km_ref[0, :, hs], qt), -jnp.inf)
            cnt = jnp.zeros_like(sv)
            for m in range(nblk - 1):
                sm = sv[m:m + 1, :]
                beats = (sm > sv) | ((sm == sv) & (m < blk))
                cnt = cnt + jnp.where(beats, 1.0, 0.0)
            sel_ref[hh] = jnp.where(valid & (cnt < MOBA_TOPK), 1.0, 0.0)

            qb = (qt * (HEAD_DIM ** -0.5)).astype(BF16)
            sd = jnp.dot(kd_ref[:, hs], qb, preferred_element_type=F32)
            sd = jnp.where(krow <= qcol, sd, -jnp.inf)
            sp = jnp.dot(kb_ref[0:npast * bs, hs], qb, preferred_element_type=F32)
            sps = [jnp.where(sel_ref[hh, j:j + 1, :] > 0.0, sp[j * bs:(j + 1) * bs], -jnp.inf)
                   for j in range(npast)]
            mx = jnp.max(sd, axis=0, keepdims=True)
            for sj in sps:
                mx = jnp.maximum(mx, jnp.max(sj, axis=0, keepdims=True))
            pd = jnp.exp(sd - mx)
            den = jnp.sum(pd, axis=0, keepdims=True)
            pps = []
            for sj in sps:
                pj = jnp.exp(sj - mx)
                den = den + jnp.sum(pj, axis=0, keepdims=True)
                pps.append(pj.astype(BF16))
            acc = jnp.dot(vd_ref[0, hs, :], pd.astype(BF16), preferred_element_type=F32)
            acc = acc + jnp.dot(vtb_ref[0, hs, 0:npast * bs], jnp.concatenate(pps, axis=0),
                                preferred_element_type=F32)
            outs.append(acc / den)
        o_ref[...] = jnp.concatenate(outs, axis=0).T.astype(o_ref.dtype)

    few = (nblk - 1) // 2

    @pl.when(i <= few)
    def _():
        run(few)

    @pl.when(i > few)
    def _():
        run(nblk - 1)


def _attn_p(qt, kb, vtb, km, *, nb, t):
    da = qt.shape[1]
    nblk = t // MOBA_BLOCK
    hw = ATTN_HEADS * HEAD_DIM
    nhp = da // hw
    return pl.pallas_call(
        functools.partial(_attn_p_body, nblk=nblk),
        grid=(nb, nhp, nblk),
        in_specs=[pl.BlockSpec((1, hw, MOBA_BLOCK), lambda b, h, i: (b, h, i)),
                  pl.BlockSpec((t, hw), lambda b, h, i: (b, h)),
                  pl.BlockSpec((MOBA_BLOCK, hw), lambda b, h, i: (b * nblk + i, h)),
                  pl.BlockSpec((1, hw, t), lambda b, h, i: (b, h, 0)),
                  pl.BlockSpec((1, hw, MOBA_BLOCK), lambda b, h, i: (b, h, i)),
                  pl.BlockSpec((1, nblk, hw), lambda b, h, i: (b, 0, h))],
        out_specs=pl.BlockSpec((MOBA_BLOCK, hw), lambda b, h, i: (b * nblk + i, h)),
        out_shape=jax.ShapeDtypeStruct((nb * t, da), BF16),
        scratch_shapes=[pltpu.VMEM((ATTN_HEADS, nblk, MOBA_BLOCK), F32)],
        compiler_params=_cparams(("arbitrary", "arbitrary", "arbitrary")),
        name="attn_p",
    )(qt, kb, kb, vtb, vtb, km)


def _rg_gates(uc, wa_ref, ba_ref, wi_ref, bi_ref, lam_ref):
    r = _sigmoid(_dot3(uc, wa_ref[...]) + ba_ref[...])
    ig = _sigmoid(_dot3(uc, wi_ref[...]) + bi_ref[...])
    nl = -lam_ref[...]
    softplus = jnp.maximum(nl, 0.0) + jnp.log(1.0 + jnp.exp(-jnp.abs(nl)))
    log_a = -RG_C * r * softplus
    a = jnp.exp(log_a)
    bx = jnp.sqrt(1.0 - jnp.exp(2.0 * log_a)) * (ig * uc)
    return a, bx


def _rg_p_body(u_ref, g_ref, cw_ref, cb_ref, wa_ref, ba_ref, wi_ref, bi_ref, lam_ref,
               o_ref, hl_ref, cn_ref, ubuf, hc):
    j = pl.program_id(1)
    tt = u_ref.shape[0]
    pad = 8

    @pl.when(j == 0)
    def _():
        ubuf[0:pad] = jnp.zeros((pad, ubuf.shape[1]), F32)
        hc[...] = jnp.zeros_like(hc)

    u = u_ref[...]
    ubuf[pad:pad + tt] = u
    cw = cw_ref[...]
    kw = cw.shape[0]
    uc = cb_ref[...] + u * cw[kw - 1:kw]
    for jj in range(kw - 1):
        off = pad - (kw - 1) + jj
        uc = uc + ubuf[off:off + tt] * cw[jj:jj + 1]
    a, bx = _rg_gates(uc, wa_ref, ba_ref, wi_ref, bi_ref, lam_ref)

    row = lax.broadcasted_iota(I32, a.shape, 0)
    d = 1
    while d < tt:
        keep = row >= d
        a_sh = jnp.where(keep, pltpu.roll(a, d, 0), 1.0)
        b_sh = jnp.where(keep, pltpu.roll(bx, d, 0), 0.0)
        bx = a * b_sh + bx
        a = a * a_sh
        d *= 2
    h = bx + a * hc[...]
    hc[...] = h[tt - 1:tt]
    hl_ref[0] = h[tt - 1:tt]
    cn_ref[0] = u[tt - (kw - 1):tt]
    ubuf[0:pad] = u[tt - pad:tt]
    o_ref[...] = (h * _gelu_tanh(g_ref[...])).astype(o_ref.dtype)


def _rg_p(u, g, cw, cb, wa, ba, wi, bi, lam, *, nb, t):
    n, dr = u.shape
    tt = min(RG_TILE, t)
    nt = t // tt
    kw = cw.shape[0]
    full = lambda a: pl.BlockSpec(a.shape, lambda b, j: (0,) * a.ndim)
    tok = pl.BlockSpec((tt, dr), lambda b, j: (b * nt + j, 0))
    return pl.pallas_call(
        _rg_p_body,
        grid=(nb, nt),
        in_specs=[tok, tok, full(cw), full(cb), full(wa), full(ba), full(wi), full(bi), full(lam)],
        out_specs=[tok,
                   pl.BlockSpec((1, 1, dr), lambda b, j: (b, 0, 0)),
                   pl.BlockSpec((1, kw - 1, dr), lambda b, j: (b, 0, 0))],
        out_shape=[jax.ShapeDtypeStruct((n, dr), BF16),
                   jax.ShapeDtypeStruct((nb, 1, dr), F32),
                   jax.ShapeDtypeStruct((nb, kw - 1, dr), F32)],
        scratch_shapes=[pltpu.VMEM((8 + tt, dr), F32), pltpu.VMEM((1, dr), F32)],
        compiler_params=_cparams(("arbitrary", "arbitrary")),
        name="rg_p",
    )(u, g, cw, cb, wa, ba, wi, bi, lam)


def _rg_s_body(u_ref, g_ref, cs_ref, h0_ref, cw_ref, cb_ref, wa_ref, ba_ref, wi_ref, bi_ref, lam_ref,
               o_ref, hl_ref, cn_ref):
    ts = u_ref.shape[0]
    cw = cw_ref[...]
    kw = cw.shape[0]
    full = [cs_ref[jj] for jj in range(kw - 1)] + [u_ref[tt] for tt in range(ts)]
    h = h0_ref[...]
    for tt in range(ts):
        uc = cb_ref[...]
        for jj in range(kw):
            uc = uc + full[tt + jj] * cw[jj:jj + 1]
        a, bx = _rg_gates(uc, wa_ref, ba_ref, wi_ref, bi_ref, lam_ref)
        h = a * h + bx
        o_ref[tt] = h * _gelu_tanh(g_ref[tt])
    hl_ref[...] = h
    for jj in range(kw - 1):
        cn_ref[jj] = full[ts + jj]


def _rg_s(u, g, cs, h0, cw, cb, wa, ba, wi, bi, lam):
    ts, db, dr = u.shape
    kw = cw.shape[0]
    return pl.pallas_call(
        _rg_s_body,
        out_shape=[jax.ShapeDtypeStruct((ts, db, dr), F32),
                   jax.ShapeDtypeStruct((db, dr), F32),
                   jax.ShapeDtypeStruct((kw - 1, db, dr), F32)],
        compiler_params=pltpu.CompilerParams(vmem_limit_bytes=VMEM_LIMIT),
        name="rg_s",
    )(u, g, cs, h0, cw, cb, wa, ba, wi, bi, lam)


def _kmean_s_body(pt_ref, *refs, npg):
    del pt_ref
    pages, o_ref = refs[:npg], refs[npg]
    cols = []
    for n in range(npg // 2):
        blk = pages[2 * n][0] + pages[2 * n + 1][0]
        cols.append(jnp.sum(blk, axis=1, keepdims=True) * (1.0 / MOBA_BLOCK))
    o_ref[0, 0] = jnp.concatenate(cols, axis=1)


def _kmean_s(pt_flat, kt3, *, db, n_pages):
    _, hd, ps = kt3.shape
    npg = 16
    nchunk = n_pages // npg
    specs = [pl.BlockSpec((1, hd, ps), functools.partial(
        lambda b, c, pt, p: (pt[b * n_pages + c * npg + p], 0, 0), p=p)) for p in range(npg)]
    return pl.pallas_call(
        functools.partial(_kmean_s_body, npg=npg),
        grid_spec=pltpu.PrefetchScalarGridSpec(
            num_scalar_prefetch=1,
            grid=(db, nchunk),
            in_specs=specs,
            out_specs=pl.BlockSpec((1, 1, hd, npg // 2), lambda b, c, pt: (b, c, 0, 0)),
        ),
        out_shape=jax.ShapeDtypeStruct((db, nchunk, hd, npg // 2), F32),
        compiler_params=_cparams(("arbitrary", "arbitrary")),
        name="kmean_s",
    )(pt_flat, *([kt3] * npg))


def _sel_s_body(qt_ref, km_ref, o_ref, *, nh):
    qt = qt_ref[0]
    km = km_ref[0]
    ts = qt.shape[1]
    nc = km.shape[1]
    lane = lax.broadcasted_iota(I32, (nh, nc), 1).astype(F32)
    outs = []
    for t in range(ts):
        prod = qt[:, t:t + 1] * km
        s = jnp.sum(prod.reshape(nh, HEAD_DIM, nc), axis=1)
        for _ in range(MOBA_TOPK):
            m = jnp.max(s, axis=1, keepdims=True)
            idx = jnp.min(jnp.where(s == m, lane, float(nc)), axis=1, keepdims=True)
            outs.append(idx)
            s = jnp.where(lane == idx, -jnp.inf, s)
    o_ref[0] = jnp.concatenate(outs, axis=1).astype(I32)


def _sel_s(qt, km):
    db, hd, ts = qt.shape
    nc = km.shape[2]
    nh = hd // HEAD_DIM
    return pl.pallas_call(
        functools.partial(_sel_s_body, nh=nh),
        grid=(db,),
        in_specs=[pl.BlockSpec((1, hd, ts), lambda b: (b, 0, 0)),
                  pl.BlockSpec((1, hd, nc), lambda b: (b, 0, 0))],
        out_specs=pl.BlockSpec((1, nh, ts * MOBA_TOPK), lambda b: (b, 0, 0)),
        out_shape=jax.ShapeDtypeStruct((db, nh, ts * MOBA_TOPK), I32),
        compiler_params=_cparams(("arbitrary",)),
        name="sel_s",
    )(qt, km)


def _attn_s_body(idx_ref, pt_ref, qt_ref, knt_ref, vnt_ref, kc_ref, vc_ref, o_ref,
                 kbuf, vbuf, sem, *, nh, ts, n_pages, ppb):
    b = pl.program_id(0)
    nb = pl.num_programs(0)
    nsl = nh * ts * MOBA_TOPK
    ps = kc_ref.shape[3]

    def copies(bb, slot, s):
        h = s // (ts * MOBA_TOPK)
        blk = idx_ref[bb * nsl + s]
        out = []
        for pg in range(ppb):
            page = pt_ref[bb * n_pages + blk * ppb + pg]
            out.append(pltpu.make_async_copy(
                kc_ref.at[page, h], kbuf.at[slot, s, :, pl.ds(pg * ps, ps)], sem.at[slot]))
            out.append(pltpu.make_async_copy(
                vc_ref.at[page, h], vbuf.at[slot, s, :, pl.ds(pg * ps, ps)], sem.at[slot]))
        return out

    def start_all(bb, slot):
        def f(s, c):
            for cp in copies(bb, slot, s):
                cp.start()
            return c
        lax.fori_loop(0, nsl, f, 0)

    def wait_all(bb, slot):
        def f(s, c):
            for cp in copies(bb, slot, s):
                cp.wait()
            return c
        lax.fori_loop(0, nsl, f, 0)

    @pl.when(b == 0)
    def _():
        start_all(0, 0)

    @pl.when(b + 1 < nb)
    def _():
        start_all(b + 1, (b + 1) % 2)

    slot = b % 2
    wait_all(b, slot)

    scale = HEAD_DIM ** -0.5
    tcol = lax.broadcasted_iota(I32, (1, ts), 1)

    def head(h, c):
        r0 = pl.multiple_of(h * HEAD_DIM, HEAD_DIM)
        qh = qt_ref[0, pl.ds(r0, HEAD_DIM), :] * scale
        knh = knt_ref[0, pl.ds(r0, HEAD_DIM), :]
        vnh = vnt_ref[0, pl.ds(r0, HEAD_DIM), :]
        cols = []
        for t in range(ts):
            qc = qh[:, t:t + 1]
            s_new = jnp.sum(qc * knh, axis=0, keepdims=True)
            s_new = jnp.where(tcol <= t, s_new, -jnp.inf)
            s_sel = []
            for j in range(MOBA_TOPK):
                sl = (h * ts + t) * MOBA_TOPK + j
                s_sel.append(jnp.sum(qc * kbuf[slot, sl], axis=0, keepdims=True))
            m = jnp.max(s_new, axis=1, keepdims=True)
            for sj in s_sel:
                m = jnp.maximum(m, jnp.max(sj, axis=1, keepdims=True))
            p_new = jnp.exp(s_new - m)
            l = jnp.sum(p_new, axis=1, keepdims=True)
            acc = jnp.sum(p_new * vnh, axis=1, keepdims=True)
            pv = None
            for j in range(MOBA_TOPK):
                sl = (h * ts + t) * MOBA_TOPK + j
                pj = jnp.exp(s_sel[j] - m)
                l = l + jnp.sum(pj, axis=1, keepdims=True)
                term = pj * vbuf[slot, sl]
                pv = term if pv is None else pv + term
            acc = acc + jnp.sum(pv, axis=1, keepdims=True)
            cols.append(acc / l)
        o_ref[0, pl.ds(r0, HEAD_DIM), :] = jnp.concatenate(cols, axis=1)
        return c

    lax.fori_loop(0, nh, head, 0)


def _attn_s(idx_flat, pt_flat, qt, knt, vnt, kc4, vc4, *, n_pages):
    db, hd, ts = qt.shape
    _, nh, dh, ps = kc4.shape
    ppb = MOBA_BLOCK // ps
    nsl = nh * ts * MOBA_TOPK
    vec = pl.BlockSpec((1, hd, ts), lambda b, idx, pt: (b, 0, 0))
    return pl.pallas_call(
        functools.partial(_attn_s_body, nh=nh, ts=ts, n_pages=n_pages, ppb=ppb),
        grid_spec=pltpu.PrefetchScalarGridSpec(
            num_scalar_prefetch=2,
            grid=(db,),
            in_specs=[vec, vec, vec, pl.BlockSpec(memory_space=pl.ANY), pl.BlockSpec(memory_space=pl.ANY)],
            out_specs=vec,
            scratch_shapes=[pltpu.VMEM((2, nsl, dh, MOBA_BLOCK), F32),
                            pltpu.VMEM((2, nsl, dh, MOBA_BLOCK), F32),
                            pltpu.SemaphoreType.DMA((2,))],
        ),
        out_shape=jax.ShapeDtypeStruct((db, hd, ts), F32),
        compiler_params=_cparams(("arbitrary",)),
        name="attn_s",
    )(idx_flat, pt_flat, qt, knt, vnt, kc4, vc4)


def _post_body(x_ref, sh1_ref, sc1_ref, gt1_ref, sh2_ref, sc2_ref, oa_ref, ob_ref,
               g1_ref, g2_ref, wm_ref, bm_ref, wpa_ref, wpb_ref, wo_ref, wr_ref, br_ref,
               x1_ref, tok_ref, gate_ref, cnt_ref):
    x = x_ref[...]
    d = x.shape[1]
    h = _norm_mod(x, g1_ref[...], sh1_ref[0], sc1_ref[0])
    gates = _sigmoid(_dot(h, wm_ref[...]) + bm_ref[...])
    merged = gates[:, :d] * _dot(oa_ref[...], wpa_ref[...]) + gates[:, d:] * _dot(ob_ref[...], wpb_ref[...])
    x1 = x + gt1_ref[0] * _dot(merged, wo_ref[...])
    x1_ref[...] = x1
    h2 = _norm_mod(x1, g2_ref[...], sh2_ref[0], sc2_ref[0])
    tok_ref[...] = h2
    logits = _dot3(h2, wr_ref[...]) + br_ref[...]
    ne = logits.shape[1]
    lane = lax.broadcasted_iota(I32, logits.shape, 1).astype(F32)
    rem = logits
    vals, hots = [], []
    for _ in range(TOPK_E):
        m = jnp.max(rem, axis=1, keepdims=True)
        idx = jnp.min(jnp.where(rem == m, lane, float(ne)), axis=1, keepdims=True)
        hot = lane == idx
        vals.append(m)
        hots.append(hot)
        rem = jnp.where(hot, -jnp.inf, rem)
    es = [jnp.exp(v - vals[0]) for v in vals]
    den = es[0]
    for e in es[1:]:
        den = den + e
    gate = jnp.zeros_like(logits)
    sel = jnp.zeros_like(logits)
    for e, hot in zip(es, hots):
        gate = gate + jnp.where(hot, e / den, 0.0)
        sel = sel + jnp.where(hot, 1.0, 0.0)
    gate_ref[...] = jnp.concatenate([gate, sel], axis=1)
    cnt_ref[0] = jnp.concatenate(
        [jnp.sum(sel[r * ROW_TILE:(r + 1) * ROW_TILE], axis=0, keepdims=True) for r in range(cnt_ref.shape[1])],
        axis=0)


def _post(x2d, mods, oa, ob, g1, g2, wm, bm, wpa, wpb, wo, wr, br, *, nb, t):
    n, d = x2d.shape
    tm = min(TOK_TILE, t)
    nt = t // tm
    ne = wr.shape[1]
    mrows = mods[0].shape[1]
    if mrows == 1:
        mod_spec = pl.BlockSpec((1, 1, d), lambda b, j: (b, 0, 0))
    else:
        mod_spec = pl.BlockSpec((1, tm, d), lambda b, j: (b, j, 0))
    full = lambda a: pl.BlockSpec(a.shape, lambda b, j: (0,) * a.ndim)
    tok = lambda w: pl.BlockSpec((tm, w), lambda b, j: (b * nt + j, 0))
    ws = (g1, g2, wm, bm, wpa, wpb, wo, wr, br)
    return pl.pallas_call(
        _post_body,
        grid=(nb, nt),
        in_specs=[tok(d)] + [mod_spec] * 5 + [tok(oa.shape[1]), tok(ob.shape[1])] + [full(w) for w in ws],
        out_specs=[tok(d), tok(d), tok(2 * ne),
                   pl.BlockSpec((1, tm // ROW_TILE, ne), lambda b, j: (b * nt + j, 0, 0))],
        out_shape=[jax.ShapeDtypeStruct((n, d), F32),
                   jax.ShapeDtypeStruct((n, d), F32),
                   jax.ShapeDtypeStruct((n, 2 * ne), F32),
                   jax.ShapeDtypeStruct((nb * nt, tm // ROW_TILE, ne), F32)],
        compiler_params=_cparams(("arbitrary", "arbitrary")),
        name="post",
    )(x2d, *mods, oa, ob, *ws)


def _route_body(gs_ref, base_ref, pos_ref, g4_ref):
    gs = gs_ref[...]
    ne = gs.shape[1] // 2
    gate, sel = gs[:, :ne], gs[:, ne:]
    tm = gs.shape[0]
    r = lax.broadcasted_iota(I32, (tm, tm), 0)
    c = lax.broadcasted_iota(I32, (tm, tm), 1)
    tri = jnp.where(c < r, 1.0, 0.0).astype(BF16)
    rank = jnp.dot(tri, sel.astype(BF16), preferred_element_type=F32)
    posm = base_ref[0] + rank
    lane = lax.broadcasted_iota(I32, sel.shape, 1).astype(F32)
    rem = sel > 0.0
    ps, gv = [], []
    for _ in range(TOPK_E):
        idx = jnp.min(jnp.where(rem, lane, float(ne)), axis=1, keepdims=True)
        hot = lane == idx
        ps.append(jnp.sum(jnp.where(hot, posm, 0.0), axis=1, keepdims=True))
        gv.append(jnp.sum(jnp.where(hot, gate, 0.0), axis=1, keepdims=True))
        rem = rem & jnp.logical_not(hot)
    pos_ref[...] = jnp.concatenate(ps, axis=1).astype(I32)
    g4_ref[...] = jnp.concatenate(gv, axis=1)


def _route(gs, base):
    n, ne2 = gs.shape
    tm = ROW_TILE
    return pl.pallas_call(
        _route_body,
        grid=(n // tm,),
        in_specs=[pl.BlockSpec((tm, ne2), lambda j: (j, 0)),
                  pl.BlockSpec((1, 1, ne2 // 2), lambda j: (j, 0, 0))],
        out_specs=[pl.BlockSpec((tm, TOPK_E), lambda j: (j, 0)),
                   pl.BlockSpec((tm, TOPK_E), lambda j: (j, 0))],
        out_shape=[jax.ShapeDtypeStruct((n, TOPK_E), I32),
                   jax.ShapeDtypeStruct((n, TOPK_E), F32)],
        compiler_params=_cparams(("arbitrary",)),
        name="route",
    )(gs, base)


def _segment_copies(ls_ref, gb_ref, nc_ref, tile, n_exp, make, act):
    def per_expert(e, c):
        i = tile * n_exp + e
        ls = ls_ref[i]
        gb = gb_ref[i]

        def per_chunk(k, c2):
            off = k * CHUNK
            cp = make(pl.multiple_of(ls + off, CHUNK), pl.multiple_of(gb + off, CHUNK))
            cp.start() if act == "start" else cp.wait()
            return c2

        lax.fori_loop(0, nc_ref[i], per_chunk, 0)
        return c

    lax.fori_loop(0, n_exp, per_expert, 0)


def _dispatch_body(ls_ref, gb_ref, nc_ref, ts_ref, tn_ref, lr_ref, tok_ref, *rest, tile0, n_exp, first):
    if first:
        xs_ref, loc, zrow, sem, zsem = rest
    else:
        _, xs_ref, loc, zrow, sem, zsem = rest
    j = pl.program_id(0)
    nj = pl.num_programs(0)
    slot = j % 2
    nloc, tm = loc.shape[1], tok_ref.shape[0]

    if first:
        @pl.when(j == 0)
        def _():
            zrow[...] = jnp.zeros_like(zrow)

            def tails(act):
                def per_expert(e, c):
                    def per_chunk(k, c2):
                        cp = pltpu.make_async_copy(
                            zrow, xs_ref.at[pl.ds(pl.multiple_of(ts_ref[e] + k * CHUNK, CHUNK), CHUNK)], zsem)
                        cp.start() if act == "start" else cp.wait()
                        return c2
                    lax.fori_loop(0, tn_ref[e], per_chunk, 0)
                    return c
                lax.fori_loop(0, n_exp, per_expert, 0)

            tails("start")
            tails("wait")

    r = lax.broadcasted_iota(I32, (nloc, tm), 0)
    perm = jnp.zeros((nloc, tm), F32)
    for k in range(TOPK_E):
        perm = perm + jnp.where(r == lr_ref[0, k:k + 1, :], 1.0, 0.0)
    loc[slot] = jnp.dot(perm.astype(BF16), tok_ref[...].astype(BF16), preferred_element_type=F32)

    def make(slot_):
        return lambda ls, gb: pltpu.make_async_copy(
            loc.at[slot_, pl.ds(ls, CHUNK)], xs_ref.at[pl.ds(gb, CHUNK)], sem.at[slot_])

    _segment_copies(ls_ref, gb_ref, nc_ref, tile0 + j, n_exp, make(slot), "start")

    @pl.when(j > 0)
    def _():
        _segment_copies(ls_ref, gb_ref, nc_ref, tile0 + j - 1, n_exp, make(1 - slot), "wait")

    @pl.when(j == nj - 1)
    def _():
        _segment_copies(ls_ref, gb_ref, nc_ref, tile0 + j, n_exp, make(slot), "wait")


def _dispatch(seg, tails, lr_rows, tok, xs, *, tile0, rows):
    n, d = tok.shape
    tm = ROW_TILE
    n_exp = tails[0].shape[0]
    first = xs is None
    nloc = TOPK_E * tm + n_exp * CHUNK
    in_specs = [pl.BlockSpec((1, TOPK_E, tm), lambda j, *_: (tile0 + j, 0, 0)),
                pl.BlockSpec((tm, d), lambda j, *_: (j, 0))]
    args = [lr_rows, tok]
    if not first:
        in_specs.append(pl.BlockSpec(memory_space=pl.ANY))
        args.append(xs)
    return pl.pallas_call(
        functools.partial(_dispatch_body, tile0=tile0, n_exp=n_exp, first=first),
        grid_spec=pltpu.PrefetchScalarGridSpec(
            num_scalar_prefetch=5,
            grid=(n // tm,),
            in_specs=in_specs,
            out_specs=pl.BlockSpec(memory_space=pl.ANY),
            scratch_shapes=[pltpu.VMEM((2, nloc, d), F32), pltpu.VMEM((CHUNK, d), F32),
                            pltpu.SemaphoreType.DMA((2,)), pltpu.SemaphoreType.DMA(())],
        ),
        out_shape=jax.ShapeDtypeStruct((rows, d), F32),
        input_output_aliases={} if first else {7: 0},
        compiler_params=_cparams(("arbitrary",)),
        name="dispatch",
    )(*seg, *tails, *args)


def _moe_body(te_ref, nt_ref, x_ref, wu_ref, bg_ref, bl_ref, wd_ref, bd_ref, o_ref, wg_s, wl_s, wd_s):
    g = pl.program_id(0)

    @pl.when((g == 0) | (te_ref[g] != te_ref[jnp.maximum(g - 1, 0)]))
    def _():
        w2 = 2 * LANES
        r = lax.broadcasted_iota(I32, (w2, w2), 0)
        c = lax.broadcasted_iota(I32, (w2, w2), 1)
        src = jnp.where(c < LANES, 2 * c, 2 * (c - LANES) + 1)
        perm = jnp.where(r == src, 1.0, 0.0).astype(BF16)
        for k in range(wu_ref.shape[2] // w2):
            pair = jnp.dot(wu_ref[0, :, k * w2:(k + 1) * w2].astype(BF16), perm, preferred_element_type=F32)
            wg_s[:, k * LANES:(k + 1) * LANES] = pair[:, :LANES].astype(BF16)
            wl_s[:, k * LANES:(k + 1) * LANES] = pair[:, LANES:].astype(BF16)
        wd_s[...] = wd_ref[0].astype(BF16)

    @pl.when(g < nt_ref[0])
    def _():
        x = x_ref[...].astype(BF16)
        glu = jnp.minimum(jnp.dot(x, wg_s[...], preferred_element_type=F32) + bg_ref[0], SWIGLU_LIMIT)
        lin = jnp.clip(jnp.dot(x, wl_s[...], preferred_element_type=F32) + bl_ref[0], -SWIGLU_LIMIT, SWIGLU_LIMIT)
        act = glu * _sigmoid(SWIGLU_ALPHA * glu) * (lin + 1.0)
        o_ref[...] = jnp.dot(act.astype(BF16), wd_s[...], preferred_element_type=F32) + bd_ref[0]

    @pl.when(g >= nt_ref[0])
    def _():
        o_ref[...] = jnp.zeros_like(o_ref)


def _moe(tile_expert, n_tiles, xs, wu, bg, bl, wd, bd):
    r, d = xs.shape
    tm = MOE_TILE
    dff = wd.shape[1]
    row = lambda g, te, nt: (jnp.minimum(g, nt[0] - 1), 0)
    wsp = lambda a: pl.BlockSpec((1,) + a.shape[1:], lambda g, te, nt: (te[g], 0, 0))
    return pl.pallas_call(
        _moe_body,
        grid_spec=pltpu.PrefetchScalarGridSpec(
            num_scalar_prefetch=2,
            grid=(r // tm,),
            in_specs=[pl.BlockSpec((tm, d), row), wsp(wu), wsp(bg), wsp(bl), wsp(wd), wsp(bd)],
            out_specs=pl.BlockSpec((tm, d), lambda g, te, nt: (g, 0)),
            scratch_shapes=[pltpu.VMEM((d, dff), BF16), pltpu.VMEM((d, dff), BF16), pltpu.VMEM((dff, d), BF16)],
        ),
        out_shape=jax.ShapeDtypeStruct((r, d), F32),
        compiler_params=_cparams(("arbitrary",)),
        name="moe",
    )(tile_expert, n_tiles, xs, wu, bg, bl, wd, bd)


def _combine_body(ls_ref, gb_ref, nc_ref, lr_ref, g4_ref, x1_ref, gt2_ref, nf_ref, ys_ref, y_ref, loc, sem,
                  *, tile0, n_exp):
    j = pl.program_id(0)
    nj = pl.num_programs(0)
    slot = j % 2
    nloc, tm = loc.shape[1], x1_ref.shape[0]

    def make(slot_):
        return lambda ls, gb: pltpu.make_async_copy(
            ys_ref.at[pl.ds(gb, CHUNK)], loc.at[slot_, pl.ds(ls, CHUNK)], sem.at[slot_])

    @pl.when(j == 0)
    def _():
        loc[...] = jnp.zeros_like(loc)
        _segment_copies(ls_ref, gb_ref, nc_ref, tile0, n_exp, make(0), "start")

    @pl.when(j + 1 < nj)
    def _():
        _segment_copies(ls_ref, gb_ref, nc_ref, tile0 + j + 1, n_exp, make(1 - slot), "start")

    _segment_copies(ls_ref, gb_ref, nc_ref, tile0 + j, n_exp, make(slot), "wait")

    lane = lax.broadcasted_iota(I32, (tm, nloc), 1)
    lr = lr_ref[...]
    g4 = g4_ref[...]
    w = jnp.zeros((tm, nloc), F32)
    for k in range(TOPK_E):
        w = w + jnp.where(lane == lr[:, k:k + 1], g4[:, k:k + 1], 0.0)
    wh, wl = _split(w)
    yb = loc[slot].astype(BF16)
    ff = jnp.dot(wh, yb, preferred_element_type=F32) + jnp.dot(wl, yb, preferred_element_type=F32)
    y_ref[...] = _rms(x1_ref[...] + gt2_ref[0] * ff, nf_ref[...])


def _combine(seg, lr, g4, x1, gt2, nf, ys, *, tile0, n_exp, t):
    n, d = x1.shape
    tm = min(ROW_TILE, t)
    nt = t // tm
    nloc = TOPK_E * tm + n_exp * CHUNK
    if gt2.shape[1] == 1:
        mod_spec = pl.BlockSpec((1, 1, d), lambda j, *_: (j // nt, 0, 0))
    else:
        mod_spec = pl.BlockSpec((1, tm, d), lambda j, *_: (j // nt, j % nt, 0))
    return pl.pallas_call(
        functools.partial(_combine_body, tile0=tile0, n_exp=n_exp),
        grid_spec=pltpu.PrefetchScalarGridSpec(
            num_scalar_prefetch=3,
            grid=(n // tm,),
            in_specs=[pl.BlockSpec((tm, TOPK_E), lambda j, *_: (tile0 + j, 0)),
                      pl.BlockSpec((tm, TOPK_E), lambda j, *_: (tile0 + j, 0)),
                      pl.BlockSpec((tm, d), lambda j, *_: (j, 0)),
                      mod_spec,
                      pl.BlockSpec((1, d), lambda j, *_: (0, 0)),
                      pl.BlockSpec(memory_space=pl.ANY)],
            out_specs=pl.BlockSpec((tm, d), lambda j, *_: (j, 0)),
            scratch_shapes=[pltpu.VMEM((2, nloc, d), F32), pltpu.SemaphoreType.DMA((2,))],
        ),
        out_shape=jax.ShapeDtypeStruct((n, d), F32),
        compiler_params=_cparams(("arbitrary",)),
        name="combine",
    )(*seg, lr, g4, x1, gt2, nf, ys)


def _block_diag(w):
    g, a, b = w.shape
    eye = jnp.eye(g, dtype=w.dtype)
    return (eye[:, None, :, None] * w[:, :, None, :]).reshape(g * a, g * b)


def _layer0(a):
    return a.reshape(a.shape[1:])

def kernel(x_prompt, x_sample, cache_k, cache_v, state_h, state_conv, page_table, c_prompt, c_sample,
           w_ada, b_ada, norm1, norm2, w_in, conv_w, conv_b, w_rg_a, b_rg_a, w_rg_i, b_rg_i, rg_lambda,
           w_proj_a, w_proj_b, w_merge, b_merge, w_out, w_router, b_router, w_up, b_up, w_down, b_down, norm_f):
    assert w_ada.shape[0] == 1, "single trunk layer"
    bp, t, d = x_prompt.shape
    db, ts, _ = x_sample.shape
    _, n_phys, ps, nh, _ = cache_k.shape
    hd = nh * HEAD_DIM
    da = hd
    dr = conv_w.shape[2]
    n_pages = page_table.shape[1]
    ne = w_router.shape[2]
    np_, ns_ = bp * t, db * ts
    nblk = t // MOBA_BLOCK
    assert t % TOK_TILE == 0 and ns_ == TOK_TILE and n_pages % 16 == 0
    assert (n_pages * ps) % MOBA_BLOCK == 0 and (n_pages * ps) // MOBA_BLOCK >= MOBA_TOPK

    mod = _ada(jnp.concatenate([c_prompt, c_sample], axis=0), _layer0(w_ada), _layer0(b_ada))
    mod = mod.reshape(bp + db, N_MOD, d)
    mp = [mod[:bp, i][:, None, :] for i in range(N_MOD)]
    ms = [jnp.tile(mod[bp:, i], (ts, 1))[None] for i in range(N_MOD)]

    g1, g2, nf = norm1, norm2, norm_f[None]
    w_in0 = _layer0(w_in)
    wqkvt, wk, wug = w_in0[:, :3 * da].T, w_in0[:, da:2 * da], w_in0[:, 3 * da:]

    xp2 = x_prompt.reshape(np_, d)
    xs2 = jnp.swapaxes(x_sample, 0, 1).reshape(ns_, d)

    qt_p, kt_p, vt_p, u_p, g_p, kb_p, vtb_p, km_p = _proj(
        xp2, mp[0], mp[1], g1, wqkvt.astype(BF16), wug.astype(BF16), wk.astype(BF16), nb=bp, t=t)
    qt_s, kt_s, vt_s, u_s, g_s = _proj(xs2, ms[0], ms[1], g1, wqkvt, wug, nb=1, t=ns_)

    oa_p = _attn_p(qt_p, kb_p, vtb_p, km_p.reshape(bp, nblk, da), nb=bp, t=t)

    kc4 = jnp.transpose(cache_k.reshape(n_phys, ps, nh, HEAD_DIM), (0, 2, 3, 1))
    vc4 = jnp.transpose(cache_v.reshape(n_phys, ps, nh, HEAD_DIM), (0, 2, 3, 1))
    pt_flat = page_table.reshape(-1)
    km = _kmean_s(pt_flat, kc4.reshape(n_phys, hd, ps), db=db, n_pages=n_pages)
    km = jnp.transpose(km, (0, 2, 1, 3)).reshape(db, hd, -1)
    tmaj = lambda a: jnp.transpose(a.reshape(hd, ts, db), (2, 0, 1))
    qs3, kn3, vn3 = tmaj(qt_s), tmaj(kt_s), tmaj(vt_s)
    idx = _sel_s(qs3, km)
    ot_s = _attn_s(idx.reshape(-1), pt_flat, qs3, kn3, vn3, kc4, vc4, n_pages=n_pages)
    oa_s = jnp.transpose(ot_s, (2, 0, 1)).reshape(ns_, hd)

    cw, cb = _layer0(conv_w), conv_b
    wa, wi = _block_diag(_layer0(w_rg_a)), _block_diag(_layer0(w_rg_i))
    ba, bi, lam = b_rg_a, b_rg_i, rg_lambda
    ob_p, hl_p, cn_p = _rg_p(u_p, g_p, cw, cb, wa, ba, wi, bi, lam, nb=bp, t=t)
    ob_s, hl_s, cn_s = _rg_s(u_s.reshape(ts, db, dr), g_s.reshape(ts, db, dr),
                             jnp.swapaxes(_layer0(state_conv), 0, 1), _layer0(state_h),
                             cw, cb, wa, ba, wi, bi, lam)

    wsm = (g1, g2, _layer0(w_merge).astype(BF16), b_merge, _layer0(w_proj_a).astype(BF16),
           _layer0(w_proj_b).astype(BF16), _layer0(w_out).astype(BF16), _layer0(w_router), b_router)
    x1_p, tok_p, gs_p, cnt_p = _post(xp2, mp[:5], oa_p, ob_p, *wsm, nb=bp, t=t)
    x1_s, tok_s, gs_s, cnt_s = _post(xs2, ms[:5], oa_s.astype(BF16), ob_s.reshape(ns_, dr).astype(BF16), *wsm,
                                     nb=1, t=ns_)

    gs = jnp.concatenate([gs_p, gs_s], axis=0)
    cnt = jnp.concatenate([cnt_p, cnt_s], axis=0).reshape(-1, ne).astype(I32)
    n_rt = cnt.shape[0]
    seg = (cnt + CHUNK - 1) // CHUNK * CHUNK
    lstart = jnp.cumsum(seg, axis=1) - seg
    tot = jnp.sum(seg, axis=0)
    tiles_e = (tot + MOE_TILE - 1) // MOE_TILE
    tile_end = jnp.cumsum(tiles_e)
    poff = (tile_end - tiles_e) * MOE_TILE
    gbase = poff[None, :] + jnp.cumsum(seg, axis=0) - seg
    seg_tabs = (lstart.reshape(-1), gbase.reshape(-1), (seg // CHUNK).reshape(-1))
    tail_tabs = (poff + tot, (tiles_e * MOE_TILE - tot) // CHUNK)
    n_all = np_ + ns_
    rows_max = n_all * TOPK_E + n_rt * ne * (CHUNK - 1) + ne * MOE_TILE
    g_max = -(-rows_max // MOE_TILE)
    n_tiles = tile_end[-1:].astype(I32)
    tile_id = jnp.minimum(jnp.arange(g_max, dtype=I32), n_tiles - 1)
    te = jnp.sum((tile_end[None, :] <= tile_id[:, None]).astype(I32), axis=1)
    lr, g4 = _route(gs, lstart.astype(F32)[:, None, :])
    lr_rows = jnp.swapaxes(lr.reshape(n_rt, ROW_TILE, TOPK_E), 1, 2)
    rt_p = np_ // ROW_TILE

    xs = _dispatch(seg_tabs, tail_tabs, lr_rows, tok_p, None, tile0=0, rows=g_max * MOE_TILE)
    xs = _dispatch(seg_tabs, tail_tabs, lr_rows, tok_s, xs, tile0=rt_p, rows=g_max * MOE_TILE)
    b_up0 = _layer0(b_up)
    ys = _moe(te, n_tiles, xs, _layer0(w_up), b_up0[:, None, 0::2], b_up0[:, None, 1::2],
              _layer0(w_down), _layer0(b_down)[:, None, :])
    y_p = _combine(seg_tabs, lr, g4, x1_p, mp[5], nf, ys, tile0=0, n_exp=ne, t=t)
    y_s = _combine(seg_tabs, lr, g4, x1_s, ms[5], nf, ys, tile0=rt_p, n_exp=ne, t=ns_)

    y_prompt = y_p.reshape(bp, t, d)
    y_sample = jnp.swapaxes(y_s.reshape(ts, db, d), 0, 1)
    to5 = lambda a: jnp.transpose(a.reshape(bp, nh, HEAD_DIM, t), (0, 3, 1, 2))[None]
    k_prompt, v_prompt = to5(kt_p), to5(vt_p)
    to5s = lambda a: jnp.transpose(a.reshape(nh, HEAD_DIM, ts, db), (3, 2, 0, 1))[None]
    k_sample, v_sample = to5s(kt_s), to5s(vt_s)
    h_prompt = hl_p.reshape(1, bp, dr)
    conv_prompt = cn_p[None]
    h_sample = hl_s[None]
    conv_sample = jnp.swapaxes(cn_s, 0, 1)[None]
    return (y_prompt, y_sample, k_prompt, v_prompt, h_prompt, conv_prompt,
            k_sample, v_sample, h_sample, conv_sample)
```

```python
import functools

import jax
import jax.numpy as jnp
from jax import lax
from jax.experimental import pallas as pl
from jax.experimental.pallas import tpu as pltpu

F32 = jnp.float32
BF16 = jnp.bfloat16
I32 = jnp.int32

HEAD_DIM = 64
MOBA_BLOCK = 256
MOBA_TOPK = 3
RG_C = 8.0
TOPK_E = 4
SWIGLU_ALPHA = 1.702
SWIGLU_LIMIT = 7.0
EPS = 1e-6
N_MOD = 6

LANES = 128
TOK_TILE = 512
ATTN_HEADS = 4
RG_TILE = 256
MOE_TILE = 512
ROW_TILE = 256
CHUNK = 8
BIG_CHUNK = 32
KMEAN_PAGES = 32
VMEM_LIMIT = 56 * 1024 * 1024


def _cparams(sem):
    return pltpu.CompilerParams(dimension_semantics=sem, vmem_limit_bytes=VMEM_LIMIT)


def _dot(a, b):
    return jnp.dot(a.astype(BF16), b.astype(BF16), preferred_element_type=F32)


def _dot_t(a, b):
    return lax.dot_general(a.astype(BF16), b.astype(BF16), (((1,), (1,)), ((), ())),
                           preferred_element_type=F32)


def _split(a):
    hi = a.astype(BF16)
    lo = (a - hi.astype(F32)).astype(BF16)
    return hi, lo


def _dot3(a, b):
    ah, al = _split(a)
    bh, bl = _split(b)
    return (jnp.dot(ah, bh, preferred_element_type=F32)
            + jnp.dot(al, bh, preferred_element_type=F32)
            + jnp.dot(ah, bl, preferred_element_type=F32))


def _dot3_t(a, b):
    ah, al = _split(a)
    bh, bl = _split(b)
    dn = (((1,), (1,)), ((), ()))
    return (lax.dot_general(ah, bh, dn, preferred_element_type=F32)
            + lax.dot_general(al, bh, dn, preferred_element_type=F32)
            + lax.dot_general(ah, bl, dn, preferred_element_type=F32))


def _sigmoid(x):
    return 1.0 / (1.0 + jnp.exp(-x))


def _rms(x, g):
    return x * lax.rsqrt(jnp.mean(x * x, axis=-1, keepdims=True) + EPS) * g


def _norm_mod(x, g, shift, scale):
    return _rms(x, g) * (1.0 + scale) + shift


def _gelu_tanh(x):
    return 0.5 * x * (1.0 + jnp.tanh(0.7978845608028654 * (x + 0.044715 * (x * x * x))))


def _ada_body(c_ref, w_ref, b_ref, o_ref):
    c = c_ref[...]
    o_ref[...] = _dot3(c * _sigmoid(c), w_ref[...]) + b_ref[...]


def _ada(c, w, b):
    n, d = c.shape
    e = w.shape[1]
    te = 1024
    return pl.pallas_call(
        _ada_body,
        grid=(e // te,),
        in_specs=[pl.BlockSpec((n, d), lambda j: (0, 0)),
                  pl.BlockSpec((d, te), lambda j: (0, j)),
                  pl.BlockSpec((1, te), lambda j: (0, j))],
        out_specs=pl.BlockSpec((n, te), lambda j: (0, j)),
        out_shape=jax.ShapeDtypeStruct((n, e), F32),
        compiler_params=_cparams(("arbitrary",)),
        name="ada",
    )(c, w, b.reshape(1, e))


def _proj_body(x_ref, sh_ref, sc_ref, g1_ref, wqkvt_ref, wug_ref, *rest, prompt):
    h = _norm_mod(x_ref[...], g1_ref[...], sh_ref[0], sc_ref[0])
    mm, mmt = (_dot, _dot_t) if prompt else (_dot3, _dot3_t)
    if prompt:
        wk_ref, qt_ref, kt_ref, vt_ref, u_ref, g_ref, kb_ref, vtb_ref, km_ref = rest
    else:
        qt_ref, kt_ref, vt_ref, u_ref, g_ref = rest
    da = qt_ref.shape[1]
    qkvt = mmt(wqkvt_ref[...], h)
    qt_ref[0] = qkvt[:da]
    kt_ref[0] = qkvt[da:2 * da]
    vt = qkvt[2 * da:]
    vt_ref[0] = vt
    ug = mm(h, wug_ref[...])
    dr = u_ref.shape[1]
    u_ref[...] = ug[:, :dr]
    g_ref[...] = ug[:, dr:]
    if prompt:
        k = mm(h, wk_ref[...])
        kb_ref[...] = k.astype(BF16)
        vtb_ref[0] = vt.astype(BF16)
        means = [jnp.sum(k[n * MOBA_BLOCK:(n + 1) * MOBA_BLOCK], axis=0, keepdims=True) * (1.0 / MOBA_BLOCK)
                 for n in range(k.shape[0] // MOBA_BLOCK)]
        km_ref[0] = jnp.concatenate(means, axis=0)


def _proj(x2d, shift, scale, g1, wqkvt, wug, wk=None, *, nb, t):
    prompt = wk is not None
    n, d = x2d.shape
    tm = min(TOK_TILE, t)
    nt = t // tm
    da = wqkvt.shape[0] // 3
    dr = wug.shape[1] // 2
    mrows = shift.shape[1]
    if mrows == 1:
        mod_spec = pl.BlockSpec((1, 1, d), lambda b, j: (b, 0, 0))
    else:
        mod_spec = pl.BlockSpec((1, tm, d), lambda b, j: (b, j, 0))
    full = lambda a: pl.BlockSpec(a.shape, lambda b, j: (0,) * a.ndim)
    tok = lambda w: pl.BlockSpec((tm, w), lambda b, j: (b * nt + j, 0))
    featmaj = pl.BlockSpec((1, da, tm), lambda b, j: (b, 0, j))
    in_specs = [tok(d), mod_spec, mod_spec, full(g1), full(wqkvt), full(wug)]
    out_specs = [featmaj, featmaj, featmaj, tok(dr), tok(dr)]
    out_shape = [jax.ShapeDtypeStruct((nb, da, t), F32)] * 3 + [jax.ShapeDtypeStruct((n, dr), F32)] * 2
    args = [x2d, shift, scale, g1, wqkvt, wug]
    if prompt:
        bpt = tm // MOBA_BLOCK
        in_specs.append(full(wk))
        args.append(wk)
        out_specs += [tok(da), featmaj,
                      pl.BlockSpec((1, bpt, da), lambda b, j: (b * nt + j, 0, 0))]
        out_shape += [jax.ShapeDtypeStruct((n, da), BF16),
                      jax.ShapeDtypeStruct((nb, da, t), BF16),
                      jax.ShapeDtypeStruct((nb * nt, bpt, da), F32)]
    return pl.pallas_call(
        functools.partial(_proj_body, prompt=prompt),
        grid=(nb, nt),
        in_specs=in_specs,
        out_specs=out_specs,
        out_shape=out_shape,
        compiler_params=_cparams(("arbitrary", "arbitrary")),
        name="proj_p" if prompt else "proj_s",
    )(*args)


def _attn_p_body(qt_ref, kb_ref, kd_ref, vtb_ref, vd_ref, km_ref, o_ref, sel_ref, *, nblk):
    i = pl.program_id(2)
    bs = MOBA_BLOCK
    krow = lax.broadcasted_iota(I32, (bs, bs), 0)
    qcol = lax.broadcasted_iota(I32, (bs, bs), 1)
    blk = lax.broadcasted_iota(I32, (nblk, bs), 0)
    valid = blk < i
    nhh = qt_ref.shape[1] // HEAD_DIM
    heads = [slice(hh * HEAD_DIM, (hh + 1) * HEAD_DIM) for hh in range(nhh)]

    def run(npast):
        outs = []
        for hh, hs in enumerate(heads):
            qt = qt_ref[0, hs, :]
            sv = jnp.where(valid, _dot3(km_ref[0, :, hs], qt), -jnp.inf)
            cnt = jnp.zeros_like(sv)
            for m in range(nblk - 1):
                sm = sv[m:m + 1, :]
                beats = (sm > sv) | ((sm == sv) & (m < blk))
                cnt = cnt + jnp.where(beats, 1.0, 0.0)
            sel_ref[hh] = jnp.where(valid & (cnt < MOBA_TOPK), 1.0, 0.0)

            qb = (qt * (HEAD_DIM ** -0.5)).astype(BF16)
            sd = jnp.dot(kd_ref[:, hs], qb, preferred_element_type=F32)
            sd = jnp.where(krow <= qcol, sd, -jnp.inf)
            sps = []
            if npast:
                sp = jnp.dot(kb_ref[0:npast * bs, hs], qb, preferred_element_type=F32)
                sps = [jnp.where(sel_ref[hh, j:j + 1, :] > 0.0, sp[j * bs:(j + 1) * bs], -jnp.inf)
                       for j in range(npast)]
            mx = jnp.max(sd, axis=0, keepdims=True)
            for sj in sps:
                mx = jnp.maximum(mx, jnp.max(sj, axis=0, keepdims=True))
            pd = jnp.exp(sd - mx)
            den = jnp.sum(pd, axis=0, keepdims=True)
            pps = []
            for sj in sps:
                pj = jnp.exp(sj - mx)
                den = den + jnp.sum(pj, axis=0, keepdims=True)
                pps.append(pj.astype(BF16))
            acc = jnp.dot(vd_ref[0, hs, :], pd.astype(BF16), preferred_element_type=F32)
            if npast:
                acc = acc + jnp.dot(vtb_ref[0, hs, 0:npast * bs], jnp.concatenate(pps, axis=0),
                                    preferred_element_type=F32)
            outs.append(acc / den)
        o_ref[...] = jnp.concatenate(outs, axis=0).T.astype(o_ref.dtype)

    bounds = sorted(set(range(1, nblk, 2)) | {nblk - 1})
    lo = -1
    for npast in bounds:
        pl.when((i > lo) & (i <= npast))(functools.partial(run, npast))
        lo = npast


def _attn_p(qt, kb, vtb, km, *, nb, t):
    da = qt.shape[1]
    nblk = t // MOBA_BLOCK
    hw = ATTN_HEADS * HEAD_DIM
    nhp = da // hw
    return pl.pallas_call(
        functools.partial(_attn_p_body, nblk=nblk),
        grid=(nb, nhp, nblk),
        in_specs=[pl.BlockSpec((1, hw, MOBA_BLOCK), lambda b, h, i: (b, h, i)),
                  pl.BlockSpec((t, hw), lambda b, h, i: (b, h)),
                  pl.BlockSpec((MOBA_BLOCK, hw), lambda b, h, i: (b * nblk + i, h)),
                  pl.BlockSpec((1, hw, t), lambda b, h, i: (b, h, 0)),
                  pl.BlockSpec((1, hw, MOBA_BLOCK), lambda b, h, i: (b, h, i)),
                  pl.BlockSpec((1, nblk, hw), lambda b, h, i: (b, 0, h))],
        out_specs=pl.BlockSpec((MOBA_BLOCK, hw), lambda b, h, i: (b * nblk + i, h)),
        out_shape=jax.ShapeDtypeStruct((nb * t, da), BF16),
        scratch_shapes=[pltpu.VMEM((ATTN_HEADS, nblk, MOBA_BLOCK), F32)],
        compiler_params=_cparams(("arbitrary", "arbitrary", "arbitrary")),
        name="attn_p",
    )(qt, kb, kb, vtb, vtb, km)


def _rg_gates(uc, wa_ref, ba_ref, wi_ref, bi_ref, lam_ref):
    r = _sigmoid(_dot3(uc, wa_ref[...]) + ba_ref[...])
    ig = _sigmoid(_dot3(uc, wi_ref[...]) + bi_ref[...])
    nl = -lam_ref[...]
    softplus = jnp.maximum(nl, 0.0) + jnp.log(1.0 + jnp.exp(-jnp.abs(nl)))
    log_a = -RG_C * r * softplus
    a = jnp.exp(log_a)
    bx = jnp.sqrt(1.0 - jnp.exp(2.0 * log_a)) * (ig * uc)
    return a, bx


def _rg_p_body(u_ref, g_ref, cw_ref, cb_ref, wa_ref, ba_ref, wi_ref, bi_ref, lam_ref,
               o_ref, hl_ref, cn_ref, ubuf, hc):
    j = pl.program_id(1)
    tt = u_ref.shape[0]
    pad = 8

    @pl.when(j == 0)
    def _():
        ubuf[0:pad] = jnp.zeros((pad, ubuf.shape[1]), F32)
        hc[...] = jnp.zeros_like(hc)

    u = u_ref[...]
    ubuf[pad:pad + tt] = u
    cw = cw_ref[...]
    kw = cw.shape[0]
    uc = cb_ref[...] + u * cw[kw - 1:kw]
    for jj in range(kw - 1):
        off = pad - (kw - 1) + jj
        uc = uc + ubuf[off:off + tt] * cw[jj:jj + 1]
    a, bx = _rg_gates(uc, wa_ref, ba_ref, wi_ref, bi_ref, lam_ref)

    row = lax.broadcasted_iota(I32, a.shape, 0)
    d = 1
    while d < tt:
        keep = row >= d
        a_sh = jnp.where(keep, pltpu.roll(a, d, 0), 1.0)
        b_sh = jnp.where(keep, pltpu.roll(bx, d, 0), 0.0)
        bx = a * b_sh + bx
        a = a * a_sh
        d *= 2
    h = bx + a * hc[...]
    hc[...] = h[tt - 1:tt]
    hl_ref[0] = h[tt - 1:tt]
    cn_ref[0] = u[tt - (kw - 1):tt]
    ubuf[0:pad] = u[tt - pad:tt]
    o_ref[...] = (h * _gelu_tanh(g_ref[...])).astype(o_ref.dtype)


def _rg_p(u, g, cw, cb, wa, ba, wi, bi, lam, *, nb, t):
    n, dr = u.shape
    tt = min(RG_TILE, t)
    nt = t // tt
    kw = cw.shape[0]
    full = lambda a: pl.BlockSpec(a.shape, lambda b, j: (0,) * a.ndim)
    tok = pl.BlockSpec((tt, dr), lambda b, j: (b * nt + j, 0))
    return pl.pallas_call(
        _rg_p_body,
        grid=(nb, nt),
        in_specs=[tok, tok, full(cw), full(cb), full(wa), full(ba), full(wi), full(bi), full(lam)],
        out_specs=[tok,
                   pl.BlockSpec((1, 1, dr), lambda b, j: (b, 0, 0)),
                   pl.BlockSpec((1, kw - 1, dr), lambda b, j: (b, 0, 0))],
        out_shape=[jax.ShapeDtypeStruct((n, dr), BF16),
                   jax.ShapeDtypeStruct((nb, 1, dr), F32),
                   jax.ShapeDtypeStruct((nb, kw - 1, dr), F32)],
        scratch_shapes=[pltpu.VMEM((8 + tt, dr), F32), pltpu.VMEM((1, dr), F32)],
        compiler_params=_cparams(("arbitrary", "arbitrary")),
        name="rg_p",
    )(u, g, cw, cb, wa, ba, wi, bi, lam)


def _rg_s_body(u_ref, g_ref, cs_ref, h0_ref, cw_ref, cb_ref, wa_ref, ba_ref, wi_ref, bi_ref, lam_ref,
               o_ref, hl_ref, cn_ref):
    ts = u_ref.shape[0]
    cw = cw_ref[...]
    kw = cw.shape[0]
    full = [cs_ref[jj] for jj in range(kw - 1)] + [u_ref[tt] for tt in range(ts)]
    h = h0_ref[...]
    for tt in range(ts):
        uc = cb_ref[...]
        for jj in range(kw):
            uc = uc + full[tt + jj] * cw[jj:jj + 1]
        a, bx = _rg_gates(uc, wa_ref, ba_ref, wi_ref, bi_ref, lam_ref)
        h = a * h + bx
        o_ref[tt] = h * _gelu_tanh(g_ref[tt])
    hl_ref[...] = h
    for jj in range(kw - 1):
        cn_ref[jj] = full[ts + jj]


def _rg_s(u, g, cs, h0, cw, cb, wa, ba, wi, bi, lam):
    ts, db, dr = u.shape
    kw = cw.shape[0]
    return pl.pallas_call(
        _rg_s_body,
        out_shape=[jax.ShapeDtypeStruct((ts, db, dr), F32),
                   jax.ShapeDtypeStruct((db, dr), F32),
                   jax.ShapeDtypeStruct((kw - 1, db, dr), F32)],
        compiler_params=pltpu.CompilerParams(vmem_limit_bytes=VMEM_LIMIT),
        name="rg_s",
    )(u, g, cs, h0, cw, cb, wa, ba, wi, bi, lam)


def _kmean_s_body(pt_ref, *refs, npg):
    del pt_ref
    pages, o_ref = refs[:npg], refs[npg]
    cols = []
    for n in range(npg // 2):
        blk = pages[2 * n][0] + pages[2 * n + 1][0]
        cols.append(jnp.sum(blk, axis=1, keepdims=True) * (1.0 / MOBA_BLOCK))
    o_ref[0, 0] = jnp.concatenate(cols, axis=1)


def _kmean_s(pt_flat, kt3, *, db, n_pages):
    _, hd, ps = kt3.shape
    npg = KMEAN_PAGES if n_pages % KMEAN_PAGES == 0 else KMEAN_PAGES // 2
    nchunk = n_pages // npg
    specs = [pl.BlockSpec((1, hd, ps), functools.partial(
        lambda b, c, pt, p: (pt[b * n_pages + c * npg + p], 0, 0), p=p)) for p in range(npg)]
    return pl.pallas_call(
        functools.partial(_kmean_s_body, npg=npg),
        grid_spec=pltpu.PrefetchScalarGridSpec(
            num_scalar_prefetch=1,
            grid=(db, nchunk),
            in_specs=specs,
            out_specs=pl.BlockSpec((1, 1, hd, npg // 2), lambda b, c, pt: (b, c, 0, 0)),
        ),
        out_shape=jax.ShapeDtypeStruct((db, nchunk, hd, npg // 2), F32),
        compiler_params=_cparams(("arbitrary", "arbitrary")),
        name="kmean_s",
    )(pt_flat, *([kt3] * npg))


def _sel_s_body(qt_ref, km_ref, o_ref, *, nh):
    nb, _, ts = qt_ref.shape
    nc = km_ref.shape[2]
    lane = lax.broadcasted_iota(I32, (nh, nc), 1).astype(F32)

    def one(b, c):
        qt = qt_ref[b]
        km = km_ref[b]
        outs = []
        for t in range(ts):
            prod = qt[:, t:t + 1] * km
            s = jnp.sum(prod.reshape(nh, HEAD_DIM, nc), axis=1)
            for _ in range(MOBA_TOPK):
                m = jnp.max(s, axis=1, keepdims=True)
                idx = jnp.min(jnp.where(s == m, lane, float(nc)), axis=1, keepdims=True)
                outs.append(idx)
                s = jnp.where(lane == idx, -jnp.inf, s)
        o_ref[b] = jnp.concatenate(outs, axis=1).astype(I32)
        return c

    lax.fori_loop(0, nb, one, 0)


def _sel_s(qt, km):
    db, hd, ts = qt.shape
    nc = km.shape[2]
    nh = hd // HEAD_DIM
    nb = 16 if db % 16 == 0 else 1
    return pl.pallas_call(
        functools.partial(_sel_s_body, nh=nh),
        grid=(db // nb,),
        in_specs=[pl.BlockSpec((nb, hd, ts), lambda b: (b, 0, 0)),
                  pl.BlockSpec((nb, hd, nc), lambda b: (b, 0, 0))],
        out_specs=pl.BlockSpec((nb, nh, ts * MOBA_TOPK), lambda b: (b, 0, 0)),
        out_shape=jax.ShapeDtypeStruct((db, nh, ts * MOBA_TOPK), I32),
        compiler_params=_cparams(("arbitrary",)),
        name="sel_s",
    )(qt, km)


def _attn_s_body(idx_ref, pt_ref, qt_ref, knt_ref, vnt_ref, kc_ref, vc_ref, o_ref,
                 kbuf, vbuf, sem, *, nh, ts, n_pages, ppb):
    b = pl.program_id(0)
    nb = pl.num_programs(0)
    nsl = nh * ts * MOBA_TOPK
    ps = kc_ref.shape[3]

    def copies(bb, slot, s):
        h = s // (ts * MOBA_TOPK)
        blk = idx_ref[bb * nsl + s]
        out = []
        for pg in range(ppb):
            page = pt_ref[bb * n_pages + blk * ppb + pg]
            out.append(pltpu.make_async_copy(
                kc_ref.at[page, h], kbuf.at[slot, s, :, pl.ds(pg * ps, ps)], sem.at[slot]))
            out.append(pltpu.make_async_copy(
                vc_ref.at[page, h], vbuf.at[slot, s, :, pl.ds(pg * ps, ps)], sem.at[slot]))
        return out

    def start_all(bb, slot):
        def f(s, c):
            for cp in copies(bb, slot, s):
                cp.start()
            return c
        lax.fori_loop(0, nsl, f, 0)

    def wait_all(bb, slot):
        def f(s, c):
            for cp in copies(bb, slot, s):
                cp.wait()
            return c
        lax.fori_loop(0, nsl, f, 0)

    @pl.when(b == 0)
    def _():
        start_all(0, 0)

    @pl.when(b + 1 < nb)
    def _():
        start_all(b + 1, (b + 1) % 2)

    slot = b % 2
    wait_all(b, slot)

    scale = HEAD_DIM ** -0.5
    tcol = lax.broadcasted_iota(I32, (1, ts), 1)

    def head(h, c):
        hs = pl.ds(pl.multiple_of(h * HEAD_DIM, HEAD_DIM), HEAD_DIM)
        qh = qt_ref[0, hs, :] * scale
        knh = knt_ref[0, hs, :]
        vnh = vnt_ref[0, hs, :]
        cols = []
        for t in range(ts):
            qc = qh[:, t:t + 1]
            s_new = jnp.sum(qc * knh, axis=0, keepdims=True)
            s_new = jnp.where(tcol <= t, s_new, -jnp.inf)
            sls = [(h * ts + t) * MOBA_TOPK + j for j in range(MOBA_TOPK)]
            s_sel = [jnp.sum(qc * kbuf[slot, sl], axis=0, keepdims=True) for sl in sls]
            s_max = s_sel[0]
            for sj in s_sel[1:]:
                s_max = jnp.maximum(s_max, sj)
            m = jnp.maximum(jnp.max(s_new, axis=1, keepdims=True), jnp.max(s_max, axis=1, keepdims=True))
            p_new = jnp.exp(s_new - m)
            ps = [jnp.exp(sj - m) for sj in s_sel]
            p_sum = ps[0]
            pv = ps[0] * vbuf[slot, sls[0]]
            for pj, sl in zip(ps[1:], sls[1:]):
                p_sum = p_sum + pj
                pv = pv + pj * vbuf[slot, sl]
            l = jnp.sum(p_new, axis=1, keepdims=True) + jnp.sum(p_sum, axis=1, keepdims=True)
            acc = jnp.sum(p_new * vnh, axis=1, keepdims=True) + jnp.sum(pv, axis=1, keepdims=True)
            cols.append(acc / l)
        o_ref[0, hs, :] = jnp.concatenate(cols, axis=1)
        return c

    lax.fori_loop(0, nh, head, 0)


def _attn_s(idx_flat, pt_flat, qt, knt, vnt, kc4, vc4, *, n_pages):
    db, hd, ts = qt.shape
    _, nh, dh, ps = kc4.shape
    ppb = MOBA_BLOCK // ps
    nsl = nh * ts * MOBA_TOPK
    vec = pl.BlockSpec((1, hd, ts), lambda b, idx, pt: (b, 0, 0))
    return pl.pallas_call(
        functools.partial(_attn_s_body, nh=nh, ts=ts, n_pages=n_pages, ppb=ppb),
        grid_spec=pltpu.PrefetchScalarGridSpec(
            num_scalar_prefetch=2,
            grid=(db,),
            in_specs=[vec, vec, vec, pl.BlockSpec(memory_space=pl.ANY), pl.BlockSpec(memory_space=pl.ANY)],
            out_specs=vec,
            scratch_shapes=[pltpu.VMEM((2, nsl, dh, MOBA_BLOCK), F32),
                            pltpu.VMEM((2, nsl, dh, MOBA_BLOCK), F32),
                            pltpu.SemaphoreType.DMA((2,))],
        ),
        out_shape=jax.ShapeDtypeStruct((db, hd, ts), F32),
        compiler_params=_cparams(("arbitrary",)),
        name="attn_s",
    )(idx_flat, pt_flat, qt, knt, vnt, kc4, vc4)


def _post_body(x_ref, sh1_ref, sc1_ref, gt1_ref, sh2_ref, sc2_ref, oa_ref, ob_ref,
               g1_ref, g2_ref, wm_ref, bm_ref, wpa_ref, wpb_ref, wo_ref, wr_ref, br_ref,
               x1_ref, tok_ref, gate_ref, cnt_ref):
    x = x_ref[...]
    d = x.shape[1]
    h = _norm_mod(x, g1_ref[...], sh1_ref[0], sc1_ref[0])
    gates = _sigmoid(_dot(h, wm_ref[...]) + bm_ref[...])
    merged = gates[:, :d] * _dot(oa_ref[...], wpa_ref[...]) + gates[:, d:] * _dot(ob_ref[...], wpb_ref[...])
    x1 = x + gt1_ref[0] * _dot(merged, wo_ref[...])
    x1_ref[...] = x1
    h2 = _norm_mod(x1, g2_ref[...], sh2_ref[0], sc2_ref[0])
    tok_ref[...] = h2
    logits = _dot3(h2, wr_ref[...]) + br_ref[...]
    ne = logits.shape[1]
    lane = lax.broadcasted_iota(I32, logits.shape, 1).astype(F32)
    rem = logits
    vals, hots = [], []
    for _ in range(TOPK_E):
        m = jnp.max(rem, axis=1, keepdims=True)
        idx = jnp.min(jnp.where(rem == m, lane, float(ne)), axis=1, keepdims=True)
        hot = lane == idx
        vals.append(m)
        hots.append(hot)
        rem = jnp.where(hot, -jnp.inf, rem)
    es = [jnp.exp(v - vals[0]) for v in vals]
    den = es[0]
    for e in es[1:]:
        den = den + e
    gate = jnp.zeros_like(logits)
    sel = jnp.zeros_like(logits)
    for e, hot in zip(es, hots):
        gate = gate + jnp.where(hot, e / den, 0.0)
        sel = sel + jnp.where(hot, 1.0, 0.0)
    gate_ref[...] = jnp.concatenate([gate, sel], axis=1)
    cnt_ref[0] = jnp.concatenate(
        [jnp.sum(sel[r * ROW_TILE:(r + 1) * ROW_TILE], axis=0, keepdims=True) for r in range(cnt_ref.shape[1])],
        axis=0)


def _post(x2d, mods, oa, ob, g1, g2, wm, bm, wpa, wpb, wo, wr, br, *, nb, t):
    n, d = x2d.shape
    tm = min(TOK_TILE, t)
    nt = t // tm
    ne = wr.shape[1]
    mrows = mods[0].shape[1]
    if mrows == 1:
        mod_spec = pl.BlockSpec((1, 1, d), lambda b, j: (b, 0, 0))
    else:
        mod_spec = pl.BlockSpec((1, tm, d), lambda b, j: (b, j, 0))
    full = lambda a: pl.BlockSpec(a.shape, lambda b, j: (0,) * a.ndim)
    tok = lambda w: pl.BlockSpec((tm, w), lambda b, j: (b * nt + j, 0))
    ws = (g1, g2, wm, bm, wpa, wpb, wo, wr, br)
    return pl.pallas_call(
        _post_body,
        grid=(nb, nt),
        in_specs=[tok(d)] + [mod_spec] * 5 + [tok(oa.shape[1]), tok(ob.shape[1])] + [full(w) for w in ws],
        out_specs=[tok(d), tok(d), tok(2 * ne),
                   pl.BlockSpec((1, tm // ROW_TILE, ne), lambda b, j: (b * nt + j, 0, 0))],
        out_shape=[jax.ShapeDtypeStruct((n, d), F32),
                   jax.ShapeDtypeStruct((n, d), F32),
                   jax.ShapeDtypeStruct((n, 2 * ne), F32),
                   jax.ShapeDtypeStruct((nb * nt, tm // ROW_TILE, ne), F32)],
        compiler_params=_cparams(("arbitrary", "arbitrary")),
        name="post",
    )(x2d, *mods, oa, ob, *ws)


def _route_body(gs_ref, base_ref, pos_ref, g4_ref):
    gs = gs_ref[...]
    ne = gs.shape[1] // 2
    gate, sel = gs[:, :ne], gs[:, ne:]
    tm = gs.shape[0]
    r = lax.broadcasted_iota(I32, (tm, tm), 0)
    c = lax.broadcasted_iota(I32, (tm, tm), 1)
    tri = jnp.where(c < r, 1.0, 0.0).astype(BF16)
    rank = jnp.dot(tri, sel.astype(BF16), preferred_element_type=F32)
    posm = base_ref[0] + rank
    lane = lax.broadcasted_iota(I32, sel.shape, 1).astype(F32)
    rem = sel > 0.0
    ps, gv = [], []
    for _ in range(TOPK_E):
        idx = jnp.min(jnp.where(rem, lane, float(ne)), axis=1, keepdims=True)
        hot = lane == idx
        ps.append(jnp.sum(jnp.where(hot, posm, 0.0), axis=1, keepdims=True))
        gv.append(jnp.sum(jnp.where(hot, gate, 0.0), axis=1, keepdims=True))
        rem = rem & jnp.logical_not(hot)
    pos_ref[...] = jnp.concatenate(ps, axis=1).astype(I32)
    g4_ref[...] = jnp.concatenate(gv, axis=1)


def _route(gs, base):
    n, ne2 = gs.shape
    tm = ROW_TILE
    return pl.pallas_call(
        _route_body,
        grid=(n // tm,),
        in_specs=[pl.BlockSpec((tm, ne2), lambda j: (j, 0)),
                  pl.BlockSpec((1, 1, ne2 // 2), lambda j: (j, 0, 0))],
        out_specs=[pl.BlockSpec((tm, TOPK_E), lambda j: (j, 0)),
                   pl.BlockSpec((tm, TOPK_E), lambda j: (j, 0))],
        out_shape=[jax.ShapeDtypeStruct((n, TOPK_E), I32),
                   jax.ShapeDtypeStruct((n, TOPK_E), F32)],
        compiler_params=_cparams(("arbitrary",)),
        name="route",
    )(gs, base)


def _segment_copies(ls_ref, gb_ref, nc_ref, tile, n_exp, make, act):
    per_big = BIG_CHUNK // CHUNK

    def per_expert(e, c):
        i = tile * n_exp + e
        ls = ls_ref[i]
        gb = gb_ref[i]
        nbig = nc_ref[i] // per_big

        def chunk(rows, base):
            def f(k, c2):
                off = base + k * rows
                cp = make(pl.multiple_of(ls + off, CHUNK), pl.multiple_of(gb + off, CHUNK), rows)
                cp.start() if act == "start" else cp.wait()
                return c2
            return f

        lax.fori_loop(0, nbig, chunk(BIG_CHUNK, 0), 0)
        lax.fori_loop(0, nc_ref[i] - nbig * per_big, chunk(CHUNK, nbig * BIG_CHUNK), 0)
        return c

    lax.fori_loop(0, n_exp, per_expert, 0)


def _dispatch_body(ls_ref, gb_ref, nc_ref, ts_ref, tn_ref, lr_ref, tok_ref, *rest, tile0, n_exp, first):
    if first:
        xs_ref, loc, zrow, sem, zsem = rest
    else:
        _, xs_ref, loc, zrow, sem, zsem = rest
    j = pl.program_id(0)
    nj = pl.num_programs(0)
    slot = j % 2
    nloc, tm = loc.shape[1], tok_ref.shape[0]

    if first:
        @pl.when(j == 0)
        def _():
            zrow[...] = jnp.zeros_like(zrow)

            def tails(act):
                def per_expert(e, c):
                    def per_chunk(k, c2):
                        cp = pltpu.make_async_copy(
                            zrow, xs_ref.at[pl.ds(pl.multiple_of(ts_ref[e] + k * CHUNK, CHUNK), CHUNK)], zsem)
                        cp.start() if act == "start" else cp.wait()
                        return c2
                    lax.fori_loop(0, tn_ref[e], per_chunk, 0)
                    return c
                lax.fori_loop(0, n_exp, per_expert, 0)

            tails("start")
            tails("wait")

    r = lax.broadcasted_iota(I32, (nloc, tm), 0)
    perm = jnp.zeros((nloc, tm), F32)
    for k in range(TOPK_E):
        perm = perm + jnp.where(r == lr_ref[0, k:k + 1, :], 1.0, 0.0)
    loc[slot] = jnp.dot(perm.astype(BF16), tok_ref[...].astype(BF16), preferred_element_type=F32)

    def make(slot_):
        return lambda ls, gb, rows: pltpu.make_async_copy(
            loc.at[slot_, pl.ds(ls, rows)], xs_ref.at[pl.ds(gb, rows)], sem.at[slot_])

    _segment_copies(ls_ref, gb_ref, nc_ref, tile0 + j, n_exp, make(slot), "start")

    @pl.when(j > 0)
    def _():
        _segment_copies(ls_ref, gb_ref, nc_ref, tile0 + j - 1, n_exp, make(1 - slot), "wait")

    @pl.when(j == nj - 1)
    def _():
        _segment_copies(ls_ref, gb_ref, nc_ref, tile0 + j, n_exp, make(slot), "wait")


def _dispatch(seg, tails, lr_rows, tok, xs, *, tile0, rows):
    n, d = tok.shape
    tm = ROW_TILE
    n_exp = tails[0].shape[0]
    first = xs is None
    nloc = TOPK_E * tm + n_exp * CHUNK
    in_specs = [pl.BlockSpec((1, TOPK_E, tm), lambda j, *_: (tile0 + j, 0, 0)),
                pl.BlockSpec((tm, d), lambda j, *_: (j, 0))]
    args = [lr_rows, tok]
    if not first:
        in_specs.append(pl.BlockSpec(memory_space=pl.ANY))
        args.append(xs)
    return pl.pallas_call(
        functools.partial(_dispatch_body, tile0=tile0, n_exp=n_exp, first=first),
        grid_spec=pltpu.PrefetchScalarGridSpec(
            num_scalar_prefetch=5,
            grid=(n // tm,),
            in_specs=in_specs,
            out_specs=pl.BlockSpec(memory_space=pl.ANY),
            scratch_shapes=[pltpu.VMEM((2, nloc, d), F32), pltpu.VMEM((CHUNK, d), F32),
                            pltpu.SemaphoreType.DMA((2,)), pltpu.SemaphoreType.DMA(())],
        ),
        out_shape=jax.ShapeDtypeStruct((rows, d), F32),
        input_output_aliases={} if first else {7: 0},
        compiler_params=_cparams(("arbitrary",)),
        name="dispatch",
    )(*seg, *tails, *args)


def _moe_body(te_ref, nt_ref, x_ref, wu_ref, bg_ref, bl_ref, wd_ref, bd_ref, o_ref, wg_s, wl_s, wd_s):
    g = pl.program_id(0)

    @pl.when((g == 0) | (te_ref[g] != te_ref[jnp.maximum(g - 1, 0)]))
    def _():
        w2 = 2 * LANES
        r = lax.broadcasted_iota(I32, (w2, w2), 0)
        c = lax.broadcasted_iota(I32, (w2, w2), 1)
        src = jnp.where(c < LANES, 2 * c, 2 * (c - LANES) + 1)
        perm = jnp.where(r == src, 1.0, 0.0).astype(BF16)
        for k in range(wu_ref.shape[2] // w2):
            pair = jnp.dot(wu_ref[0, :, k * w2:(k + 1) * w2].astype(BF16), perm, preferred_element_type=F32)
            wg_s[:, k * LANES:(k + 1) * LANES] = pair[:, :LANES].astype(BF16)
            wl_s[:, k * LANES:(k + 1) * LANES] = pair[:, LANES:].astype(BF16)
        wd_s[...] = wd_ref[0].astype(BF16)

    @pl.when(g < nt_ref[0])
    def _():
        x = x_ref[...].astype(BF16)
        glu = jnp.minimum(jnp.dot(x, wg_s[...], preferred_element_type=F32) + bg_ref[0], SWIGLU_LIMIT)
        lin = jnp.clip(jnp.dot(x, wl_s[...], preferred_element_type=F32) + bl_ref[0], -SWIGLU_LIMIT, SWIGLU_LIMIT)
        act = glu * _sigmoid(SWIGLU_ALPHA * glu) * (lin + 1.0)
        o_ref[...] = jnp.dot(act.astype(BF16), wd_s[...], preferred_element_type=F32) + bd_ref[0]

    @pl.when(g >= nt_ref[0])
    def _():
        o_ref[...] = jnp.zeros_like(o_ref)


def _moe(tile_expert, n_tiles, xs, wu, bg, bl, wd, bd):
    r, d = xs.shape
    tm = MOE_TILE
    dff = wd.shape[1]
    row = lambda g, te, nt: (jnp.minimum(g, nt[0] - 1), 0)
    wsp = lambda a: pl.BlockSpec((1,) + a.shape[1:], lambda g, te, nt: (te[g], 0, 0))
    return pl.pallas_call(
        _moe_body,
        grid_spec=pltpu.PrefetchScalarGridSpec(
            num_scalar_prefetch=2,
            grid=(r // tm,),
            in_specs=[pl.BlockSpec((tm, d), row), wsp(wu), wsp(bg), wsp(bl), wsp(wd), wsp(bd)],
            out_specs=pl.BlockSpec((tm, d), lambda g, te, nt: (g, 0)),
            scratch_shapes=[pltpu.VMEM((d, dff), BF16), pltpu.VMEM((d, dff), BF16), pltpu.VMEM((dff, d), BF16)],
        ),
        out_shape=jax.ShapeDtypeStruct((r, d), F32),
        compiler_params=_cparams(("arbitrary",)),
        name="moe",
    )(tile_expert, n_tiles, xs, wu, bg, bl, wd, bd)


def _combine_body(ls_ref, gb_ref, nc_ref, lr_ref, g4_ref, x1_ref, gt2_ref, nf_ref, ys_ref, y_ref, loc, sem,
                  *, tile0, n_exp):
    j = pl.program_id(0)
    nj = pl.num_programs(0)
    slot = j % 2
    nloc, tm = loc.shape[1], x1_ref.shape[0]

    def make(slot_):
        return lambda ls, gb, rows: pltpu.make_async_copy(
            ys_ref.at[pl.ds(gb, rows)], loc.at[slot_, pl.ds(ls, rows)], sem.at[slot_])

    @pl.when(j == 0)
    def _():
        loc[...] = jnp.zeros_like(loc)
        _segment_copies(ls_ref, gb_ref, nc_ref, tile0, n_exp, make(0), "start")

    @pl.when(j + 1 < nj)
    def _():
        _segment_copies(ls_ref, gb_ref, nc_ref, tile0 + j + 1, n_exp, make(1 - slot), "start")

    _segment_copies(ls_ref, gb_ref, nc_ref, tile0 + j, n_exp, make(slot), "wait")

    lane = lax.broadcasted_iota(I32, (tm, nloc), 1)
    lr = lr_ref[...]
    g4 = g4_ref[...]
    w = jnp.zeros((tm, nloc), F32)
    for k in range(TOPK_E):
        w = w + jnp.where(lane == lr[:, k:k + 1], g4[:, k:k + 1], 0.0)
    wh, wl = _split(w)
    yb = loc[slot].astype(BF16)
    ff = jnp.dot(wh, yb, preferred_element_type=F32) + jnp.dot(wl, yb, preferred_element_type=F32)
    y_ref[...] = _rms(x1_ref[...] + gt2_ref[0] * ff, nf_ref[...])


def _combine(seg, lr, g4, x1, gt2, nf, ys, *, tile0, n_exp, t):
    n, d = x1.shape
    tm = min(ROW_TILE, t)
    nt = t // tm
    nloc = TOPK_E * tm + n_exp * CHUNK
    if gt2.shape[1] == 1:
        mod_spec = pl.BlockSpec((1, 1, d), lambda j, *_: (j // nt, 0, 0))
    else:
        mod_spec = pl.BlockSpec((1, tm, d), lambda j, *_: (j // nt, j % nt, 0))
    return pl.pallas_call(
        functools.partial(_combine_body, tile0=tile0, n_exp=n_exp),
        grid_spec=pltpu.PrefetchScalarGridSpec(
            num_scalar_prefetch=3,
            grid=(n // tm,),
            in_specs=[pl.BlockSpec((tm, TOPK_E), lambda j, *_: (tile0 + j, 0)),
                      pl.BlockSpec((tm, TOPK_E), lambda j, *_: (tile0 + j, 0)),
                      pl.BlockSpec((tm, d), lambda j, *_: (j, 0)),
                      mod_spec,
                      pl.BlockSpec((1, d), lambda j, *_: (0, 0)),
                      pl.BlockSpec(memory_space=pl.ANY)],
            out_specs=pl.BlockSpec((tm, d), lambda j, *_: (j, 0)),
            scratch_shapes=[pltpu.VMEM((2, nloc, d), F32), pltpu.SemaphoreType.DMA((2,))],
        ),
        out_shape=jax.ShapeDtypeStruct((n, d), F32),
        compiler_params=_cparams(("arbitrary",)),
        name="combine",
    )(*seg, lr, g4, x1, gt2, nf, ys)


def _block_diag(w):
    g, a, b = w.shape
    eye = jnp.eye(g, dtype=w.dtype)
    return (eye[:, None, :, None] * w[:, :, None, :]).reshape(g * a, g * b)


def _layer0(a):
    return a.reshape(a.shape[1:])

def kernel(x_prompt, x_sample, cache_k, cache_v, state_h, state_conv, page_table, c_prompt, c_sample,
           w_ada, b_ada, norm1, norm2, w_in, conv_w, conv_b, w_rg_a, b_rg_a, w_rg_i, b_rg_i, rg_lambda,
           w_proj_a, w_proj_b, w_merge, b_merge, w_out, w_router, b_router, w_up, b_up, w_down, b_down, norm_f):
    assert w_ada.shape[0] == 1, "single trunk layer"
    bp, t, d = x_prompt.shape
    db, ts, _ = x_sample.shape
    _, n_phys, ps, nh, _ = cache_k.shape
    hd = nh * HEAD_DIM
    da = hd
    dr = conv_w.shape[2]
    n_pages = page_table.shape[1]
    ne = w_router.shape[2]
    np_, ns_ = bp * t, db * ts
    nblk = t // MOBA_BLOCK
    assert t % TOK_TILE == 0 and ns_ == TOK_TILE and n_pages % 16 == 0
    assert (n_pages * ps) % MOBA_BLOCK == 0 and (n_pages * ps) // MOBA_BLOCK >= MOBA_TOPK

    mod = _ada(jnp.concatenate([c_prompt, c_sample], axis=0), _layer0(w_ada), _layer0(b_ada))
    mod = mod.reshape(bp + db, N_MOD, d)
    mp = [mod[:bp, i][:, None, :] for i in range(N_MOD)]
    ms = [jnp.tile(mod[bp:, i], (ts, 1))[None] for i in range(N_MOD)]

    g1, g2, nf = norm1, norm2, norm_f[None]
    w_in0 = _layer0(w_in)
    wqkvt, wk, wug = w_in0[:, :3 * da].T, w_in0[:, da:2 * da], w_in0[:, 3 * da:]

    xp2 = x_prompt.reshape(np_, d)
    xs2 = jnp.swapaxes(x_sample, 0, 1).reshape(ns_, d)

    qt_p, kt_p, vt_p, u_p, g_p, kb_p, vtb_p, km_p = _proj(
        xp2, mp[0], mp[1], g1, wqkvt.astype(BF16), wug.astype(BF16), wk.astype(BF16), nb=bp, t=t)
    qt_s, kt_s, vt_s, u_s, g_s = _proj(xs2, ms[0], ms[1], g1, wqkvt, wug, nb=1, t=ns_)

    oa_p = _attn_p(qt_p, kb_p, vtb_p, km_p.reshape(bp, nblk, da), nb=bp, t=t)

    kc4 = jnp.transpose(cache_k.reshape(n_phys, ps, nh, HEAD_DIM), (0, 2, 3, 1))
    vc4 = jnp.transpose(cache_v.reshape(n_phys, ps, nh, HEAD_DIM), (0, 2, 3, 1))
    pt_flat = page_table.reshape(-1)
    km = _kmean_s(pt_flat, kc4.reshape(n_phys, hd, ps), db=db, n_pages=n_pages)
    km = jnp.transpose(km, (0, 2, 1, 3)).reshape(db, hd, -1)
    tmaj = lambda a: jnp.transpose(a.reshape(hd, ts, db), (2, 0, 1))
    qs3, kn3, vn3 = tmaj(qt_s), tmaj(kt_s), tmaj(vt_s)
    idx = _sel_s(qs3, km)
    ot_s = _attn_s(idx.reshape(-1), pt_flat, qs3, kn3, vn3, kc4, vc4, n_pages=n_pages)
    oa_s = jnp.transpose(ot_s, (2, 0, 1)).reshape(ns_, hd)

    cw, cb = _layer0(conv_w), conv_b
    wa, wi = _block_diag(_layer0(w_rg_a)), _block_diag(_layer0(w_rg_i))
    ba, bi, lam = b_rg_a, b_rg_i, rg_lambda
    ob_p, hl_p, cn_p = _rg_p(u_p, g_p, cw, cb, wa, ba, wi, bi, lam, nb=bp, t=t)
    ob_s, hl_s, cn_s = _rg_s(u_s.reshape(ts, db, dr), g_s.reshape(ts, db, dr),
                             jnp.swapaxes(_layer0(state_conv), 0, 1), _layer0(state_h),
                             cw, cb, wa, ba, wi, bi, lam)

    wsm = (g1, g2, _layer0(w_merge).astype(BF16), b_merge, _layer0(w_proj_a).astype(BF16),
           _layer0(w_proj_b).astype(BF16), _layer0(w_out).astype(BF16), _layer0(w_router), b_router)
    x1_p, tok_p, gs_p, cnt_p = _post(xp2, mp[:5], oa_p, ob_p, *wsm, nb=bp, t=t)
    x1_s, tok_s, gs_s, cnt_s = _post(xs2, ms[:5], oa_s.astype(BF16), ob_s.reshape(ns_, dr).astype(BF16), *wsm,
                                     nb=1, t=ns_)

    gs = jnp.concatenate([gs_p, gs_s], axis=0)
    cnt = jnp.concatenate([cnt_p, cnt_s], axis=0).reshape(-1, ne).astype(I32)
    n_rt = cnt.shape[0]
    seg = (cnt + CHUNK - 1) // CHUNK * CHUNK
    lstart = jnp.cumsum(seg, axis=1) - seg
    tot = jnp.sum(seg, axis=0)
    tiles_e = (tot + MOE_TILE - 1) // MOE_TILE
    tile_end = jnp.cumsum(tiles_e)
    poff = (tile_end - tiles_e) * MOE_TILE
    gbase = poff[None, :] + jnp.cumsum(seg, axis=0) - seg
    seg_tabs = (lstart.reshape(-1), gbase.reshape(-1), (seg // CHUNK).reshape(-1))
    tail_tabs = (poff + tot, (tiles_e * MOE_TILE - tot) // CHUNK)
    n_all = np_ + ns_
    rows_max = n_all * TOPK_E + n_rt * ne * (CHUNK - 1) + ne * MOE_TILE
    g_max = -(-rows_max // MOE_TILE)
    n_tiles = tile_end[-1:].astype(I32)
    tile_id = jnp.minimum(jnp.arange(g_max, dtype=I32), n_tiles - 1)
    te = jnp.sum((tile_end[None, :] <= tile_id[:, None]).astype(I32), axis=1)
    lr, g4 = _route(gs, lstart.astype(F32)[:, None, :])
    lr_rows = jnp.swapaxes(lr.reshape(n_rt, ROW_TILE, TOPK_E), 1, 2)
    rt_p = np_ // ROW_TILE

    xs = _dispatch(seg_tabs, tail_tabs, lr_rows, tok_p, None, tile0=0, rows=g_max * MOE_TILE)
    xs = _dispatch(seg_tabs, tail_tabs, lr_rows, tok_s, xs, tile0=rt_p, rows=g_max * MOE_TILE)
    b_up0 = _layer0(b_up)
    ys = _moe(te, n_tiles, xs, _layer0(w_up), b_up0[:, None, 0::2], b_up0[:, None, 1::2],
              _layer0(w_down), _layer0(b_down)[:, None, :])
    y_p = _combine(seg_tabs, lr, g4, x1_p, mp[5], nf, ys, tile0=0, n_exp=ne, t=t)
    y_s = _combine(seg_tabs, lr, g4, x1_s, ms[5], nf, ys, tile0=rt_p, n_exp=ne, t=ns_)

    y_prompt = y_p.reshape(bp, t, d)
    y_sample = jnp.swapaxes(y_s.reshape(ts, db, d), 0, 1)
    to5 = lambda a: jnp.transpose(a.reshape(bp, nh, HEAD_DIM, t), (0, 3, 1, 2))[None]
    k_prompt, v_prompt = to5(kt_p), to5(vt_p)
    to5s = lambda a: jnp.transpose(a.reshape(nh, HEAD_DIM, ts, db), (3, 2, 0, 1))[None]
    k_sample, v_sample = to5s(kt_s), to5s(vt_s)
    h_prompt = hl_p.reshape(1, bp, dr)
    conv_prompt = cn_p[None]
    h_sample = hl_s[None]
    conv_sample = jnp.swapaxes(cn_s, 0, 1)[None]
    return (y_prompt, y_sample, k_prompt, v_prompt, h_prompt, conv_prompt,
            k_sample, v_sample, h_sample, conv_sample)
```

```python
import functools

import jax
import jax.numpy as jnp
from jax import lax
from jax.experimental import pallas as pl
from jax.experimental.pallas import tpu as pltpu

F32 = jnp.float32
BF16 = jnp.bfloat16
I32 = jnp.int32

HEAD_DIM = 64
MOBA_BLOCK = 256
MOBA_TOPK = 3
RG_C = 8.0
TOPK_E = 4
SWIGLU_ALPHA = 1.702
SWIGLU_LIMIT = 7.0
EPS = 1e-6
N_MOD = 6

LANES = 128
TOK_TILE = 512
ATTN_HEADS = 4
RG_TILE = 256
MOE_TILE = 512
ROW_TILE = 256
CHUNK = 8
BIG_CHUNK = 32
KMEAN_PAGES = 32
VMEM_LIMIT = 56 * 1024 * 1024


def _cparams(sem):
    return pltpu.CompilerParams(dimension_semantics=sem, vmem_limit_bytes=VMEM_LIMIT)


def _dot(a, b):
    return jnp.dot(a.astype(BF16), b.astype(BF16), preferred_element_type=F32)


def _dot_t(a, b):
    return lax.dot_general(a.astype(BF16), b.astype(BF16), (((1,), (1,)), ((), ())),
                           preferred_element_type=F32)


def _split(a):
    hi = a.astype(BF16)
    lo = (a - hi.astype(F32)).astype(BF16)
    return hi, lo


def _dot3(a, b):
    ah, al = _split(a)
    bh, bl = _split(b)
    return (jnp.dot(ah, bh, preferred_element_type=F32)
            + jnp.dot(al, bh, preferred_element_type=F32)
            + jnp.dot(ah, bl, preferred_element_type=F32))


def _dot3_t(a, b):
    ah, al = _split(a)
    bh, bl = _split(b)
    dn = (((1,), (1,)), ((), ()))
    return (lax.dot_general(ah, bh, dn, preferred_element_type=F32)
            + lax.dot_general(al, bh, dn, preferred_element_type=F32)
            + lax.dot_general(ah, bl, dn, preferred_element_type=F32))


def _sigmoid(x):
    return 1.0 / (1.0 + jnp.exp(-x))


def _rms(x, g):
    return x * lax.rsqrt(jnp.mean(x * x, axis=-1, keepdims=True) + EPS) * g


def _norm_mod(x, g, shift, scale):
    return _rms(x, g) * (1.0 + scale) + shift


def _gelu_tanh(x):
    return 0.5 * x * (1.0 + jnp.tanh(0.7978845608028654 * (x + 0.044715 * (x * x * x))))


def _ada_body(c_ref, w_ref, b_ref, o_ref):
    c = c_ref[...]
    o_ref[...] = _dot3(c * _sigmoid(c), w_ref[...]) + b_ref[...]


def _ada(c, w, b):
    n, d = c.shape
    e = w.shape[1]
    te = 1024
    return pl.pallas_call(
        _ada_body,
        grid=(e // te,),
        in_specs=[pl.BlockSpec((n, d), lambda j: (0, 0)),
                  pl.BlockSpec((d, te), lambda j: (0, j)),
                  pl.BlockSpec((1, te), lambda j: (0, j))],
        out_specs=pl.BlockSpec((n, te), lambda j: (0, j)),
        out_shape=jax.ShapeDtypeStruct((n, e), F32),
        compiler_params=_cparams(("arbitrary",)),
        name="ada",
    )(c, w, b.reshape(1, e))


def _proj_body(x_ref, sh_ref, sc_ref, g1_ref, wqkvt_ref, wug_ref, *rest, prompt):
    h = _norm_mod(x_ref[...], g1_ref[...], sh_ref[0], sc_ref[0])
    mm, mmt = (_dot, _dot_t) if prompt else (_dot3, _dot3_t)
    if prompt:
        wk_ref, qt_ref, kt_ref, vt_ref, u_ref, g_ref, kb_ref, vtb_ref, km_ref = rest
    else:
        qt_ref, kt_ref, vt_ref, u_ref, g_ref = rest
    da = qt_ref.shape[1]
    qkvt = mmt(wqkvt_ref[...], h)
    qt_ref[0] = qkvt[:da]
    kt_ref[0] = qkvt[da:2 * da]
    vt = qkvt[2 * da:]
    vt_ref[0] = vt
    ug = mm(h, wug_ref[...])
    dr = u_ref.shape[1]
    u_ref[...] = ug[:, :dr]
    g_ref[...] = ug[:, dr:]
    if prompt:
        k = mm(h, wk_ref[...])
        kb_ref[...] = k.astype(BF16)
        vtb_ref[0] = vt.astype(BF16)
        means = [jnp.sum(k[n * MOBA_BLOCK:(n + 1) * MOBA_BLOCK], axis=0, keepdims=True) * (1.0 / MOBA_BLOCK)
                 for n in range(k.shape[0] // MOBA_BLOCK)]
        km_ref[0] = jnp.concatenate(means, axis=0)


def _proj(x2d, shift, scale, g1, wqkvt, wug, wk=None, *, nb, t):
    prompt = wk is not None
    n, d = x2d.shape
    tm = min(TOK_TILE, t)
    nt = t // tm
    da = wqkvt.shape[0] // 3
    dr = wug.shape[1] // 2
    mrows = shift.shape[1]
    if mrows == 1:
        mod_spec = pl.BlockSpec((1, 1, d), lambda b, j: (b, 0, 0))
    else:
        mod_spec = pl.BlockSpec((1, tm, d), lambda b, j: (b, j, 0))
    full = lambda a: pl.BlockSpec(a.shape, lambda b, j: (0,) * a.ndim)
    tok = lambda w: pl.BlockSpec((tm, w), lambda b, j: (b * nt + j, 0))
    featmaj = pl.BlockSpec((1, da, tm), lambda b, j: (b, 0, j))
    in_specs = [tok(d), mod_spec, mod_spec, full(g1), full(wqkvt), full(wug)]
    out_specs = [featmaj, featmaj, featmaj, tok(dr), tok(dr)]
    out_shape = [jax.ShapeDtypeStruct((nb, da, t), F32)] * 3 + [jax.ShapeDtypeStruct((n, dr), F32)] * 2
    args = [x2d, shift, scale, g1, wqkvt, wug]
    if prompt:
        bpt = tm // MOBA_BLOCK
        in_specs.append(full(wk))
        args.append(wk)
        out_specs += [tok(da), featmaj,
                      pl.BlockSpec((1, bpt, da), lambda b, j: (b * nt + j, 0, 0))]
        out_shape += [jax.ShapeDtypeStruct((n, da), BF16),
                      jax.ShapeDtypeStruct((nb, da, t), BF16),
                      jax.ShapeDtypeStruct((nb * nt, bpt, da), F32)]
    return pl.pallas_call(
        functools.partial(_proj_body, prompt=prompt),
        grid=(nb, nt),
        in_specs=in_specs,
        out_specs=out_specs,
        out_shape=out_shape,
        compiler_params=_cparams(("arbitrary", "arbitrary")),
        name="proj_p" if prompt else "proj_s",
    )(*args)


def _attn_p_body(qt_ref, kb_ref, kd_ref, vtb_ref, vd_ref, km_ref, o_ref, sel_ref, *, nblk):
    i = pl.program_id(2)
    bs = MOBA_BLOCK
    krow = lax.broadcasted_iota(I32, (bs, bs), 0)
    qcol = lax.broadcasted_iota(I32, (bs, bs), 1)
    blk = lax.broadcasted_iota(I32, (nblk, bs), 0)
    valid = blk < i
    nhh = qt_ref.shape[1] // HEAD_DIM
    heads = [slice(hh * HEAD_DIM, (hh + 1) * HEAD_DIM) for hh in range(nhh)]

    def run(npast):
        outs = []
        for hh, hs in enumerate(heads):
            qt = qt_ref[0, hs, :]
            sv = jnp.where(valid, _dot3(km_ref[0, :, hs], qt), -jnp.inf)
            cnt = jnp.zeros_like(sv)
            for m in range(nblk - 1):
                sm = sv[m:m + 1, :]
                beats = (sm > sv) | ((sm == sv) & (m < blk))
                cnt = cnt + jnp.where(beats, 1.0, 0.0)
            sel_ref[hh] = jnp.where(valid & (cnt < MOBA_TOPK), 1.0, 0.0)

            qb = (qt * (HEAD_DIM ** -0.5)).astype(BF16)
            sd = jnp.dot(kd_ref[:, hs], qb, preferred_element_type=F32)
            sd = jnp.where(krow <= qcol, sd, -jnp.inf)
            sps = []
            if npast:
                sp = jnp.dot(kb_ref[0:npast * bs, hs], qb, preferred_element_type=F32)
                sps = [jnp.where(sel_ref[hh, j:j + 1, :] > 0.0, sp[j * bs:(j + 1) * bs], -jnp.inf)
                       for j in range(npast)]
            mx = jnp.max(sd, axis=0, keepdims=True)
            for sj in sps:
                mx = jnp.maximum(mx, jnp.max(sj, axis=0, keepdims=True))
            pd = jnp.exp(sd - mx)
            den = jnp.sum(pd, axis=0, keepdims=True)
            pps = []
            for sj in sps:
                pj = jnp.exp(sj - mx)
                den = den + jnp.sum(pj, axis=0, keepdims=True)
                pps.append(pj.astype(BF16))
            acc = jnp.dot(vd_ref[0, hs, :], pd.astype(BF16), preferred_element_type=F32)
            if npast:
                acc = acc + jnp.dot(vtb_ref[0, hs, 0:npast * bs], jnp.concatenate(pps, axis=0),
                                    preferred_element_type=F32)
            outs.append(acc / den)
        o_ref[...] = jnp.concatenate(outs, axis=0).T.astype(o_ref.dtype)

    for npast in range(nblk):
        pl.when(i == npast)(functools.partial(run, npast))


def _attn_p(qt, kb, vtb, km, *, nb, t):
    da = qt.shape[1]
    nblk = t // MOBA_BLOCK
    hw = ATTN_HEADS * HEAD_DIM
    nhp = da // hw
    return pl.pallas_call(
        functools.partial(_attn_p_body, nblk=nblk),
        grid=(nb, nhp, nblk),
        in_specs=[pl.BlockSpec((1, hw, MOBA_BLOCK), lambda b, h, i: (b, h, i)),
                  pl.BlockSpec((t, hw), lambda b, h, i: (b, h)),
                  pl.BlockSpec((MOBA_BLOCK, hw), lambda b, h, i: (b * nblk + i, h)),
                  pl.BlockSpec((1, hw, t), lambda b, h, i: (b, h, 0)),
                  pl.BlockSpec((1, hw, MOBA_BLOCK), lambda b, h, i: (b, h, i)),
                  pl.BlockSpec((1, nblk, hw), lambda b, h, i: (b, 0, h))],
        out_specs=pl.BlockSpec((MOBA_BLOCK, hw), lambda b, h, i: (b * nblk + i, h)),
        out_shape=jax.ShapeDtypeStruct((nb * t, da), BF16),
        scratch_shapes=[pltpu.VMEM((ATTN_HEADS, nblk, MOBA_BLOCK), F32)],
        compiler_params=_cparams(("arbitrary", "arbitrary", "arbitrary")),
        name="attn_p",
    )(qt, kb, kb, vtb, vtb, km)


def _rg_gates(uc, wa_ref, ba_ref, wi_ref, bi_ref, lam_ref):
    r = _sigmoid(_dot3(uc, wa_ref[...]) + ba_ref[...])
    ig = _sigmoid(_dot3(uc, wi_ref[...]) + bi_ref[...])
    nl = -lam_ref[...]
    softplus = jnp.maximum(nl, 0.0) + jnp.log(1.0 + jnp.exp(-jnp.abs(nl)))
    log_a = -RG_C * r * softplus
    a = jnp.exp(log_a)
    bx = jnp.sqrt(1.0 - a * a) * (ig * uc)
    return a, bx


def _rg_p_body(u_ref, g_ref, cw_ref, cb_ref, wa_ref, ba_ref, wi_ref, bi_ref, lam_ref,
               o_ref, hl_ref, cn_ref, ubuf, hc):
    j = pl.program_id(2)
    tt = u_ref.shape[0]
    pad = 8

    @pl.when(j == 0)
    def _():
        ubuf[0:pad] = jnp.zeros((pad, ubuf.shape[1]), F32)
        hc[...] = jnp.zeros_like(hc)

    u = u_ref[...]
    ubuf[pad:pad + tt] = u
    cw = cw_ref[...]
    kw = cw.shape[0]
    uc = cb_ref[...] + u * cw[kw - 1:kw]
    for jj in range(kw - 1):
        off = pad - (kw - 1) + jj
        uc = uc + ubuf[off:off + tt] * cw[jj:jj + 1]
    a, bx = _rg_gates(uc, wa_ref, ba_ref, wi_ref, bi_ref, lam_ref)

    row = lax.broadcasted_iota(I32, a.shape, 0)
    d = 1
    while d < tt:
        keep = row >= d
        a_sh = jnp.where(keep, pltpu.roll(a, d, 0), 1.0)
        b_sh = jnp.where(keep, pltpu.roll(bx, d, 0), 0.0)
        bx = a * b_sh + bx
        a = a * a_sh
        d *= 2
    h = bx + a * hc[...]
    hc[...] = h[tt - 1:tt]
    hl_ref[0] = h[tt - 1:tt]
    cn_ref[0] = u[tt - (kw - 1):tt]
    ubuf[0:pad] = u[tt - pad:tt]
    o_ref[...] = (h * _gelu_tanh(g_ref[...])).astype(o_ref.dtype)


def _rg_p(u, g, cw, cb, wa, ba, wi, bi, lam, *, nb, t):
    n, dr = u.shape
    tt = min(RG_TILE, t)
    nt = t // tt
    kw = cw.shape[0]
    cwid = dr // 2
    vec = lambda a: pl.BlockSpec((a.shape[0], cwid), lambda b, c, j: (0, c))
    diag = pl.BlockSpec((cwid, cwid), lambda b, c, j: (c, c))
    tok = pl.BlockSpec((tt, cwid), lambda b, c, j: (b * nt + j, c))
    return pl.pallas_call(
        _rg_p_body,
        grid=(nb, 2, nt),
        in_specs=[tok, tok, vec(cw), vec(cb), diag, vec(ba), diag, vec(bi), vec(lam)],
        out_specs=[tok,
                   pl.BlockSpec((1, 1, cwid), lambda b, c, j: (b, 0, c)),
                   pl.BlockSpec((1, kw - 1, cwid), lambda b, c, j: (b, 0, c))],
        out_shape=[jax.ShapeDtypeStruct((n, dr), BF16),
                   jax.ShapeDtypeStruct((nb, 1, dr), F32),
                   jax.ShapeDtypeStruct((nb, kw - 1, dr), F32)],
        scratch_shapes=[pltpu.VMEM((8 + tt, cwid), F32), pltpu.VMEM((1, cwid), F32)],
        compiler_params=_cparams(("arbitrary", "arbitrary", "arbitrary")),
        name="rg_p",
    )(u, g, cw, cb, wa, ba, wi, bi, lam)


def _rg_s_body(u_ref, g_ref, cs_ref, h0_ref, cw_ref, cb_ref, wa_ref, ba_ref, wi_ref, bi_ref, lam_ref,
               o_ref, hl_ref, cn_ref):
    ts = u_ref.shape[0]
    cw = cw_ref[...]
    kw = cw.shape[0]
    full = [cs_ref[jj] for jj in range(kw - 1)] + [u_ref[tt] for tt in range(ts)]
    h = h0_ref[...]
    for tt in range(ts):
        uc = cb_ref[...]
        for jj in range(kw):
            uc = uc + full[tt + jj] * cw[jj:jj + 1]
        a, bx = _rg_gates(uc, wa_ref, ba_ref, wi_ref, bi_ref, lam_ref)
        h = a * h + bx
        o_ref[tt] = h * _gelu_tanh(g_ref[tt])
    hl_ref[...] = h
    for jj in range(kw - 1):
        cn_ref[jj] = full[ts + jj]


def _rg_s(u, g, cs, h0, cw, cb, wa, ba, wi, bi, lam):
    ts, db, dr = u.shape
    kw = cw.shape[0]
    return pl.pallas_call(
        _rg_s_body,
        out_shape=[jax.ShapeDtypeStruct((ts, db, dr), F32),
                   jax.ShapeDtypeStruct((db, dr), F32),
                   jax.ShapeDtypeStruct((kw - 1, db, dr), F32)],
        compiler_params=pltpu.CompilerParams(vmem_limit_bytes=VMEM_LIMIT),
        name="rg_s",
    )(u, g, cs, h0, cw, cb, wa, ba, wi, bi, lam)


def _kmean_s_body(pt_ref, *refs, npg):
    del pt_ref
    pages, o_ref = refs[:npg], refs[npg]
    cols = []
    for n in range(npg // 2):
        blk = pages[2 * n][0] + pages[2 * n + 1][0]
        cols.append(jnp.sum(blk, axis=1, keepdims=True) * (1.0 / MOBA_BLOCK))
    o_ref[0, 0] = jnp.concatenate(cols, axis=1)


def _kmean_s(pt_flat, kt3, *, db, n_pages):
    _, hd, ps = kt3.shape
    npg = KMEAN_PAGES if n_pages % KMEAN_PAGES == 0 else KMEAN_PAGES // 2
    nchunk = n_pages // npg
    specs = [pl.BlockSpec((1, hd, ps), functools.partial(
        lambda b, c, pt, p: (pt[b * n_pages + c * npg + p], 0, 0), p=p)) for p in range(npg)]
    return pl.pallas_call(
        functools.partial(_kmean_s_body, npg=npg),
        grid_spec=pltpu.PrefetchScalarGridSpec(
            num_scalar_prefetch=1,
            grid=(db, nchunk),
            in_specs=specs,
            out_specs=pl.BlockSpec((1, 1, hd, npg // 2), lambda b, c, pt: (b, c, 0, 0)),
        ),
        out_shape=jax.ShapeDtypeStruct((db, nchunk, hd, npg // 2), F32),
        compiler_params=_cparams(("arbitrary", "arbitrary")),
        name="kmean_s",
    )(pt_flat, *([kt3] * npg))


def _sel_s_body(q_ref, km_ref, o_ref, *, nh):
    nb, ts, hd = q_ref.shape
    nc = km_ref.shape[1]
    cand = lax.broadcasted_iota(I32, (nc, nh), 0).astype(F32)
    grp = lax.broadcasted_iota(I32, (hd, nh), 0) // HEAD_DIM == lax.broadcasted_iota(I32, (hd, nh), 1)
    head_sum = jnp.where(grp, 1.0, 0.0)

    def one(b, c):
        km = km_ref[b]
        outs = []
        for t in range(ts):
            s = _dot3(q_ref[b, t:t + 1, :] * km, head_sum)
            for _ in range(MOBA_TOPK):
                m = jnp.max(s, axis=0, keepdims=True)
                idx = jnp.min(jnp.where(s == m, cand, float(nc)), axis=0, keepdims=True)
                outs.append(idx)
                s = jnp.where(cand == idx, -jnp.inf, s)
        o_ref[b] = jnp.concatenate(outs, axis=0).astype(I32)
        return c

    lax.fori_loop(0, nb, one, 0)


def _sel_s(q, km):
    db, ts, hd = q.shape
    nc = km.shape[1]
    nh = hd // HEAD_DIM
    nb = 16 if db % 16 == 0 else 1
    return pl.pallas_call(
        functools.partial(_sel_s_body, nh=nh),
        grid=(db // nb,),
        in_specs=[pl.BlockSpec((nb, ts, hd), lambda b: (b, 0, 0)),
                  pl.BlockSpec((nb, nc, hd), lambda b: (b, 0, 0))],
        out_specs=pl.BlockSpec((nb, ts * MOBA_TOPK, nh), lambda b: (b, 0, 0)),
        out_shape=jax.ShapeDtypeStruct((db, ts * MOBA_TOPK, nh), I32),
        compiler_params=_cparams(("arbitrary",)),
        name="sel_s",
    )(q, km)


def _attn_s_body(idx_ref, pt_ref, qt_ref, knt_ref, vnt_ref, kc_ref, vc_ref, o_ref,
                 kbuf, vbuf, sem, *, nh, ts, n_pages, ppb):
    b = pl.program_id(0)
    nb = pl.num_programs(0)
    nsl = nh * ts * MOBA_TOPK
    ps = kc_ref.shape[3]

    def copies(bb, slot, s):
        h = s // (ts * MOBA_TOPK)
        blk = idx_ref[bb * nsl + s]
        out = []
        for pg in range(ppb):
            page = pt_ref[bb * n_pages + blk * ppb + pg]
            out.append(pltpu.make_async_copy(
                kc_ref.at[page, h], kbuf.at[slot, s, :, pl.ds(pg * ps, ps)], sem.at[slot]))
            out.append(pltpu.make_async_copy(
                vc_ref.at[page, h], vbuf.at[slot, s, :, pl.ds(pg * ps, ps)], sem.at[slot]))
        return out

    def start_all(bb, slot):
        def f(s, c):
            for cp in copies(bb, slot, s):
                cp.start()
            return c
        lax.fori_loop(0, nsl, f, 0)

    def wait_all(bb, slot):
        def f(s, c):
            for cp in copies(bb, slot, s):
                cp.wait()
            return c
        lax.fori_loop(0, nsl, f, 0)

    @pl.when(b == 0)
    def _():
        start_all(0, 0)

    @pl.when(b + 1 < nb)
    def _():
        start_all(b + 1, (b + 1) % 2)

    slot = b % 2
    wait_all(b, slot)

    scale = HEAD_DIM ** -0.5
    tcol = lax.broadcasted_iota(I32, (1, ts), 1)

    def head(h, c):
        hs = pl.ds(pl.multiple_of(h * HEAD_DIM, HEAD_DIM), HEAD_DIM)
        qh = qt_ref[0, hs, :] * scale
        knh = knt_ref[0, hs, :]
        vnh = vnt_ref[0, hs, :]
        cols = []
        for t in range(ts):
            qc = qh[:, t:t + 1]
            s_new = jnp.sum(qc * knh, axis=0, keepdims=True)
            s_new = jnp.where(tcol <= t, s_new, -jnp.inf)
            sls = [(h * ts + t) * MOBA_TOPK + j for j in range(MOBA_TOPK)]
            s_sel = [jnp.sum(qc * kbuf[slot, sl], axis=0, keepdims=True) for sl in sls]
            s_max = s_sel[0]
            for sj in s_sel[1:]:
                s_max = jnp.maximum(s_max, sj)
            m = jnp.maximum(jnp.max(s_new, axis=1, keepdims=True), jnp.max(s_max, axis=1, keepdims=True))
            p_new = jnp.exp(s_new - m)
            ps = [jnp.exp(sj - m) for sj in s_sel]
            p_sum = ps[0]
            pv = ps[0] * vbuf[slot, sls[0]]
            for pj, sl in zip(ps[1:], sls[1:]):
                p_sum = p_sum + pj
                pv = pv + pj * vbuf[slot, sl]
            l = jnp.sum(p_new, axis=1, keepdims=True) + jnp.sum(p_sum, axis=1, keepdims=True)
            acc = jnp.sum(p_new * vnh, axis=1, keepdims=True) + jnp.sum(pv, axis=1, keepdims=True)
            cols.append(acc / l)
        o_ref[0, hs, :] = jnp.concatenate(cols, axis=1)
        return c

    lax.fori_loop(0, nh, head, 0)


def _attn_s(idx_flat, pt_flat, qt, knt, vnt, kc4, vc4, *, n_pages):
    db, hd, ts = qt.shape
    _, nh, dh, ps = kc4.shape
    ppb = MOBA_BLOCK // ps
    nsl = nh * ts * MOBA_TOPK
    vec = pl.BlockSpec((1, hd, ts), lambda b, idx, pt: (b, 0, 0))
    return pl.pallas_call(
        functools.partial(_attn_s_body, nh=nh, ts=ts, n_pages=n_pages, ppb=ppb),
        grid_spec=pltpu.PrefetchScalarGridSpec(
            num_scalar_prefetch=2,
            grid=(db,),
            in_specs=[vec, vec, vec, pl.BlockSpec(memory_space=pl.ANY), pl.BlockSpec(memory_space=pl.ANY)],
            out_specs=vec,
            scratch_shapes=[pltpu.VMEM((2, nsl, dh, MOBA_BLOCK), F32),
                            pltpu.VMEM((2, nsl, dh, MOBA_BLOCK), F32),
                            pltpu.SemaphoreType.DMA((2,))],
        ),
        out_shape=jax.ShapeDtypeStruct((db, hd, ts), F32),
        compiler_params=_cparams(("arbitrary",)),
        name="attn_s",
    )(idx_flat, pt_flat, qt, knt, vnt, kc4, vc4)


def _post_body(x_ref, sh1_ref, sc1_ref, gt1_ref, sh2_ref, sc2_ref, oa_ref, ob_ref,
               g1_ref, g2_ref, wm_ref, bm_ref, wpa_ref, wpb_ref, wo_ref, wr_ref, br_ref,
               x1_ref, tok_ref, gate_ref, cnt_ref):
    x = x_ref[...]
    d = x.shape[1]
    h = _norm_mod(x, g1_ref[...], sh1_ref[0], sc1_ref[0])
    gates = _sigmoid(_dot(h, wm_ref[...]) + bm_ref[...])
    merged = gates[:, :d] * _dot(oa_ref[...], wpa_ref[...]) + gates[:, d:] * _dot(ob_ref[...], wpb_ref[...])
    x1 = x + gt1_ref[0] * _dot(merged, wo_ref[...])
    x1_ref[...] = x1
    h2 = _norm_mod(x1, g2_ref[...], sh2_ref[0], sc2_ref[0])
    tok_ref[...] = h2
    logits = _dot3(h2, wr_ref[...]) + br_ref[...]
    ne = logits.shape[1]
    lane = lax.broadcasted_iota(I32, logits.shape, 1).astype(F32)
    rem = logits
    vals, hots = [], []
    for _ in range(TOPK_E):
        m = jnp.max(rem, axis=1, keepdims=True)
        idx = jnp.min(jnp.where(rem == m, lane, float(ne)), axis=1, keepdims=True)
        hot = lane == idx
        vals.append(m)
        hots.append(hot)
        rem = jnp.where(hot, -jnp.inf, rem)
    es = [jnp.exp(v - vals[0]) for v in vals]
    den = es[0]
    for e in es[1:]:
        den = den + e
    gate = jnp.zeros_like(logits)
    sel = jnp.zeros_like(logits)
    for e, hot in zip(es, hots):
        gate = gate + jnp.where(hot, e / den, 0.0)
        sel = sel + jnp.where(hot, 1.0, 0.0)
    gate_ref[...] = jnp.concatenate([gate, sel], axis=1)
    cnt_ref[0] = jnp.concatenate(
        [jnp.sum(sel[r * ROW_TILE:(r + 1) * ROW_TILE], axis=0, keepdims=True) for r in range(cnt_ref.shape[1])],
        axis=0)


def _post(x2d, mods, oa, ob, g1, g2, wm, bm, wpa, wpb, wo, wr, br, *, nb, t):
    n, d = x2d.shape
    tm = min(TOK_TILE, t)
    nt = t // tm
    ne = wr.shape[1]
    mrows = mods[0].shape[1]
    if mrows == 1:
        mod_spec = pl.BlockSpec((1, 1, d), lambda b, j: (b, 0, 0))
    else:
        mod_spec = pl.BlockSpec((1, tm, d), lambda b, j: (b, j, 0))
    full = lambda a: pl.BlockSpec(a.shape, lambda b, j: (0,) * a.ndim)
    tok = lambda w: pl.BlockSpec((tm, w), lambda b, j: (b * nt + j, 0))
    ws = (g1, g2, wm, bm, wpa, wpb, wo, wr, br)
    return pl.pallas_call(
        _post_body,
        grid=(nb, nt),
        in_specs=[tok(d)] + [mod_spec] * 5 + [tok(oa.shape[1]), tok(ob.shape[1])] + [full(w) for w in ws],
        out_specs=[tok(d), tok(d), tok(2 * ne),
                   pl.BlockSpec((1, tm // ROW_TILE, ne), lambda b, j: (b * nt + j, 0, 0))],
        out_shape=[jax.ShapeDtypeStruct((n, d), F32),
                   jax.ShapeDtypeStruct((n, d), F32),
                   jax.ShapeDtypeStruct((n, 2 * ne), F32),
                   jax.ShapeDtypeStruct((nb * nt, tm // ROW_TILE, ne), F32)],
        compiler_params=_cparams(("arbitrary", "arbitrary")),
        name="post",
    )(x2d, *mods, oa, ob, *ws)


def _route_body(gs_ref, base_ref, pos_ref, g4_ref):
    gs = gs_ref[...]
    ne = gs.shape[1] // 2
    gate, sel = gs[:, :ne], gs[:, ne:]
    tm = gs.shape[0]
    r = lax.broadcasted_iota(I32, (tm, tm), 0)
    c = lax.broadcasted_iota(I32, (tm, tm), 1)
    tri = jnp.where(c < r, 1.0, 0.0).astype(BF16)
    rank = jnp.dot(tri, sel.astype(BF16), preferred_element_type=F32)
    posm = base_ref[0] + rank
    lane = lax.broadcasted_iota(I32, sel.shape, 1).astype(F32)
    rem = sel > 0.0
    ps, gv = [], []
    for _ in range(TOPK_E):
        idx = jnp.min(jnp.where(rem, lane, float(ne)), axis=1, keepdims=True)
        hot = lane == idx
        ps.append(jnp.sum(jnp.where(hot, posm, 0.0), axis=1, keepdims=True))
        gv.append(jnp.sum(jnp.where(hot, gate, 0.0), axis=1, keepdims=True))
        rem = rem & jnp.logical_not(hot)
    pos_ref[...] = jnp.concatenate(ps, axis=1).astype(I32)
    g4_ref[...] = jnp.concatenate(gv, axis=1)


def _route(gs, base):
    n, ne2 = gs.shape
    tm = ROW_TILE
    return pl.pallas_call(
        _route_body,
        grid=(n // tm,),
        in_specs=[pl.BlockSpec((tm, ne2), lambda j: (j, 0)),
                  pl.BlockSpec((1, 1, ne2 // 2), lambda j: (j, 0, 0))],
        out_specs=[pl.BlockSpec((tm, TOPK_E), lambda j: (j, 0)),
                   pl.BlockSpec((tm, TOPK_E), lambda j: (j, 0))],
        out_shape=[jax.ShapeDtypeStruct((n, TOPK_E), I32),
                   jax.ShapeDtypeStruct((n, TOPK_E), F32)],
        compiler_params=_cparams(("arbitrary",)),
        name="route",
    )(gs, base)


def _segment_copies(ls_ref, gb_ref, nc_ref, tile, n_exp, make, act):
    per_big = BIG_CHUNK // CHUNK

    def per_expert(e, c):
        i = tile * n_exp + e
        ls = ls_ref[i]
        gb = gb_ref[i]
        nbig = nc_ref[i] // per_big

        def chunk(rows, base):
            def f(k, c2):
                off = base + k * rows
                cp = make(pl.multiple_of(ls + off, CHUNK), pl.multiple_of(gb + off, CHUNK), rows)
                cp.start() if act == "start" else cp.wait()
                return c2
            return f

        lax.fori_loop(0, nbig, chunk(BIG_CHUNK, 0), 0)
        lax.fori_loop(0, nc_ref[i] - nbig * per_big, chunk(CHUNK, nbig * BIG_CHUNK), 0)
        return c

    lax.fori_loop(0, n_exp, per_expert, 0)


def _dispatch_body(ls_ref, gb_ref, nc_ref, ts_ref, tn_ref, lr_ref, tok_ref, *rest, tile0, n_exp, first):
    if first:
        xs_ref, loc, zrow, sem, zsem = rest
    else:
        _, xs_ref, loc, zrow, sem, zsem = rest
    j = pl.program_id(0)
    nj = pl.num_programs(0)
    slot = j % 2
    nloc, tm = loc.shape[1], tok_ref.shape[0]

    if first:
        @pl.when(j == 0)
        def _():
            zrow[...] = jnp.zeros_like(zrow)

            def tails(act):
                def per_expert(e, c):
                    def per_chunk(k, c2):
                        cp = pltpu.make_async_copy(
                            zrow, xs_ref.at[pl.ds(pl.multiple_of(ts_ref[e] + k * CHUNK, CHUNK), CHUNK)], zsem)
                        cp.start() if act == "start" else cp.wait()
                        return c2
                    lax.fori_loop(0, tn_ref[e], per_chunk, 0)
                    return c
                lax.fori_loop(0, n_exp, per_expert, 0)

            tails("start")
            tails("wait")

    r = lax.broadcasted_iota(I32, (nloc, tm), 0)
    perm = jnp.zeros((nloc, tm), F32)
    for k in range(TOPK_E):
        perm = perm + jnp.where(r == lr_ref[0, k:k + 1, :], 1.0, 0.0)
    loc[slot] = jnp.dot(perm.astype(BF16), tok_ref[...].astype(BF16), preferred_element_type=F32)

    def make(slot_):
        return lambda ls, gb, rows: pltpu.make_async_copy(
            loc.at[slot_, pl.ds(ls, rows)], xs_ref.at[pl.ds(gb, rows)], sem.at[slot_])

    _segment_copies(ls_ref, gb_ref, nc_ref, tile0 + j, n_exp, make(slot), "start")

    @pl.when(j > 0)
    def _():
        _segment_copies(ls_ref, gb_ref, nc_ref, tile0 + j - 1, n_exp, make(1 - slot), "wait")

    @pl.when(j == nj - 1)
    def _():
        _segment_copies(ls_ref, gb_ref, nc_ref, tile0 + j, n_exp, make(slot), "wait")


def _dispatch(seg, tails, lr_rows, tok, xs, *, tile0, rows):
    n, d = tok.shape
    tm = ROW_TILE
    n_exp = tails[0].shape[0]
    first = xs is None
    nloc = TOPK_E * tm + n_exp * CHUNK
    in_specs = [pl.BlockSpec((1, TOPK_E, tm), lambda j, *_: (tile0 + j, 0, 0)),
                pl.BlockSpec((tm, d), lambda j, *_: (j, 0))]
    args = [lr_rows, tok]
    if not first:
        in_specs.append(pl.BlockSpec(memory_space=pl.ANY))
        args.append(xs)
    return pl.pallas_call(
        functools.partial(_dispatch_body, tile0=tile0, n_exp=n_exp, first=first),
        grid_spec=pltpu.PrefetchScalarGridSpec(
            num_scalar_prefetch=5,
            grid=(n // tm,),
            in_specs=in_specs,
            out_specs=pl.BlockSpec(memory_space=pl.ANY),
            scratch_shapes=[pltpu.VMEM((2, nloc, d), F32), pltpu.VMEM((CHUNK, d), F32),
                            pltpu.SemaphoreType.DMA((2,)), pltpu.SemaphoreType.DMA(())],
        ),
        out_shape=jax.ShapeDtypeStruct((rows, d), F32),
        input_output_aliases={} if first else {7: 0},
        compiler_params=_cparams(("arbitrary",)),
        name="dispatch",
    )(*seg, *tails, *args)


def _moe_body(te_ref, nt_ref, x_ref, wu_ref, bg_ref, bl_ref, wd_ref, bd_ref, o_ref, wg_s, wl_s, wd_s):
    g = pl.program_id(0)

    @pl.when((g == 0) | (te_ref[g] != te_ref[jnp.maximum(g - 1, 0)]))
    def _():
        w2 = 2 * LANES
        r = lax.broadcasted_iota(I32, (w2, w2), 0)
        c = lax.broadcasted_iota(I32, (w2, w2), 1)
        src = jnp.where(c < LANES, 2 * c, 2 * (c - LANES) + 1)
        perm = jnp.where(r == src, 1.0, 0.0).astype(BF16)
        for k in range(wu_ref.shape[2] // w2):
            pair = jnp.dot(wu_ref[0, :, k * w2:(k + 1) * w2].astype(BF16), perm, preferred_element_type=F32)
            wg_s[:, k * LANES:(k + 1) * LANES] = pair[:, :LANES].astype(BF16)
            wl_s[:, k * LANES:(k + 1) * LANES] = pair[:, LANES:].astype(BF16)
        wd_s[...] = wd_ref[0].astype(BF16)

    @pl.when(g < nt_ref[0])
    def _():
        x = x_ref[...].astype(BF16)
        glu = jnp.minimum(jnp.dot(x, wg_s[...], preferred_element_type=F32) + bg_ref[0], SWIGLU_LIMIT)
        lin = jnp.clip(jnp.dot(x, wl_s[...], preferred_element_type=F32) + bl_ref[0], -SWIGLU_LIMIT, SWIGLU_LIMIT)
        act = glu * _sigmoid(SWIGLU_ALPHA * glu) * (lin + 1.0)
        o_ref[...] = jnp.dot(act.astype(BF16), wd_s[...], preferred_element_type=F32) + bd_ref[0]

    @pl.when(g >= nt_ref[0])
    def _():
        o_ref[...] = jnp.zeros_like(o_ref)


def _moe(tile_expert, n_tiles, xs, wu, bg, bl, wd, bd):
    r, d = xs.shape
    tm = MOE_TILE
    dff = wd.shape[1]
    row = lambda g, te, nt: (jnp.minimum(g, nt[0] - 1), 0)
    wsp = lambda a: pl.BlockSpec((1,) + a.shape[1:], lambda g, te, nt: (te[g], 0, 0))
    return pl.pallas_call(
        _moe_body,
        grid_spec=pltpu.PrefetchScalarGridSpec(
            num_scalar_prefetch=2,
            grid=(r // tm,),
            in_specs=[pl.BlockSpec((tm, d), row), wsp(wu), wsp(bg), wsp(bl), wsp(wd), wsp(bd)],
            out_specs=pl.BlockSpec((tm, d), lambda g, te, nt: (g, 0)),
            scratch_shapes=[pltpu.VMEM((d, dff), BF16), pltpu.VMEM((d, dff), BF16), pltpu.VMEM((dff, d), BF16)],
        ),
        out_shape=jax.ShapeDtypeStruct((r, d), F32),
        compiler_params=_cparams(("arbitrary",)),
        name="moe",
    )(tile_expert, n_tiles, xs, wu, bg, bl, wd, bd)


def _combine_body(ls_ref, gb_ref, nc_ref, lr_ref, g4_ref, x1_ref, gt2_ref, nf_ref, ys_ref, y_ref, loc, sem,
                  *, tile0, n_exp):
    j = pl.program_id(0)
    nj = pl.num_programs(0)
    slot = j % 2
    nloc, tm = loc.shape[1], x1_ref.shape[0]

    def make(slot_):
        return lambda ls, gb, rows: pltpu.make_async_copy(
            ys_ref.at[pl.ds(gb, rows)], loc.at[slot_, pl.ds(ls, rows)], sem.at[slot_])

    @pl.when(j == 0)
    def _():
        loc[...] = jnp.zeros_like(loc)
        _segment_copies(ls_ref, gb_ref, nc_ref, tile0, n_exp, make(0), "start")

    @pl.when(j + 1 < nj)
    def _():
        _segment_copies(ls_ref, gb_ref, nc_ref, tile0 + j + 1, n_exp, make(1 - slot), "start")

    _segment_copies(ls_ref, gb_ref, nc_ref, tile0 + j, n_exp, make(slot), "wait")

    lane = lax.broadcasted_iota(I32, (tm, nloc), 1)
    lr = lr_ref[...]
    g4 = g4_ref[...]
    w = jnp.zeros((tm, nloc), F32)
    for k in range(TOPK_E):
        w = w + jnp.where(lane == lr[:, k:k + 1], g4[:, k:k + 1], 0.0)
    wh, wl = _split(w)
    yb = loc[slot].astype(BF16)
    ff = jnp.dot(wh, yb, preferred_element_type=F32) + jnp.dot(wl, yb, preferred_element_type=F32)
    y_ref[...] = _rms(x1_ref[...] + gt2_ref[0] * ff, nf_ref[...])


def _combine(seg, lr, g4, x1, gt2, nf, ys, *, tile0, n_exp, t):
    n, d = x1.shape
    tm = min(ROW_TILE, t)
    nt = t // tm
    nloc = TOPK_E * tm + n_exp * CHUNK
    if gt2.shape[1] == 1:
        mod_spec = pl.BlockSpec((1, 1, d), lambda j, *_: (j // nt, 0, 0))
    else:
        mod_spec = pl.BlockSpec((1, tm, d), lambda j, *_: (j // nt, j % nt, 0))
    return pl.pallas_call(
        functools.partial(_combine_body, tile0=tile0, n_exp=n_exp),
        grid_spec=pltpu.PrefetchScalarGridSpec(
            num_scalar_prefetch=3,
            grid=(n // tm,),
            in_specs=[pl.BlockSpec((tm, TOPK_E), lambda j, *_: (tile0 + j, 0)),
                      pl.BlockSpec((tm, TOPK_E), lambda j, *_: (tile0 + j, 0)),
                      pl.BlockSpec((tm, d), lambda j, *_: (j, 0)),
                      mod_spec,
                      pl.BlockSpec((1, d), lambda j, *_: (0, 0)),
                      pl.BlockSpec(memory_space=pl.ANY)],
            out_specs=pl.BlockSpec((tm, d), lambda j, *_: (j, 0)),
            scratch_shapes=[pltpu.VMEM((2, nloc, d), F32), pltpu.SemaphoreType.DMA((2,))],
        ),
        out_shape=jax.ShapeDtypeStruct((n, d), F32),
        compiler_params=_cparams(("arbitrary",)),
        name="combine",
    )(*seg, lr, g4, x1, gt2, nf, ys)


def _block_diag(w):
    g, a, b = w.shape
    eye = jnp.eye(g, dtype=w.dtype)
    return (eye[:, None, :, None] * w[:, :, None, :]).reshape(g * a, g * b)


def _layer0(a):
    return a.reshape(a.shape[1:])

def kernel(x_prompt, x_sample, cache_k, cache_v, state_h, state_conv, page_table, c_prompt, c_sample,
           w_ada, b_ada, norm1, norm2, w_in, conv_w, conv_b, w_rg_a, b_rg_a, w_rg_i, b_rg_i, rg_lambda,
           w_proj_a, w_proj_b, w_merge, b_merge, w_out, w_router, b_router, w_up, b_up, w_down, b_down, norm_f):
    assert w_ada.shape[0] == 1, "single trunk layer"
    bp, t, d = x_prompt.shape
    db, ts, _ = x_sample.shape
    _, n_phys, ps, nh, _ = cache_k.shape
    hd = nh * HEAD_DIM
    da = hd
    dr = conv_w.shape[2]
    n_pages = page_table.shape[1]
    ne = w_router.shape[2]
    np_, ns_ = bp * t, db * ts
    nblk = t // MOBA_BLOCK
    assert t % TOK_TILE == 0 and ns_ == TOK_TILE and n_pages % 16 == 0 and (dr // 2) % w_rg_a.shape[2] == 0
    assert (n_pages * ps) % MOBA_BLOCK == 0 and (n_pages * ps) // MOBA_BLOCK >= MOBA_TOPK

    mod = _ada(jnp.concatenate([c_prompt, c_sample], axis=0), _layer0(w_ada), _layer0(b_ada))
    mod = mod.reshape(bp + db, N_MOD, d)
    mp = [mod[:bp, i][:, None, :] for i in range(N_MOD)]
    ms = [jnp.tile(mod[bp:, i], (ts, 1))[None] for i in range(N_MOD)]

    g1, g2, nf = norm1, norm2, norm_f[None]
    w_in0 = _layer0(w_in)
    wqkvt, wk, wug = w_in0[:, :3 * da].T, w_in0[:, da:2 * da], w_in0[:, 3 * da:]

    xp2 = x_prompt.reshape(np_, d)
    xs2 = jnp.swapaxes(x_sample, 0, 1).reshape(ns_, d)

    qt_p, kt_p, vt_p, u_p, g_p, kb_p, vtb_p, km_p = _proj(
        xp2, mp[0], mp[1], g1, wqkvt.astype(BF16), wug.astype(BF16), wk.astype(BF16), nb=bp, t=t)
    qt_s, kt_s, vt_s, u_s, g_s = _proj(xs2, ms[0], ms[1], g1, wqkvt, wug, nb=1, t=ns_)

    oa_p = _attn_p(qt_p, kb_p, vtb_p, km_p.reshape(bp, nblk, da), nb=bp, t=t)

    kc4 = jnp.transpose(cache_k.reshape(n_phys, ps, nh, HEAD_DIM), (0, 2, 3, 1))
    vc4 = jnp.transpose(cache_v.reshape(n_phys, ps, nh, HEAD_DIM), (0, 2, 3, 1))
    pt_flat = page_table.reshape(-1)
    km = _kmean_s(pt_flat, kc4.reshape(n_phys, hd, ps), db=db, n_pages=n_pages)
    km = jnp.transpose(km, (0, 1, 3, 2)).reshape(db, -1, hd)
    tmaj = lambda a: jnp.transpose(a.reshape(hd, ts, db), (2, 0, 1))
    qs3, kn3, vn3 = tmaj(qt_s), tmaj(kt_s), tmaj(vt_s)
    idx = _sel_s(jnp.transpose(qt_s.reshape(hd, ts, db), (2, 1, 0)), km)
    idx = jnp.swapaxes(idx, 1, 2)
    ot_s = _attn_s(idx.reshape(-1), pt_flat, qs3, kn3, vn3, kc4, vc4, n_pages=n_pages)
    oa_s = jnp.transpose(ot_s, (2, 0, 1)).reshape(ns_, hd)

    cw, cb = _layer0(conv_w), conv_b
    wa, wi = _block_diag(_layer0(w_rg_a)), _block_diag(_layer0(w_rg_i))
    ba, bi, lam = b_rg_a, b_rg_i, rg_lambda
    ob_p, hl_p, cn_p = _rg_p(u_p, g_p, cw, cb, wa, ba, wi, bi, lam, nb=bp, t=t)
    ob_s, hl_s, cn_s = _rg_s(u_s.reshape(ts, db, dr), g_s.reshape(ts, db, dr),
                             jnp.swapaxes(_layer0(state_conv), 0, 1), _layer0(state_h),
                             cw, cb, wa, ba, wi, bi, lam)

    wsm = (g1, g2, _layer0(w_merge).astype(BF16), b_merge, _layer0(w_proj_a).astype(BF16),
           _layer0(w_proj_b).astype(BF16), _layer0(w_out).astype(BF16), _layer0(w_router), b_router)
    x1_p, tok_p, gs_p, cnt_p = _post(xp2, mp[:5], oa_p, ob_p, *wsm, nb=bp, t=t)
    x1_s, tok_s, gs_s, cnt_s = _post(xs2, ms[:5], oa_s.astype(BF16), ob_s.reshape(ns_, dr).astype(BF16), *wsm,
                                     nb=1, t=ns_)

    gs = jnp.concatenate([gs_p, gs_s], axis=0)
    cnt = jnp.concatenate([cnt_p, cnt_s], axis=0).reshape(-1, ne).astype(I32)
    n_rt = cnt.shape[0]
    seg = (cnt + CHUNK - 1) // CHUNK * CHUNK
    lstart = jnp.cumsum(seg, axis=1) - seg
    tot = jnp.sum(seg, axis=0)
    tiles_e = (tot + MOE_TILE - 1) // MOE_TILE
    tile_end = jnp.cumsum(tiles_e)
    poff = (tile_end - tiles_e) * MOE_TILE
    gbase = poff[None, :] + jnp.cumsum(seg, axis=0) - seg
    seg_tabs = (lstart.reshape(-1), gbase.reshape(-1), (seg // CHUNK).reshape(-1))
    tail_tabs = (poff + tot, (tiles_e * MOE_TILE - tot) // CHUNK)
    n_all = np_ + ns_
    rows_max = n_all * TOPK_E + n_rt * ne * (CHUNK - 1) + ne * MOE_TILE
    g_max = -(-rows_max // MOE_TILE)
    n_tiles = tile_end[-1:].astype(I32)
    tile_id = jnp.minimum(jnp.arange(g_max, dtype=I32), n_tiles - 1)
    te = jnp.sum((tile_end[None, :] <= tile_id[:, None]).astype(I32), axis=1)
    lr, g4 = _route(gs, lstart.astype(F32)[:, None, :])
    lr_rows = jnp.swapaxes(lr.reshape(n_rt, ROW_TILE, TOPK_E), 1, 2)
    rt_p = np_ // ROW_TILE

    xs = _dispatch(seg_tabs, tail_tabs, lr_rows, tok_p, None, tile0=0, rows=g_max * MOE_TILE)
    xs = _dispatch(seg_tabs, tail_tabs, lr_rows, tok_s, xs, tile0=rt_p, rows=g_max * MOE_TILE)
    b_up0 = _layer0(b_up)
    ys = _moe(te, n_tiles, xs, _layer0(w_up), b_up0[:, None, 0::2], b_up0[:, None, 1::2],
              _layer0(w_down), _layer0(b_down)[:, None, :])
    y_p = _combine(seg_tabs, lr, g4, x1_p, mp[5], nf, ys, tile0=0, n_exp=ne, t=t)
    y_s = _combine(seg_tabs, lr, g4, x1_s, ms[5], nf, ys, tile0=rt_p, n_exp=ne, t=ns_)

    y_prompt = y_p.reshape(bp, t, d)
    y_sample = jnp.swapaxes(y_s.reshape(ts, db, d), 0, 1)
    to5 = lambda a: jnp.transpose(a.reshape(bp, nh, HEAD_DIM, t), (0, 3, 1, 2))[None]
    k_prompt, v_prompt = to5(kt_p), to5(vt_p)
    to5s = lambda a: jnp.transpose(a.reshape(nh, HEAD_DIM, ts, db), (3, 2, 0, 1))[None]
    k_sample, v_sample = to5s(kt_s), to5s(vt_s)
    h_prompt = hl_p.reshape(1, bp, dr)
    conv_prompt = cn_p[None]
    h_sample = hl_s[None]
    conv_sample = jnp.swapaxes(cn_s, 0, 1)[None]
    return (y_prompt, y_sample, k_prompt, v_prompt, h_prompt, conv_prompt,
            k_sample, v_sample, h_sample, conv_sample)
```

```python
import functools

import jax
import jax.numpy as jnp
from jax import lax
from jax.experimental import pallas as pl
from jax.experimental.pallas import tpu as pltpu

F32 = jnp.float32
BF16 = jnp.bfloat16
I32 = jnp.int32

HEAD_DIM = 64
MOBA_BLOCK = 256
MOBA_TOPK = 3
RG_C = 8.0
TOPK_E = 4
SWIGLU_ALPHA = 1.702
SWIGLU_LIMIT = 7.0
EPS = 1e-6
N_MOD = 6

LANES = 128
TOK_TILE = 512
ATTN_HEADS = 4
RG_TILE = 256
MOE_TILE = 512
ROW_TILE = 256
CHUNK = 8
BIG_CHUNK = 32
KMEAN_PAGES = 32
VMEM_LIMIT = 56 * 1024 * 1024


def _cparams(sem):
    return pltpu.CompilerParams(dimension_semantics=sem, vmem_limit_bytes=VMEM_LIMIT)


def _dot(a, b):
    return jnp.dot(a.astype(BF16), b.astype(BF16), preferred_element_type=F32)


def _dot_t(a, b):
    return lax.dot_general(a.astype(BF16), b.astype(BF16), (((1,), (1,)), ((), ())),
                           preferred_element_type=F32)


def _split(a):
    hi = a.astype(BF16)
    lo = (a - hi.astype(F32)).astype(BF16)
    return hi, lo


def _dot3(a, b):
    ah, al = _split(a)
    bh, bl = _split(b)
    return (jnp.dot(ah, bh, preferred_element_type=F32)
            + jnp.dot(al, bh, preferred_element_type=F32)
            + jnp.dot(ah, bl, preferred_element_type=F32))


def _dot3_t(a, b):
    ah, al = _split(a)
    bh, bl = _split(b)
    dn = (((1,), (1,)), ((), ()))
    return (lax.dot_general(ah, bh, dn, preferred_element_type=F32)
            + lax.dot_general(al, bh, dn, preferred_element_type=F32)
            + lax.dot_general(ah, bl, dn, preferred_element_type=F32))


def _sigmoid(x):
    return 1.0 / (1.0 + jnp.exp(-x))


def _rms(x, g):
    return x * lax.rsqrt(jnp.mean(x * x, axis=-1, keepdims=True) + EPS) * g


def _norm_mod(x, g, shift, scale):
    return _rms(x, g) * (1.0 + scale) + shift


def _gelu_tanh(x):
    return 0.5 * x * (1.0 + jnp.tanh(0.7978845608028654 * (x + 0.044715 * (x * x * x))))


def _ada_body(c_ref, w_ref, b_ref, o_ref):
    c = c_ref[...]
    o_ref[...] = _dot3(c * _sigmoid(c), w_ref[...]) + b_ref[...]


def _ada(c, w, b):
    n, d = c.shape
    e = w.shape[1]
    te = 1024
    return pl.pallas_call(
        _ada_body,
        grid=(e // te,),
        in_specs=[pl.BlockSpec((n, d), lambda j: (0, 0)),
                  pl.BlockSpec((d, te), lambda j: (0, j)),
                  pl.BlockSpec((1, te), lambda j: (0, j))],
        out_specs=pl.BlockSpec((n, te), lambda j: (0, j)),
        out_shape=jax.ShapeDtypeStruct((n, e), F32),
        compiler_params=_cparams(("arbitrary",)),
        name="ada",
    )(c, w, b.reshape(1, e))


def _proj_body(x_ref, sh_ref, sc_ref, g1_ref, wqkvt_ref, wug_ref, *rest, prompt):
    h = _norm_mod(x_ref[...], g1_ref[...], sh_ref[0], sc_ref[0])
    mm, mmt = (_dot, _dot_t) if prompt else (_dot3, _dot3_t)
    if prompt:
        wk_ref, qt_ref, kt_ref, vt_ref, u_ref, g_ref, kb_ref, vtb_ref, km_ref = rest
    else:
        qt_ref, kt_ref, vt_ref, u_ref, g_ref = rest
    da = qt_ref.shape[1]
    qkvt = mmt(wqkvt_ref[...], h)
    qt_ref[0] = qkvt[:da]
    kt_ref[0] = qkvt[da:2 * da]
    vt = qkvt[2 * da:]
    vt_ref[0] = vt
    ug = mm(h, wug_ref[...])
    dr = u_ref.shape[1]
    u_ref[...] = ug[:, :dr]
    g_ref[...] = ug[:, dr:]
    if prompt:
        k = mm(h, wk_ref[...])
        kb_ref[...] = k.astype(BF16)
        vtb_ref[0] = vt.astype(BF16)
        means = [jnp.sum(k[n * MOBA_BLOCK:(n + 1) * MOBA_BLOCK], axis=0, keepdims=True) * (1.0 / MOBA_BLOCK)
                 for n in range(k.shape[0] // MOBA_BLOCK)]
        km_ref[0] = jnp.concatenate(means, axis=0)


def _proj(x2d, shift, scale, g1, wqkvt, wug, wk=None, *, nb, t):
    prompt = wk is not None
    n, d = x2d.shape
    tm = min(TOK_TILE, t)
    nt = t // tm
    da = wqkvt.shape[0] // 3
    dr = wug.shape[1] // 2
    mrows = shift.shape[1]
    if mrows == 1:
        mod_spec = pl.BlockSpec((1, 1, d), lambda b, j: (b, 0, 0))
    else:
        mod_spec = pl.BlockSpec((1, tm, d), lambda b, j: (b, j, 0))
    full = lambda a: pl.BlockSpec(a.shape, lambda b, j: (0,) * a.ndim)
    tok = lambda w: pl.BlockSpec((tm, w), lambda b, j: (b * nt + j, 0))
    featmaj = pl.BlockSpec((1, da, tm), lambda b, j: (b, 0, j))
    in_specs = [tok(d), mod_spec, mod_spec, full(g1), full(wqkvt), full(wug)]
    out_specs = [featmaj, featmaj, featmaj, tok(dr), tok(dr)]
    out_shape = [jax.ShapeDtypeStruct((nb, da, t), F32)] * 3 + [jax.ShapeDtypeStruct((n, dr), F32)] * 2
    args = [x2d, shift, scale, g1, wqkvt, wug]
    if prompt:
        bpt = tm // MOBA_BLOCK
        in_specs.append(full(wk))
        args.append(wk)
        out_specs += [tok(da), featmaj,
                      pl.BlockSpec((1, bpt, da), lambda b, j: (b * nt + j, 0, 0))]
        out_shape += [jax.ShapeDtypeStruct((n, da), BF16),
                      jax.ShapeDtypeStruct((nb, da, t), BF16),
                      jax.ShapeDtypeStruct((nb * nt, bpt, da), F32)]
    return pl.pallas_call(
        functools.partial(_proj_body, prompt=prompt),
        grid=(nb, nt),
        in_specs=in_specs,
        out_specs=out_specs,
        out_shape=out_shape,
        compiler_params=_cparams(("arbitrary", "arbitrary")),
        name="proj_p" if prompt else "proj_s",
    )(*args)


def _attn_p_body(qt_ref, kb_ref, kd_ref, vtb_ref, vd_ref, km_ref, o_ref, sel_ref, *, nblk):
    i = pl.program_id(2)
    bs = MOBA_BLOCK
    krow = lax.broadcasted_iota(I32, (bs, bs), 0)
    qcol = lax.broadcasted_iota(I32, (bs, bs), 1)
    blk = lax.broadcasted_iota(I32, (nblk, bs), 0)
    valid = blk < i
    nhh = qt_ref.shape[1] // HEAD_DIM
    heads = [slice(hh * HEAD_DIM, (hh + 1) * HEAD_DIM) for hh in range(nhh)]

    def run(npast):
        outs = []
        for hh, hs in enumerate(heads):
            qt = qt_ref[0, hs, :]
            sv = jnp.where(valid, _dot3(km_ref[0, :, hs], qt), -jnp.inf)
            cnt = jnp.zeros_like(sv)
            for m in range(nblk - 1):
                sm = sv[m:m + 1, :]
                beats = (sm > sv) | ((sm == sv) & (m < blk))
                cnt = cnt + jnp.where(beats, 1.0, 0.0)
            sel_ref[hh] = jnp.where(valid & (cnt < MOBA_TOPK), 1.0, 0.0)

            qb = (qt * (HEAD_DIM ** -0.5)).astype(BF16)
            sd = jnp.dot(kd_ref[:, hs], qb, preferred_element_type=F32)
            sd = jnp.where(krow <= qcol, sd, -jnp.inf)
            sps = []
            if npast:
                sp = jnp.dot(kb_ref[0:npast * bs, hs], qb, preferred_element_type=F32)
                sps = [jnp.where(sel_ref[hh, j:j + 1, :] > 0.0, sp[j * bs:(j + 1) * bs], -jnp.inf)
                       for j in range(npast)]
            mx = jnp.max(sd, axis=0, keepdims=True)
            for sj in sps:
                mx = jnp.maximum(mx, jnp.max(sj, axis=0, keepdims=True))
            pd = jnp.exp(sd - mx)
            den = jnp.sum(pd, axis=0, keepdims=True)
            pps = []
            for sj in sps:
                pj = jnp.exp(sj - mx)
                den = den + jnp.sum(pj, axis=0, keepdims=True)
                pps.append(pj.astype(BF16))
            acc = jnp.dot(vd_ref[0, hs, :], pd.astype(BF16), preferred_element_type=F32)
            if npast:
                acc = acc + jnp.dot(vtb_ref[0, hs, 0:npast * bs], jnp.concatenate(pps, axis=0),
                                    preferred_element_type=F32)
            outs.append(acc / den)
        o_ref[...] = jnp.concatenate(outs, axis=0).T.astype(o_ref.dtype)

    for npast in range(nblk):
        pl.when(i == npast)(functools.partial(run, npast))


def _attn_p(qt, kb, vtb, km, *, nb, t):
    da = qt.shape[1]
    nblk = t // MOBA_BLOCK
    hw = ATTN_HEADS * HEAD_DIM
    nhp = da // hw
    return pl.pallas_call(
        functools.partial(_attn_p_body, nblk=nblk),
        grid=(nb, nhp, nblk),
        in_specs=[pl.BlockSpec((1, hw, MOBA_BLOCK), lambda b, h, i: (b, h, i)),
                  pl.BlockSpec((t, hw), lambda b, h, i: (b, h)),
                  pl.BlockSpec((MOBA_BLOCK, hw), lambda b, h, i: (b * nblk + i, h)),
                  pl.BlockSpec((1, hw, t), lambda b, h, i: (b, h, 0)),
                  pl.BlockSpec((1, hw, MOBA_BLOCK), lambda b, h, i: (b, h, i)),
                  pl.BlockSpec((1, nblk, hw), lambda b, h, i: (b, 0, h))],
        out_specs=pl.BlockSpec((MOBA_BLOCK, hw), lambda b, h, i: (b * nblk + i, h)),
        out_shape=jax.ShapeDtypeStruct((nb * t, da), BF16),
        scratch_shapes=[pltpu.VMEM((ATTN_HEADS, nblk, MOBA_BLOCK), F32)],
        compiler_params=_cparams(("arbitrary", "arbitrary", "arbitrary")),
        name="attn_p",
    )(qt, kb, kb, vtb, vtb, km)


def _rg_gates(uc, wa_ref, ba_ref, wi_ref, bi_ref, lam_ref):
    r = _sigmoid(_dot3(uc, wa_ref[...]) + ba_ref[...])
    ig = _sigmoid(_dot3(uc, wi_ref[...]) + bi_ref[...])
    nl = -lam_ref[...]
    softplus = jnp.maximum(nl, 0.0) + jnp.log(1.0 + jnp.exp(-jnp.abs(nl)))
    log_a = -RG_C * r * softplus
    a = jnp.exp(log_a)
    bx = jnp.sqrt(1.0 - a * a) * (ig * uc)
    return a, bx


def _rg_p_body(u_ref, g_ref, cw_ref, cb_ref, wa_ref, ba_ref, wi_ref, bi_ref, lam_ref,
               o_ref, hl_ref, cn_ref, ubuf, hc):
    j = pl.program_id(2)
    tt = u_ref.shape[0]
    pad = 8

    @pl.when(j == 0)
    def _():
        ubuf[0:pad] = jnp.zeros((pad, ubuf.shape[1]), F32)
        hc[...] = jnp.zeros_like(hc)

    u = u_ref[...]
    ubuf[pad:pad + tt] = u
    cw = cw_ref[...]
    kw = cw.shape[0]
    uc = cb_ref[...] + u * cw[kw - 1:kw]
    for jj in range(kw - 1):
        off = pad - (kw - 1) + jj
        uc = uc + ubuf[off:off + tt] * cw[jj:jj + 1]
    a, bx = _rg_gates(uc, wa_ref, ba_ref, wi_ref, bi_ref, lam_ref)

    row = lax.broadcasted_iota(I32, a.shape, 0)
    d = 1
    while d < tt:
        keep = row >= d
        a_sh = jnp.where(keep, pltpu.roll(a, d, 0), 1.0)
        b_sh = jnp.where(keep, pltpu.roll(bx, d, 0), 0.0)
        bx = a * b_sh + bx
        a = a * a_sh
        d *= 2
    h = bx + a * hc[...]
    hc[...] = h[tt - 1:tt]
    hl_ref[0] = h[tt - 1:tt]
    cn_ref[0] = u[tt - (kw - 1):tt]
    ubuf[0:pad] = u[tt - pad:tt]
    o_ref[...] = (h * _gelu_tanh(g_ref[...])).astype(o_ref.dtype)


def _rg_p(u, g, cw, cb, wa, ba, wi, bi, lam, *, nb, t):
    n, dr = u.shape
    tt = min(RG_TILE, t)
    nt = t // tt
    kw = cw.shape[0]
    cwid = dr // 2
    vec = lambda a: pl.BlockSpec((a.shape[0], cwid), lambda b, c, j: (0, c))
    diag = pl.BlockSpec((cwid, cwid), lambda b, c, j: (c, c))
    tok = pl.BlockSpec((tt, cwid), lambda b, c, j: (b * nt + j, c))
    return pl.pallas_call(
        _rg_p_body,
        grid=(nb, 2, nt),
        in_specs=[tok, tok, vec(cw), vec(cb), diag, vec(ba), diag, vec(bi), vec(lam)],
        out_specs=[tok,
                   pl.BlockSpec((1, 1, cwid), lambda b, c, j: (b, 0, c)),
                   pl.BlockSpec((1, kw - 1, cwid), lambda b, c, j: (b, 0, c))],
        out_shape=[jax.ShapeDtypeStruct((n, dr), BF16),
                   jax.ShapeDtypeStruct((nb, 1, dr), F32),
                   jax.ShapeDtypeStruct((nb, kw - 1, dr), F32)],
        scratch_shapes=[pltpu.VMEM((8 + tt, cwid), F32), pltpu.VMEM((1, cwid), F32)],
        compiler_params=_cparams(("arbitrary", "arbitrary", "arbitrary")),
        name="rg_p",
    )(u, g, cw, cb, wa, ba, wi, bi, lam)


def _rg_s_body(u_ref, g_ref, cs_ref, h0_ref, cw_ref, cb_ref, wa_ref, ba_ref, wi_ref, bi_ref, lam_ref,
               o_ref, hl_ref, cn_ref):
    ts = u_ref.shape[0]
    cw = cw_ref[...]
    kw = cw.shape[0]
    full = [cs_ref[jj] for jj in range(kw - 1)] + [u_ref[tt] for tt in range(ts)]
    h = h0_ref[...]
    for tt in range(ts):
        uc = cb_ref[...]
        for jj in range(kw):
            uc = uc + full[tt + jj] * cw[jj:jj + 1]
        a, bx = _rg_gates(uc, wa_ref, ba_ref, wi_ref, bi_ref, lam_ref)
        h = a * h + bx
        o_ref[tt] = h * _gelu_tanh(g_ref[tt])
    hl_ref[...] = h
    for jj in range(kw - 1):
        cn_ref[jj] = full[ts + jj]


def _rg_s(u, g, cs, h0, cw, cb, wa, ba, wi, bi, lam):
    ts, db, dr = u.shape
    kw = cw.shape[0]
    return pl.pallas_call(
        _rg_s_body,
        out_shape=[jax.ShapeDtypeStruct((ts, db, dr), F32),
                   jax.ShapeDtypeStruct((db, dr), F32),
                   jax.ShapeDtypeStruct((kw - 1, db, dr), F32)],
        compiler_params=pltpu.CompilerParams(vmem_limit_bytes=VMEM_LIMIT),
        name="rg_s",
    )(u, g, cs, h0, cw, cb, wa, ba, wi, bi, lam)


def _kmean_s_body(pt_ref, *refs, npg):
    del pt_ref
    pages, o_ref = refs[:npg], refs[npg]
    cols = []
    for n in range(npg // 2):
        blk = pages[2 * n][0] + pages[2 * n + 1][0]
        cols.append(jnp.sum(blk, axis=1, keepdims=True) * (1.0 / MOBA_BLOCK))
    o_ref[0, 0] = jnp.concatenate(cols, axis=1)


def _kmean_s(pt_flat, kt3, *, db, n_pages):
    _, hd, ps = kt3.shape
    npg = KMEAN_PAGES if n_pages % KMEAN_PAGES == 0 else KMEAN_PAGES // 2
    nchunk = n_pages // npg
    specs = [pl.BlockSpec((1, hd, ps), functools.partial(
        lambda b, c, pt, p: (pt[b * n_pages + c * npg + p], 0, 0), p=p)) for p in range(npg)]
    return pl.pallas_call(
        functools.partial(_kmean_s_body, npg=npg),
        grid_spec=pltpu.PrefetchScalarGridSpec(
            num_scalar_prefetch=1,
            grid=(db, nchunk),
            in_specs=specs,
            out_specs=pl.BlockSpec((1, 1, hd, npg // 2), lambda b, c, pt: (b, c, 0, 0)),
        ),
        out_shape=jax.ShapeDtypeStruct((db, nchunk, hd, npg // 2), F32),
        compiler_params=_cparams(("arbitrary", "arbitrary")),
        name="kmean_s",
    )(pt_flat, *([kt3] * npg))


def _sel_s_body(q_ref, km_ref, o_ref, *, nh):
    nb, ts, hd = q_ref.shape
    nc = km_ref.shape[1]
    cand = lax.broadcasted_iota(I32, (nc, nh), 0).astype(F32)
    grp = lax.broadcasted_iota(I32, (hd, nh), 0) // HEAD_DIM == lax.broadcasted_iota(I32, (hd, nh), 1)
    head_sum = jnp.where(grp, 1.0, 0.0)

    def one(b, c):
        km = km_ref[b]
        outs = []
        for t in range(ts):
            s = _dot3(q_ref[b, t:t + 1, :] * km, head_sum)
            for _ in range(MOBA_TOPK):
                m = jnp.max(s, axis=0, keepdims=True)
                idx = jnp.min(jnp.where(s == m, cand, float(nc)), axis=0, keepdims=True)
                outs.append(idx)
                s = jnp.where(cand == idx, -jnp.inf, s)
        o_ref[b] = jnp.concatenate(outs, axis=0).astype(I32)
        return c

    lax.fori_loop(0, nb, one, 0)


def _sel_s(q, km):
    db, ts, hd = q.shape
    nc = km.shape[1]
    nh = hd // HEAD_DIM
    nb = 16 if db % 16 == 0 else 1
    return pl.pallas_call(
        functools.partial(_sel_s_body, nh=nh),
        grid=(db // nb,),
        in_specs=[pl.BlockSpec((nb, ts, hd), lambda b: (b, 0, 0)),
                  pl.BlockSpec((nb, nc, hd), lambda b: (b, 0, 0))],
        out_specs=pl.BlockSpec((nb, ts * MOBA_TOPK, nh), lambda b: (b, 0, 0)),
        out_shape=jax.ShapeDtypeStruct((db, ts * MOBA_TOPK, nh), I32),
        compiler_params=_cparams(("arbitrary",)),
        name="sel_s",
    )(q, km)


def _attn_s_body(idx_ref, pt_ref, qt_ref, knt_ref, vnt_ref, kc_ref, vc_ref, o_ref,
                 kbuf, vbuf, sem, *, nh, ts, n_pages, ppb):
    b = pl.program_id(0)
    nb = pl.num_programs(0)
    nsl = nh * ts * MOBA_TOPK
    ps = kc_ref.shape[3]

    def copies(bb, slot, s):
        h = s // (ts * MOBA_TOPK)
        blk = idx_ref[bb * nsl + s]
        out = []
        for pg in range(ppb):
            page = pt_ref[bb * n_pages + blk * ppb + pg]
            out.append(pltpu.make_async_copy(
                kc_ref.at[page, h], kbuf.at[slot, s, :, pl.ds(pg * ps, ps)], sem.at[slot]))
            out.append(pltpu.make_async_copy(
                vc_ref.at[page, h], vbuf.at[slot, s, :, pl.ds(pg * ps, ps)], sem.at[slot]))
        return out

    def start_all(bb, slot):
        def f(s, c):
            for n, cp in enumerate(copies(bb, slot, s)):
                cp.start(priority=n % 2)
            return c
        lax.fori_loop(0, nsl, f, 0)

    def wait_all(bb, slot):
        def f(s, c):
            for cp in copies(bb, slot, s):
                cp.wait()
            return c
        lax.fori_loop(0, nsl, f, 0)

    @pl.when(b == 0)
    def _():
        start_all(0, 0)

    @pl.when(b + 1 < nb)
    def _():
        start_all(b + 1, (b + 1) % 2)

    slot = b % 2
    wait_all(b, slot)

    scale = HEAD_DIM ** -0.5
    tcol = lax.broadcasted_iota(I32, (1, ts), 1)

    def head(h, c):
        hs = pl.ds(pl.multiple_of(h * HEAD_DIM, HEAD_DIM), HEAD_DIM)
        qh = qt_ref[0, hs, :] * scale
        knh = knt_ref[0, hs, :]
        vnh = vnt_ref[0, hs, :]
        cols = []
        for t in range(ts):
            qc = qh[:, t:t + 1]
            s_new = jnp.sum(qc * knh, axis=0, keepdims=True)
            s_new = jnp.where(tcol <= t, s_new, -jnp.inf)
            sls = [(h * ts + t) * MOBA_TOPK + j for j in range(MOBA_TOPK)]
            s_sel = [jnp.sum(qc * kbuf[slot, sl], axis=0, keepdims=True) for sl in sls]
            s_max = s_sel[0]
            for sj in s_sel[1:]:
                s_max = jnp.maximum(s_max, sj)
            m = jnp.maximum(jnp.max(s_new, axis=1, keepdims=True), jnp.max(s_max, axis=1, keepdims=True))
            p_new = jnp.exp(s_new - m)
            ps = [jnp.exp(sj - m) for sj in s_sel]
            p_sum = ps[0]
            pv = ps[0] * vbuf[slot, sls[0]]
            for pj, sl in zip(ps[1:], sls[1:]):
                p_sum = p_sum + pj
                pv = pv + pj * vbuf[slot, sl]
            l = jnp.sum(p_new, axis=1, keepdims=True) + jnp.sum(p_sum, axis=1, keepdims=True)
            acc = jnp.sum(p_new * vnh, axis=1, keepdims=True) + jnp.sum(pv, axis=1, keepdims=True)
            cols.append(acc / l)
        o_ref[0, hs, :] = jnp.concatenate(cols, axis=1)
        return c

    lax.fori_loop(0, nh, head, 0)


def _attn_s(idx_flat, pt_flat, qt, knt, vnt, kc4, vc4, *, n_pages):
    db, hd, ts = qt.shape
    _, nh, dh, ps = kc4.shape
    ppb = MOBA_BLOCK // ps
    nsl = nh * ts * MOBA_TOPK
    vec = pl.BlockSpec((1, hd, ts), lambda b, idx, pt: (b, 0, 0))
    return pl.pallas_call(
        functools.partial(_attn_s_body, nh=nh, ts=ts, n_pages=n_pages, ppb=ppb),
        grid_spec=pltpu.PrefetchScalarGridSpec(
            num_scalar_prefetch=2,
            grid=(db,),
            in_specs=[vec, vec, vec, pl.BlockSpec(memory_space=pl.ANY), pl.BlockSpec(memory_space=pl.ANY)],
            out_specs=vec,
            scratch_shapes=[pltpu.VMEM((2, nsl, dh, MOBA_BLOCK), F32),
                            pltpu.VMEM((2, nsl, dh, MOBA_BLOCK), F32),
                            pltpu.SemaphoreType.DMA((2,))],
        ),
        out_shape=jax.ShapeDtypeStruct((db, hd, ts), F32),
        compiler_params=_cparams(("arbitrary",)),
        name="attn_s",
    )(idx_flat, pt_flat, qt, knt, vnt, kc4, vc4)


def _post_body(x_ref, sh1_ref, sc1_ref, gt1_ref, sh2_ref, sc2_ref, oa_ref, ob_ref,
               g1_ref, g2_ref, wm_ref, bm_ref, wpa_ref, wpb_ref, wo_ref, wr_ref, br_ref,
               x1_ref, tok_ref, gate_ref, cnt_ref):
    x = x_ref[...]
    d = x.shape[1]
    h = _norm_mod(x, g1_ref[...], sh1_ref[0], sc1_ref[0])
    gates = _sigmoid(_dot(h, wm_ref[...]) + bm_ref[...])
    merged = gates[:, :d] * _dot(oa_ref[...], wpa_ref[...]) + gates[:, d:] * _dot(ob_ref[...], wpb_ref[...])
    x1 = x + gt1_ref[0] * _dot(merged, wo_ref[...])
    x1_ref[...] = x1
    h2 = _norm_mod(x1, g2_ref[...], sh2_ref[0], sc2_ref[0])
    tok_ref[...] = h2
    logits = _dot3(h2, wr_ref[...]) + br_ref[...]
    ne = logits.shape[1]
    lane = lax.broadcasted_iota(I32, logits.shape, 1).astype(F32)
    rem = logits
    vals, hots = [], []
    for _ in range(TOPK_E):
        m = jnp.max(rem, axis=1, keepdims=True)
        idx = jnp.min(jnp.where(rem == m, lane, float(ne)), axis=1, keepdims=True)
        hot = lane == idx
        vals.append(m)
        hots.append(hot)
        rem = jnp.where(hot, -jnp.inf, rem)
    es = [jnp.exp(v - vals[0]) for v in vals]
    den = es[0]
    for e in es[1:]:
        den = den + e
    gate = jnp.zeros_like(logits)
    sel = jnp.zeros_like(logits)
    for e, hot in zip(es, hots):
        gate = gate + jnp.where(hot, e / den, 0.0)
        sel = sel + jnp.where(hot, 1.0, 0.0)
    gate_ref[...] = jnp.concatenate([gate, sel], axis=1)
    cnt_ref[0] = jnp.concatenate(
        [jnp.sum(sel[r * ROW_TILE:(r + 1) * ROW_TILE], axis=0, keepdims=True) for r in range(cnt_ref.shape[1])],
        axis=0)


def _post(x2d, mods, oa, ob, g1, g2, wm, bm, wpa, wpb, wo, wr, br, *, nb, t):
    n, d = x2d.shape
    tm = min(TOK_TILE, t)
    nt = t // tm
    ne = wr.shape[1]
    mrows = mods[0].shape[1]
    if mrows == 1:
        mod_spec = pl.BlockSpec((1, 1, d), lambda b, j: (b, 0, 0))
    else:
        mod_spec = pl.BlockSpec((1, tm, d), lambda b, j: (b, j, 0))
    full = lambda a: pl.BlockSpec(a.shape, lambda b, j: (0,) * a.ndim)
    tok = lambda w: pl.BlockSpec((tm, w), lambda b, j: (b * nt + j, 0))
    ws = (g1, g2, wm, bm, wpa, wpb, wo, wr, br)
    return pl.pallas_call(
        _post_body,
        grid=(nb, nt),
        in_specs=[tok(d)] + [mod_spec] * 5 + [tok(oa.shape[1]), tok(ob.shape[1])] + [full(w) for w in ws],
        out_specs=[tok(d), tok(d), tok(2 * ne),
                   pl.BlockSpec((1, tm // ROW_TILE, ne), lambda b, j: (b * nt + j, 0, 0))],
        out_shape=[jax.ShapeDtypeStruct((n, d), F32),
                   jax.ShapeDtypeStruct((n, d), F32),
                   jax.ShapeDtypeStruct((n, 2 * ne), F32),
                   jax.ShapeDtypeStruct((nb * nt, tm // ROW_TILE, ne), F32)],
        compiler_params=_cparams(("arbitrary", "arbitrary")),
        name="post",
    )(x2d, *mods, oa, ob, *ws)


def _route_body(gs_ref, base_ref, pos_ref, g4_ref):
    gs = gs_ref[...]
    ne = gs.shape[1] // 2
    gate, sel = gs[:, :ne], gs[:, ne:]
    tm = gs.shape[0]
    r = lax.broadcasted_iota(I32, (tm, tm), 0)
    c = lax.broadcasted_iota(I32, (tm, tm), 1)
    tri = jnp.where(c < r, 1.0, 0.0).astype(BF16)
    rank = jnp.dot(tri, sel.astype(BF16), preferred_element_type=F32)
    posm = base_ref[0] + rank
    lane = lax.broadcasted_iota(I32, sel.shape, 1).astype(F32)
    rem = sel > 0.0
    ps, gv = [], []
    for _ in range(TOPK_E):
        idx = jnp.min(jnp.where(rem, lane, float(ne)), axis=1, keepdims=True)
        hot = lane == idx
        ps.append(jnp.sum(jnp.where(hot, posm, 0.0), axis=1, keepdims=True))
        gv.append(jnp.sum(jnp.where(hot, gate, 0.0), axis=1, keepdims=True))
        rem = rem & jnp.logical_not(hot)
    pos_ref[...] = jnp.concatenate(ps, axis=1).astype(I32)
    g4_ref[...] = jnp.concatenate(gv, axis=1)


def _route(gs, base):
    n, ne2 = gs.shape
    tm = ROW_TILE
    return pl.pallas_call(
        _route_body,
        grid=(n // tm,),
        in_specs=[pl.BlockSpec((tm, ne2), lambda j: (j, 0)),
                  pl.BlockSpec((1, 1, ne2 // 2), lambda j: (j, 0, 0))],
        out_specs=[pl.BlockSpec((tm, TOPK_E), lambda j: (j, 0)),
                   pl.BlockSpec((tm, TOPK_E), lambda j: (j, 0))],
        out_shape=[jax.ShapeDtypeStruct((n, TOPK_E), I32),
                   jax.ShapeDtypeStruct((n, TOPK_E), F32)],
        compiler_params=_cparams(("arbitrary",)),
        name="route",
    )(gs, base)


def _segment_copies(ls_ref, gb_ref, nc_ref, tile, n_exp, make, act):
    per_big = BIG_CHUNK // CHUNK

    def per_expert(e, c):
        i = tile * n_exp + e
        ls = ls_ref[i]
        gb = gb_ref[i]
        nbig = nc_ref[i] // per_big

        def chunk(rows, base):
            def f(k, c2):
                off = base + k * rows
                cp = make(pl.multiple_of(ls + off, CHUNK), pl.multiple_of(gb + off, CHUNK), rows)
                cp.start() if act == "start" else cp.wait()
                return c2
            return f

        lax.fori_loop(0, nbig, chunk(BIG_CHUNK, 0), 0)
        lax.fori_loop(0, nc_ref[i] - nbig * per_big, chunk(CHUNK, nbig * BIG_CHUNK), 0)
        return c

    lax.fori_loop(0, n_exp, per_expert, 0)


def _dispatch_body(ls_ref, gb_ref, nc_ref, ts_ref, tn_ref, lr_ref, tok_ref, *rest, tile0, n_exp, first):
    if first:
        xs_ref, loc, zrow, sem, zsem = rest
    else:
        _, xs_ref, loc, zrow, sem, zsem = rest
    j = pl.program_id(0)
    nj = pl.num_programs(0)
    slot = j % 2
    nloc, tm = loc.shape[1], tok_ref.shape[0]

    if first:
        @pl.when(j == 0)
        def _():
            zrow[...] = jnp.zeros_like(zrow)

            def tails(act):
                def per_expert(e, c):
                    def per_chunk(k, c2):
                        cp = pltpu.make_async_copy(
                            zrow, xs_ref.at[pl.ds(pl.multiple_of(ts_ref[e] + k * CHUNK, CHUNK), CHUNK)], zsem)
                        cp.start() if act == "start" else cp.wait()
                        return c2
                    lax.fori_loop(0, tn_ref[e], per_chunk, 0)
                    return c
                lax.fori_loop(0, n_exp, per_expert, 0)

            tails("start")
            tails("wait")

    r = lax.broadcasted_iota(I32, (nloc, tm), 0)
    perm = jnp.zeros((nloc, tm), F32)
    for k in range(TOPK_E):
        perm = perm + jnp.where(r == lr_ref[0, k:k + 1, :], 1.0, 0.0)
    loc[slot] = jnp.dot(perm.astype(BF16), tok_ref[...].astype(BF16), preferred_element_type=F32)

    def make(slot_):
        return lambda ls, gb, rows: pltpu.make_async_copy(
            loc.at[slot_, pl.ds(ls, rows)], xs_ref.at[pl.ds(gb, rows)], sem.at[slot_])

    _segment_copies(ls_ref, gb_ref, nc_ref, tile0 + j, n_exp, make(slot), "start")

    @pl.when(j > 0)
    def _():
        _segment_copies(ls_ref, gb_ref, nc_ref, tile0 + j - 1, n_exp, make(1 - slot), "wait")

    @pl.when(j == nj - 1)
    def _():
        _segment_copies(ls_ref, gb_ref, nc_ref, tile0 + j, n_exp, make(slot), "wait")


def _dispatch(seg, tails, lr_rows, tok, xs, *, tile0, rows):
    n, d = tok.shape
    tm = ROW_TILE
    n_exp = tails[0].shape[0]
    first = xs is None
    nloc = TOPK_E * tm + n_exp * CHUNK
    in_specs = [pl.BlockSpec((1, TOPK_E, tm), lambda j, *_: (tile0 + j, 0, 0)),
                pl.BlockSpec((tm, d), lambda j, *_: (j, 0))]
    args = [lr_rows, tok]
    if not first:
        in_specs.append(pl.BlockSpec(memory_space=pl.ANY))
        args.append(xs)
    return pl.pallas_call(
        functools.partial(_dispatch_body, tile0=tile0, n_exp=n_exp, first=first),
        grid_spec=pltpu.PrefetchScalarGridSpec(
            num_scalar_prefetch=5,
            grid=(n // tm,),
            in_specs=in_specs,
            out_specs=pl.BlockSpec(memory_space=pl.ANY),
            scratch_shapes=[pltpu.VMEM((2, nloc, d), F32), pltpu.VMEM((CHUNK, d), F32),
                            pltpu.SemaphoreType.DMA((2,)), pltpu.SemaphoreType.DMA(())],
        ),
        out_shape=jax.ShapeDtypeStruct((rows, d), F32),
        input_output_aliases={} if first else {7: 0},
        compiler_params=_cparams(("arbitrary",)),
        name="dispatch",
    )(*seg, *tails, *args)


def _moe_body(te_ref, nt_ref, x_ref, wu_ref, bg_ref, bl_ref, wd_ref, bd_ref, o_ref, wg_s, wl_s, wd_s):
    g = pl.program_id(0)

    @pl.when((g == 0) | (te_ref[g] != te_ref[jnp.maximum(g - 1, 0)]))
    def _():
        w2 = 2 * LANES
        r = lax.broadcasted_iota(I32, (w2, w2), 0)
        c = lax.broadcasted_iota(I32, (w2, w2), 1)
        src = jnp.where(c < LANES, 2 * c, 2 * (c - LANES) + 1)
        perm = jnp.where(r == src, 1.0, 0.0).astype(BF16)
        for k in range(wu_ref.shape[2] // w2):
            pair = jnp.dot(wu_ref[0, :, k * w2:(k + 1) * w2].astype(BF16), perm, preferred_element_type=F32)
            wg_s[:, k * LANES:(k + 1) * LANES] = pair[:, :LANES].astype(BF16)
            wl_s[:, k * LANES:(k + 1) * LANES] = pair[:, LANES:].astype(BF16)
        wd_s[...] = wd_ref[0].astype(BF16)

    @pl.when(g < nt_ref[0])
    def _():
        x = x_ref[...].astype(BF16)
        glu = jnp.minimum(jnp.dot(x, wg_s[...], preferred_element_type=F32) + bg_ref[0], SWIGLU_LIMIT)
        lin = jnp.clip(jnp.dot(x, wl_s[...], preferred_element_type=F32) + bl_ref[0], -SWIGLU_LIMIT, SWIGLU_LIMIT)
        act = glu * _sigmoid(SWIGLU_ALPHA * glu) * (lin + 1.0)
        o_ref[...] = jnp.dot(act.astype(BF16), wd_s[...], preferred_element_type=F32) + bd_ref[0]

    @pl.when(g >= nt_ref[0])
    def _():
        o_ref[...] = jnp.zeros_like(o_ref)


def _moe(tile_expert, n_tiles, xs, wu, bg, bl, wd, bd):
    r, d = xs.shape
    tm = MOE_TILE
    dff = wd.shape[1]
    row = lambda g, te, nt: (jnp.minimum(g, nt[0] - 1), 0)
    wsp = lambda a: pl.BlockSpec((1,) + a.shape[1:], lambda g, te, nt: (te[g], 0, 0))
    return pl.pallas_call(
        _moe_body,
        grid_spec=pltpu.PrefetchScalarGridSpec(
            num_scalar_prefetch=2,
            grid=(r // tm,),
            in_specs=[pl.BlockSpec((tm, d), row), wsp(wu), wsp(bg), wsp(bl), wsp(wd), wsp(bd)],
            out_specs=pl.BlockSpec((tm, d), lambda g, te, nt: (g, 0)),
            scratch_shapes=[pltpu.VMEM((d, dff), BF16), pltpu.VMEM((d, dff), BF16), pltpu.VMEM((dff, d), BF16)],
        ),
        out_shape=jax.ShapeDtypeStruct((r, d), F32),
        compiler_params=_cparams(("arbitrary",)),
        name="moe",
    )(tile_expert, n_tiles, xs, wu, bg, bl, wd, bd)


def _combine_body(ls_ref, gb_ref, nc_ref, lr_ref, g4_ref, x1_ref, gt2_ref, nf_ref, ys_ref, y_ref, loc, sem,
                  *, tile0, n_exp):
    j = pl.program_id(0)
    nj = pl.num_programs(0)
    slot = j % 2
    nloc, tm = loc.shape[1], x1_ref.shape[0]

    def make(slot_):
        return lambda ls, gb, rows: pltpu.make_async_copy(
            ys_ref.at[pl.ds(gb, rows)], loc.at[slot_, pl.ds(ls, rows)], sem.at[slot_])

    @pl.when(j == 0)
    def _():
        loc[...] = jnp.zeros_like(loc)
        _segment_copies(ls_ref, gb_ref, nc_ref, tile0, n_exp, make(0), "start")

    @pl.when(j + 1 < nj)
    def _():
        _segment_copies(ls_ref, gb_ref, nc_ref, tile0 + j + 1, n_exp, make(1 - slot), "start")

    _segment_copies(ls_ref, gb_ref, nc_ref, tile0 + j, n_exp, make(slot), "wait")

    lane = lax.broadcasted_iota(I32, (tm, nloc), 1)
    lr = lr_ref[...]
    g4 = g4_ref[...]
    w = jnp.zeros((tm, nloc), F32)
    for k in range(TOPK_E):
        w = w + jnp.where(lane == lr[:, k:k + 1], g4[:, k:k + 1], 0.0)
    wh, wl = _split(w)
    yb = loc[slot].astype(BF16)
    ff = jnp.dot(wh, yb, preferred_element_type=F32) + jnp.dot(wl, yb, preferred_element_type=F32)
    y_ref[...] = _rms(x1_ref[...] + gt2_ref[0] * ff, nf_ref[...])


def _combine(seg, lr, g4, x1, gt2, nf, ys, *, tile0, n_exp, t):
    n, d = x1.shape
    tm = min(ROW_TILE, t)
    nt = t // tm
    nloc = TOPK_E * tm + n_exp * CHUNK
    if gt2.shape[1] == 1:
        mod_spec = pl.BlockSpec((1, 1, d), lambda j, *_: (j // nt, 0, 0))
    else:
        mod_spec = pl.BlockSpec((1, tm, d), lambda j, *_: (j // nt, j % nt, 0))
    return pl.pallas_call(
        functools.partial(_combine_body, tile0=tile0, n_exp=n_exp),
        grid_spec=pltpu.PrefetchScalarGridSpec(
            num_scalar_prefetch=3,
            grid=(n // tm,),
            in_specs=[pl.BlockSpec((tm, TOPK_E), lambda j, *_: (tile0 + j, 0)),
                      pl.BlockSpec((tm, TOPK_E), lambda j, *_: (tile0 + j, 0)),
                      pl.BlockSpec((tm, d), lambda j, *_: (j, 0)),
                      mod_spec,
                      pl.BlockSpec((1, d), lambda j, *_: (0, 0)),
                      pl.BlockSpec(memory_space=pl.ANY)],
            out_specs=pl.BlockSpec((tm, d), lambda j, *_: (j, 0)),
            scratch_shapes=[pltpu.VMEM((2, nloc, d), F32), pltpu.SemaphoreType.DMA((2,))],
        ),
        out_shape=jax.ShapeDtypeStruct((n, d), F32),
        compiler_params=_cparams(("arbitrary",)),
        name="combine",
    )(*seg, lr, g4, x1, gt2, nf, ys)


def _block_diag(w):
    g, a, b = w.shape
    eye = jnp.eye(g, dtype=w.dtype)
    return (eye[:, None, :, None] * w[:, :, None, :]).reshape(g * a, g * b)


def _layer0(a):
    return a.reshape(a.shape[1:])

def kernel(x_prompt, x_sample, cache_k, cache_v, state_h, state_conv, page_table, c_prompt, c_sample,
           w_ada, b_ada, norm1, norm2, w_in, conv_w, conv_b, w_rg_a, b_rg_a, w_rg_i, b_rg_i, rg_lambda,
           w_proj_a, w_proj_b, w_merge, b_merge, w_out, w_router, b_router, w_up, b_up, w_down, b_down, norm_f):
    assert w_ada.shape[0] == 1, "single trunk layer"
    bp, t, d = x_prompt.shape
    db, ts, _ = x_sample.shape
    _, n_phys, ps, nh, _ = cache_k.shape
    hd = nh * HEAD_DIM
    da = hd
    dr = conv_w.shape[2]
    n_pages = page_table.shape[1]
    ne = w_router.shape[2]
    np_, ns_ = bp * t, db * ts
    nblk = t // MOBA_BLOCK
    assert t % TOK_TILE == 0 and ns_ == TOK_TILE and n_pages % 16 == 0 and (dr // 2) % w_rg_a.shape[2] == 0
    assert (n_pages * ps) % MOBA_BLOCK == 0 and (n_pages * ps) // MOBA_BLOCK >= MOBA_TOPK

    mod = _ada(jnp.concatenate([c_prompt, c_sample], axis=0), _layer0(w_ada), _layer0(b_ada))
    mod = mod.reshape(bp + db, N_MOD, d)
    mp = [mod[:bp, i][:, None, :] for i in range(N_MOD)]
    ms = [jnp.tile(mod[bp:, i], (ts, 1))[None] for i in range(N_MOD)]

    g1, g2, nf = norm1, norm2, norm_f[None]
    w_in0 = _layer0(w_in)
    wqkvt, wk, wug = w_in0[:, :3 * da].T, w_in0[:, da:2 * da], w_in0[:, 3 * da:]

    xp2 = x_prompt.reshape(np_, d)
    xs2 = jnp.swapaxes(x_sample, 0, 1).reshape(ns_, d)

    qt_p, kt_p, vt_p, u_p, g_p, kb_p, vtb_p, km_p = _proj(
        xp2, mp[0], mp[1], g1, wqkvt.astype(BF16), wug.astype(BF16), wk.astype(BF16), nb=bp, t=t)
    qt_s, kt_s, vt_s, u_s, g_s = _proj(xs2, ms[0], ms[1], g1, wqkvt, wug, nb=1, t=ns_)

    oa_p = _attn_p(qt_p, kb_p, vtb_p, km_p.reshape(bp, nblk, da), nb=bp, t=t)

    kc4 = jnp.transpose(cache_k.reshape(n_phys, ps, nh, HEAD_DIM), (0, 2, 3, 1))
    vc4 = jnp.transpose(cache_v.reshape(n_phys, ps, nh, HEAD_DIM), (0, 2, 3, 1))
    pt_flat = page_table.reshape(-1)
    km = _kmean_s(pt_flat, kc4.reshape(n_phys, hd, ps), db=db, n_pages=n_pages)
    km = jnp.transpose(km, (0, 1, 3, 2)).reshape(db, -1, hd)
    tmaj = lambda a: jnp.transpose(a.reshape(hd, ts, db), (2, 0, 1))
    qs3, kn3, vn3 = tmaj(qt_s), tmaj(kt_s), tmaj(vt_s)
    idx = _sel_s(jnp.transpose(qt_s.reshape(hd, ts, db), (2, 1, 0)), km)
    idx = jnp.swapaxes(idx, 1, 2)
    ot_s = _attn_s(idx.reshape(-1), pt_flat, qs3, kn3, vn3, kc4, vc4, n_pages=n_pages)
    oa_s = jnp.transpose(ot_s, (2, 0, 1)).reshape(ns_, hd)

    cw, cb = _layer0(conv_w), conv_b
    wa, wi = _block_diag(_layer0(w_rg_a)), _block_diag(_layer0(w_rg_i))
    ba, bi, lam = b_rg_a, b_rg_i, rg_lambda
    ob_p, hl_p, cn_p = _rg_p(u_p, g_p, cw, cb, wa, ba, wi, bi, lam, nb=bp, t=t)
    ob_s, hl_s, cn_s = _rg_s(u_s.reshape(ts, db, dr), g_s.reshape(ts, db, dr),
                             jnp.swapaxes(_layer0(state_conv), 0, 1), _layer0(state_h),
                             cw, cb, wa, ba, wi, bi, lam)

    wsm = (g1, g2, _layer0(w_merge).astype(BF16), b_merge, _layer0(w_proj_a).astype(BF16),
           _layer0(w_proj_b).astype(BF16), _layer0(w_out).astype(BF16), _layer0(w_router), b_router)
    x1_p, tok_p, gs_p, cnt_p = _post(xp2, mp[:5], oa_p, ob_p, *wsm, nb=bp, t=t)
    x1_s, tok_s, gs_s, cnt_s = _post(xs2, ms[:5], oa_s.astype(BF16), ob_s.reshape(ns_, dr).astype(BF16), *wsm,
                                     nb=1, t=ns_)

    gs = jnp.concatenate([gs_p, gs_s], axis=0)
    cnt = jnp.concatenate([cnt_p, cnt_s], axis=0).reshape(-1, ne).astype(I32)
    n_rt = cnt.shape[0]
    seg = (cnt + CHUNK - 1) // CHUNK * CHUNK
    lstart = jnp.cumsum(seg, axis=1) - seg
    tot = jnp.sum(seg, axis=0)
    tiles_e = (tot + MOE_TILE - 1) // MOE_TILE
    tile_end = jnp.cumsum(tiles_e)
    poff = (tile_end - tiles_e) * MOE_TILE
    gbase = poff[None, :] + jnp.cumsum(seg, axis=0) - seg
    seg_tabs = (lstart.reshape(-1), gbase.reshape(-1), (seg // CHUNK).reshape(-1))
    tail_tabs = (poff + tot, (tiles_e * MOE_TILE - tot) // CHUNK)
    n_all = np_ + ns_
    rows_max = n_all * TOPK_E + n_rt * ne * (CHUNK - 1) + ne * MOE_TILE
    g_max = -(-rows_max // MOE_TILE)
    n_tiles = tile_end[-1:].astype(I32)
    tile_id = jnp.minimum(jnp.arange(g_max, dtype=I32), n_tiles - 1)
    te = jnp.sum((tile_end[None, :] <= tile_id[:, None]).astype(I32), axis=1)
    lr, g4 = _route(gs, lstart.astype(F32)[:, None, :])
    lr_rows = jnp.swapaxes(lr.reshape(n_rt, ROW_TILE, TOPK_E), 1, 2)
    rt_p = np_ // ROW_TILE

    xs = _dispatch(seg_tabs, tail_tabs, lr_rows, tok_p, None, tile0=0, rows=g_max * MOE_TILE)
    xs = _dispatch(seg_tabs, tail_tabs, lr_rows, tok_s, xs, tile0=rt_p, rows=g_max * MOE_TILE)
    b_up0 = _layer0(b_up)
    ys = _moe(te, n_tiles, xs, _layer0(w_up), b_up0[:, None, 0::2], b_up0[:, None, 1::2],
              _layer0(w_down), _layer0(b_down)[:, None, :])
    y_p = _combine(seg_tabs, lr, g4, x1_p, mp[5], nf, ys, tile0=0, n_exp=ne, t=t)
    y_s = _combine(seg_tabs, lr, g4, x1_s, ms[5], nf, ys, tile0=rt_p, n_exp=ne, t=ns_)

    y_prompt = y_p.reshape(bp, t, d)
    y_sample = jnp.swapaxes(y_s.reshape(ts, db, d), 0, 1)
    to5 = lambda a: jnp.transpose(a.reshape(bp, nh, HEAD_DIM, t), (0, 3, 1, 2))[None]
    k_prompt, v_prompt = to5(kt_p), to5(vt_p)
    to5s = lambda a: jnp.transpose(a.reshape(nh, HEAD_DIM, ts, db), (3, 2, 0, 1))[None]
    k_sample, v_sample = to5s(kt_s), to5s(vt_s)
    h_prompt = hl_p.reshape(1, bp, dr)
    conv_prompt = cn_p[None]
    h_sample = hl_s[None]
    conv_sample = jnp.swapaxes(cn_s, 0, 1)[None]
    return (y_prompt, y_sample, k_prompt, v_prompt, h_prompt, conv_prompt,
            k_sample, v_sample, h_sample, conv_sample)
```

```python
import functools

import jax
import jax.numpy as jnp
from jax import lax
from jax.experimental import pallas as pl
from jax.experimental.pallas import tpu as pltpu

F32 = jnp.float32
BF16 = jnp.bfloat16
I32 = jnp.int32

HEAD_DIM = 64
MOBA_BLOCK = 256
MOBA_TOPK = 3
RG_C = 8.0
TOPK_E = 4
SWIGLU_ALPHA = 1.702
SWIGLU_LIMIT = 7.0
EPS = 1e-6
N_MOD = 6

LANES = 128
TOK_TILE = 512
ATTN_HEADS = 4
RG_TILE = 256
MOE_TILE = 512
ROW_TILE = 256
CHUNK = 8
BIG_CHUNK = 32
KMEAN_PAGES = 32
VMEM_LIMIT = 56 * 1024 * 1024


def _cparams(sem):
    return pltpu.CompilerParams(dimension_semantics=sem, vmem_limit_bytes=VMEM_LIMIT)


def _dot(a, b):
    return jnp.dot(a.astype(BF16), b.astype(BF16), preferred_element_type=F32)


def _dot_t(a, b):
    return lax.dot_general(a.astype(BF16), b.astype(BF16), (((1,), (1,)), ((), ())),
                           preferred_element_type=F32)


def _split(a):
    hi = a.astype(BF16)
    lo = (a - hi.astype(F32)).astype(BF16)
    return hi, lo


def _dot3(a, b):
    ah, al = _split(a)
    bh, bl = _split(b)
    return (jnp.dot(ah, bh, preferred_element_type=F32)
            + jnp.dot(al, bh, preferred_element_type=F32)
            + jnp.dot(ah, bl, preferred_element_type=F32))


def _dot3_t(a, b):
    ah, al = _split(a)
    bh, bl = _split(b)
    dn = (((1,), (1,)), ((), ()))
    return (lax.dot_general(ah, bh, dn, preferred_element_type=F32)
            + lax.dot_general(al, bh, dn, preferred_element_type=F32)
            + lax.dot_general(ah, bl, dn, preferred_element_type=F32))


def _sigmoid(x):
    return 1.0 / (1.0 + jnp.exp(-x))


def _rms(x, g):
    return x * lax.rsqrt(jnp.mean(x * x, axis=-1, keepdims=True) + EPS) * g


def _norm_mod(x, g, shift, scale):
    return _rms(x, g) * (1.0 + scale) + shift


def _gelu_tanh(x):
    return 0.5 * x * (1.0 + jnp.tanh(0.7978845608028654 * (x + 0.044715 * (x * x * x))))


def _ada_body(c_ref, w_ref, b_ref, o_ref):
    c = c_ref[...]
    o_ref[...] = _dot3(c * _sigmoid(c), w_ref[...]) + b_ref[...]


def _ada(c, w, b):
    n, d = c.shape
    e = w.shape[1]
    te = 1024
    return pl.pallas_call(
        _ada_body,
        grid=(e // te,),
        in_specs=[pl.BlockSpec((n, d), lambda j: (0, 0)),
                  pl.BlockSpec((d, te), lambda j: (0, j)),
                  pl.BlockSpec((1, te), lambda j: (0, j))],
        out_specs=pl.BlockSpec((n, te), lambda j: (0, j)),
        out_shape=jax.ShapeDtypeStruct((n, e), F32),
        compiler_params=_cparams(("arbitrary",)),
        name="ada",
    )(c, w, b.reshape(1, e))


def _proj_body(x_ref, sh_ref, sc_ref, g1_ref, wqkvt_ref, wug_ref, *rest, prompt):
    h = _norm_mod(x_ref[...], g1_ref[...], sh_ref[0], sc_ref[0])
    mm, mmt = (_dot, _dot_t) if prompt else (_dot3, _dot3_t)
    if prompt:
        wk_ref, qt_ref, kt_ref, vt_ref, u_ref, g_ref, kb_ref, vtb_ref, km_ref = rest
    else:
        qt_ref, kt_ref, vt_ref, u_ref, g_ref = rest
    da = qt_ref.shape[1]
    qkvt = mmt(wqkvt_ref[...], h)
    qt_ref[0] = qkvt[:da]
    kt_ref[0] = qkvt[da:2 * da]
    vt = qkvt[2 * da:]
    vt_ref[0] = vt
    ug = mm(h, wug_ref[...])
    dr = u_ref.shape[1]
    u_ref[...] = ug[:, :dr]
    g_ref[...] = ug[:, dr:]
    if prompt:
        k = mm(h, wk_ref[...])
        kb_ref[...] = k.astype(BF16)
        vtb_ref[0] = vt.astype(BF16)
        means = [jnp.sum(k[n * MOBA_BLOCK:(n + 1) * MOBA_BLOCK], axis=0, keepdims=True) * (1.0 / MOBA_BLOCK)
                 for n in range(k.shape[0] // MOBA_BLOCK)]
        km_ref[0] = jnp.concatenate(means, axis=0)


def _proj(x2d, shift, scale, g1, wqkvt, wug, wk=None, *, nb, t):
    prompt = wk is not None
    n, d = x2d.shape
    tm = min(TOK_TILE, t)
    nt = t // tm
    da = wqkvt.shape[0] // 3
    dr = wug.shape[1] // 2
    mrows = shift.shape[1]
    if mrows == 1:
        mod_spec = pl.BlockSpec((1, 1, d), lambda b, j: (b, 0, 0))
    else:
        mod_spec = pl.BlockSpec((1, tm, d), lambda b, j: (b, j, 0))
    full = lambda a: pl.BlockSpec(a.shape, lambda b, j: (0,) * a.ndim)
    tok = lambda w: pl.BlockSpec((tm, w), lambda b, j: (b * nt + j, 0))
    featmaj = pl.BlockSpec((1, da, tm), lambda b, j: (b, 0, j))
    in_specs = [tok(d), mod_spec, mod_spec, full(g1), full(wqkvt), full(wug)]
    out_specs = [featmaj, featmaj, featmaj, tok(dr), tok(dr)]
    out_shape = [jax.ShapeDtypeStruct((nb, da, t), F32)] * 3 + [jax.ShapeDtypeStruct((n, dr), F32)] * 2
    args = [x2d, shift, scale, g1, wqkvt, wug]
    if prompt:
        bpt = tm // MOBA_BLOCK
        in_specs.append(full(wk))
        args.append(wk)
        out_specs += [tok(da), featmaj,
                      pl.BlockSpec((1, bpt, da), lambda b, j: (b * nt + j, 0, 0))]
        out_shape += [jax.ShapeDtypeStruct((n, da), BF16),
                      jax.ShapeDtypeStruct((nb, da, t), BF16),
                      jax.ShapeDtypeStruct((nb * nt, bpt, da), F32)]
    return pl.pallas_call(
        functools.partial(_proj_body, prompt=prompt),
        grid=(nb, nt),
        in_specs=in_specs,
        out_specs=out_specs,
        out_shape=out_shape,
        compiler_params=_cparams(("arbitrary", "arbitrary")),
        name="proj_p" if prompt else "proj_s",
    )(*args)


def _attn_p_body(qt_ref, kb_ref, kd_ref, vtb_ref, vd_ref, km_ref, o_ref, sel_ref, *, nblk):
    i = pl.program_id(2)
    bs = MOBA_BLOCK
    krow = lax.broadcasted_iota(I32, (bs, bs), 0)
    qcol = lax.broadcasted_iota(I32, (bs, bs), 1)
    blk = lax.broadcasted_iota(I32, (nblk, bs), 0)
    valid = blk < i
    nhh = qt_ref.shape[1] // HEAD_DIM
    heads = [slice(hh * HEAD_DIM, (hh + 1) * HEAD_DIM) for hh in range(nhh)]

    def run(npast):
        outs = []
        for hh, hs in enumerate(heads):
            qt = qt_ref[0, hs, :]
            sv = jnp.where(valid, _dot3(km_ref[0, :, hs], qt), -jnp.inf)
            cnt = jnp.zeros_like(sv)
            for m in range(nblk - 1):
                sm = sv[m:m + 1, :]
                beats = (sm > sv) | ((sm == sv) & (m < blk))
                cnt = cnt + jnp.where(beats, 1.0, 0.0)
            sel_ref[hh] = jnp.where(valid & (cnt < MOBA_TOPK), 1.0, 0.0)

            qb = (qt * (HEAD_DIM ** -0.5)).astype(BF16)
            sd = jnp.dot(kd_ref[:, hs], qb, preferred_element_type=F32)
            sd = jnp.where(krow <= qcol, sd, -jnp.inf)
            sps = []
            if npast:
                sp = jnp.dot(kb_ref[0:npast * bs, hs], qb, preferred_element_type=F32)
                sps = [jnp.where(sel_ref[hh, j:j + 1, :] > 0.0, sp[j * bs:(j + 1) * bs], -jnp.inf)
                       for j in range(npast)]
            mx = jnp.max(sd, axis=0, keepdims=True)
            for sj in sps:
                mx = jnp.maximum(mx, jnp.max(sj, axis=0, keepdims=True))
            pd = jnp.exp(sd - mx)
            den = jnp.sum(pd, axis=0, keepdims=True)
            pps = []
            for sj in sps:
                pj = jnp.exp(sj - mx)
                den = den + jnp.sum(pj, axis=0, keepdims=True)
                pps.append(pj.astype(BF16))
            acc = jnp.dot(vd_ref[0, hs, :], pd.astype(BF16), preferred_element_type=F32)
            if npast:
                acc = acc + jnp.dot(vtb_ref[0, hs, 0:npast * bs], jnp.concatenate(pps, axis=0),
                                    preferred_element_type=F32)
            outs.append(acc / den)
        o_ref[...] = jnp.concatenate(outs, axis=0).T.astype(o_ref.dtype)

    for npast in range(nblk):
        pl.when(i == npast)(functools.partial(run, npast))


def _attn_p(qt, kb, vtb, km, *, nb, t):
    da = qt.shape[1]
    nblk = t // MOBA_BLOCK
    hw = ATTN_HEADS * HEAD_DIM
    nhp = da // hw
    return pl.pallas_call(
        functools.partial(_attn_p_body, nblk=nblk),
        grid=(nb, nhp, nblk),
        in_specs=[pl.BlockSpec((1, hw, MOBA_BLOCK), lambda b, h, i: (b, h, i)),
                  pl.BlockSpec((t, hw), lambda b, h, i: (b, h)),
                  pl.BlockSpec((MOBA_BLOCK, hw), lambda b, h, i: (b * nblk + i, h)),
                  pl.BlockSpec((1, hw, t), lambda b, h, i: (b, h, 0)),
                  pl.BlockSpec((1, hw, MOBA_BLOCK), lambda b, h, i: (b, h, i)),
                  pl.BlockSpec((1, nblk, hw), lambda b, h, i: (b, 0, h))],
        out_specs=pl.BlockSpec((MOBA_BLOCK, hw), lambda b, h, i: (b * nblk + i, h)),
        out_shape=jax.ShapeDtypeStruct((nb * t, da), BF16),
        scratch_shapes=[pltpu.VMEM((ATTN_HEADS, nblk, MOBA_BLOCK), F32)],
        compiler_params=_cparams(("arbitrary", "arbitrary", "arbitrary")),
        name="attn_p",
    )(qt, kb, kb, vtb, vtb, km)


def _rg_gates(uc, wa_ref, ba_ref, wi_ref, bi_ref, lam_ref):
    r = _sigmoid(_dot3(uc, wa_ref[...]) + ba_ref[...])
    ig = _sigmoid(_dot3(uc, wi_ref[...]) + bi_ref[...])
    nl = -lam_ref[...]
    softplus = jnp.maximum(nl, 0.0) + jnp.log(1.0 + jnp.exp(-jnp.abs(nl)))
    log_a = -RG_C * r * softplus
    a = jnp.exp(log_a)
    bx = jnp.sqrt(1.0 - a * a) * (ig * uc)
    return a, bx


def _rg_p_body(u_ref, g_ref, cw_ref, cb_ref, wa_ref, ba_ref, wi_ref, bi_ref, lam_ref,
               o_ref, hl_ref, cn_ref, ubuf, hc):
    j = pl.program_id(2)
    tt = u_ref.shape[0]
    pad = 8

    @pl.when(j == 0)
    def _():
        ubuf[0:pad] = jnp.zeros((pad, ubuf.shape[1]), F32)
        hc[...] = jnp.zeros_like(hc)

    u = u_ref[...]
    ubuf[pad:pad + tt] = u
    cw = cw_ref[...]
    kw = cw.shape[0]
    uc = cb_ref[...] + u * cw[kw - 1:kw]
    for jj in range(kw - 1):
        off = pad - (kw - 1) + jj
        uc = uc + ubuf[off:off + tt] * cw[jj:jj + 1]
    a, bx = _rg_gates(uc, wa_ref, ba_ref, wi_ref, bi_ref, lam_ref)

    row = lax.broadcasted_iota(I32, a.shape, 0)
    d = 1
    while d < tt:
        keep = row >= d
        a_sh = jnp.where(keep, pltpu.roll(a, d, 0), 1.0)
        b_sh = jnp.where(keep, pltpu.roll(bx, d, 0), 0.0)
        bx = a * b_sh + bx
        a = a * a_sh
        d *= 2
    h = bx + a * hc[...]
    hc[...] = h[tt - 1:tt]
    hl_ref[0] = h[tt - 1:tt]
    cn_ref[0] = u[tt - (kw - 1):tt]
    ubuf[0:pad] = u[tt - pad:tt]
    o_ref[...] = (h * _gelu_tanh(g_ref[...])).astype(o_ref.dtype)


def _rg_p(u, g, cw, cb, wa, ba, wi, bi, lam, *, nb, t):
    n, dr = u.shape
    tt = min(RG_TILE, t)
    nt = t // tt
    kw = cw.shape[0]
    cwid = dr // 2
    vec = lambda a: pl.BlockSpec((a.shape[0], cwid), lambda b, c, j: (0, c))
    diag = pl.BlockSpec((cwid, cwid), lambda b, c, j: (c, c))
    tok = pl.BlockSpec((tt, cwid), lambda b, c, j: (b * nt + j, c))
    return pl.pallas_call(
        _rg_p_body,
        grid=(nb, 2, nt),
        in_specs=[tok, tok, vec(cw), vec(cb), diag, vec(ba), diag, vec(bi), vec(lam)],
        out_specs=[tok,
                   pl.BlockSpec((1, 1, cwid), lambda b, c, j: (b, 0, c)),
                   pl.BlockSpec((1, kw - 1, cwid), lambda b, c, j: (b, 0, c))],
        out_shape=[jax.ShapeDtypeStruct((n, dr), BF16),
                   jax.ShapeDtypeStruct((nb, 1, dr), F32),
                   jax.ShapeDtypeStruct((nb, kw - 1, dr), F32)],
        scratch_shapes=[pltpu.VMEM((8 + tt, cwid), F32), pltpu.VMEM((1, cwid), F32)],
        compiler_params=_cparams(("arbitrary", "arbitrary", "arbitrary")),
        name="rg_p",
    )(u, g, cw, cb, wa, ba, wi, bi, lam)


def _rg_s_body(u_ref, g_ref, cs_ref, h0_ref, cw_ref, cb_ref, wa_ref, ba_ref, wi_ref, bi_ref, lam_ref,
               o_ref, hl_ref, cn_ref):
    ts = u_ref.shape[0]
    cw = cw_ref[...]
    kw = cw.shape[0]
    full = [cs_ref[jj] for jj in range(kw - 1)] + [u_ref[tt] for tt in range(ts)]
    h = h0_ref[...]
    for tt in range(ts):
        uc = cb_ref[...]
        for jj in range(kw):
            uc = uc + full[tt + jj] * cw[jj:jj + 1]
        a, bx = _rg_gates(uc, wa_ref, ba_ref, wi_ref, bi_ref, lam_ref)
        h = a * h + bx
        o_ref[tt] = h * _gelu_tanh(g_ref[tt])
    hl_ref[...] = h
    for jj in range(kw - 1):
        cn_ref[jj] = full[ts + jj]


def _rg_s(u, g, cs, h0, cw, cb, wa, ba, wi, bi, lam):
    ts, db, dr = u.shape
    kw = cw.shape[0]
    return pl.pallas_call(
        _rg_s_body,
        out_shape=[jax.ShapeDtypeStruct((ts, db, dr), F32),
                   jax.ShapeDtypeStruct((db, dr), F32),
                   jax.ShapeDtypeStruct((kw - 1, db, dr), F32)],
        compiler_params=pltpu.CompilerParams(vmem_limit_bytes=VMEM_LIMIT),
        name="rg_s",
    )(u, g, cs, h0, cw, cb, wa, ba, wi, bi, lam)


def _kmean_s_body(pt_ref, *refs, npg):
    del pt_ref
    pages, o_ref = refs[:npg], refs[npg]
    cols = []
    for n in range(npg // 2):
        blk = pages[2 * n][0] + pages[2 * n + 1][0]
        cols.append(jnp.sum(blk, axis=1, keepdims=True) * (1.0 / MOBA_BLOCK))
    o_ref[0, 0] = jnp.concatenate(cols, axis=1)


def _kmean_s(pt_flat, kt3, *, db, n_pages):
    _, hd, ps = kt3.shape
    npg = KMEAN_PAGES if n_pages % KMEAN_PAGES == 0 else KMEAN_PAGES // 2
    nchunk = n_pages // npg
    specs = [pl.BlockSpec((1, hd, ps), functools.partial(
        lambda b, c, pt, p: (pt[b * n_pages + c * npg + p], 0, 0), p=p)) for p in range(npg)]
    return pl.pallas_call(
        functools.partial(_kmean_s_body, npg=npg),
        grid_spec=pltpu.PrefetchScalarGridSpec(
            num_scalar_prefetch=1,
            grid=(db, nchunk),
            in_specs=specs,
            out_specs=pl.BlockSpec((1, 1, hd, npg // 2), lambda b, c, pt: (b, c, 0, 0)),
        ),
        out_shape=jax.ShapeDtypeStruct((db, nchunk, hd, npg // 2), F32),
        compiler_params=_cparams(("arbitrary", "arbitrary")),
        name="kmean_s",
    )(pt_flat, *([kt3] * npg))


def _sel_s_body(q_ref, km_ref, o_ref, *, nh):
    nb, ts, hd = q_ref.shape
    nc = km_ref.shape[1]
    cand = lax.broadcasted_iota(I32, (nc, nh), 0).astype(F32)
    grp = lax.broadcasted_iota(I32, (hd, nh), 0) // HEAD_DIM == lax.broadcasted_iota(I32, (hd, nh), 1)
    head_sum = jnp.where(grp, 1.0, 0.0)

    def one(b, c):
        km = km_ref[b]
        outs = []
        for t in range(ts):
            s = _dot3(q_ref[b, t:t + 1, :] * km, head_sum)
            for _ in range(MOBA_TOPK):
                m = jnp.max(s, axis=0, keepdims=True)
                idx = jnp.min(jnp.where(s == m, cand, float(nc)), axis=0, keepdims=True)
                outs.append(idx)
                s = jnp.where(cand == idx, -jnp.inf, s)
        o_ref[b] = jnp.concatenate(outs, axis=0).astype(I32)
        return c

    lax.fori_loop(0, nb, one, 0)


def _sel_s(q, km):
    db, ts, hd = q.shape
    nc = km.shape[1]
    nh = hd // HEAD_DIM
    nb = 16 if db % 16 == 0 else 1
    return pl.pallas_call(
        functools.partial(_sel_s_body, nh=nh),
        grid=(db // nb,),
        in_specs=[pl.BlockSpec((nb, ts, hd), lambda b: (b, 0, 0)),
                  pl.BlockSpec((nb, nc, hd), lambda b: (b, 0, 0))],
        out_specs=pl.BlockSpec((nb, ts * MOBA_TOPK, nh), lambda b: (b, 0, 0)),
        out_shape=jax.ShapeDtypeStruct((db, ts * MOBA_TOPK, nh), I32),
        compiler_params=_cparams(("arbitrary",)),
        name="sel_s",
    )(q, km)


def _attn_s_body(idx_ref, pt_ref, qt_ref, knt_ref, vnt_ref, kc_ref, vc_ref, o_ref,
                 kbuf, vbuf, sem, *, nh, ts, n_pages, ppb):
    b = pl.program_id(0)
    nb = pl.num_programs(0)
    nsl = nh * ts * MOBA_TOPK
    ps = kc_ref.shape[3]

    def copies(bb, slot, s):
        h = s // (ts * MOBA_TOPK)
        blk = idx_ref[bb * nsl + s]
        out = []
        for pg in range(ppb):
            page = pt_ref[bb * n_pages + blk * ppb + pg]
            out.append(pltpu.make_async_copy(
                kc_ref.at[page, h], kbuf.at[slot, s, :, pl.ds(pg * ps, ps)], sem.at[slot]))
            out.append(pltpu.make_async_copy(
                vc_ref.at[page, h], vbuf.at[slot, s, :, pl.ds(pg * ps, ps)], sem.at[slot]))
        return out

    def start_all(bb, slot):
        def f(s, c):
            for cp in copies(bb, slot, s):
                cp.start()
            return c
        lax.fori_loop(0, nsl, f, 0)

    def wait_all(bb, slot):
        def f(s, c):
            for cp in copies(bb, slot, s):
                cp.wait()
            return c
        lax.fori_loop(0, nsl, f, 0)

    @pl.when(b == 0)
    def _():
        start_all(0, 0)

    @pl.when(b + 1 < nb)
    def _():
        start_all(b + 1, (b + 1) % 2)

    slot = b % 2
    wait_all(b, slot)

    scale = HEAD_DIM ** -0.5
    tcol = lax.broadcasted_iota(I32, (1, ts), 1)

    def head(h, c):
        hs = pl.ds(pl.multiple_of(h * HEAD_DIM, HEAD_DIM), HEAD_DIM)
        qh = qt_ref[0, hs, :] * scale
        knh = knt_ref[0, hs, :]
        vnh = vnt_ref[0, hs, :]
        cols = []
        for t in range(ts):
            qc = qh[:, t:t + 1]
            s_new = jnp.sum(qc * knh, axis=0, keepdims=True)
            s_new = jnp.where(tcol <= t, s_new, -jnp.inf)
            sls = [(h * ts + t) * MOBA_TOPK + j for j in range(MOBA_TOPK)]
            s_sel = [jnp.sum(qc * kbuf[slot, sl], axis=0, keepdims=True) for sl in sls]
            s_max = s_sel[0]
            for sj in s_sel[1:]:
                s_max = jnp.maximum(s_max, sj)
            m = jnp.maximum(jnp.max(s_new, axis=1, keepdims=True), jnp.max(s_max, axis=1, keepdims=True))
            p_new = jnp.exp(s_new - m)
            ps = [jnp.exp(sj - m) for sj in s_sel]
            p_sum = ps[0]
            pv = ps[0] * vbuf[slot, sls[0]]
            for pj, sl in zip(ps[1:], sls[1:]):
                p_sum = p_sum + pj
                pv = pv + pj * vbuf[slot, sl]
            l = jnp.sum(p_new, axis=1, keepdims=True) + jnp.sum(p_sum, axis=1, keepdims=True)
            acc = jnp.sum(p_new * vnh, axis=1, keepdims=True) + jnp.sum(pv, axis=1, keepdims=True)
            cols.append(acc / l)
        o_ref[0, hs, :] = jnp.concatenate(cols, axis=1)
        return c

    lax.fori_loop(0, nh, head, 0)


def _attn_s(idx_flat, pt_flat, qt, knt, vnt, kc4, vc4, *, n_pages):
    db, hd, ts = qt.shape
    _, nh, dh, ps = kc4.shape
    ppb = MOBA_BLOCK // ps
    nsl = nh * ts * MOBA_TOPK
    vec = pl.BlockSpec((1, hd, ts), lambda b, idx, pt: (b, 0, 0))
    return pl.pallas_call(
        functools.partial(_attn_s_body, nh=nh, ts=ts, n_pages=n_pages, ppb=ppb),
        grid_spec=pltpu.PrefetchScalarGridSpec(
            num_scalar_prefetch=2,
            grid=(db,),
            in_specs=[vec, vec, vec, pl.BlockSpec(memory_space=pl.ANY), pl.BlockSpec(memory_space=pl.ANY)],
            out_specs=vec,
            scratch_shapes=[pltpu.VMEM((2, nsl, dh, MOBA_BLOCK), F32),
                            pltpu.VMEM((2, nsl, dh, MOBA_BLOCK), F32),
                            pltpu.SemaphoreType.DMA((2,))],
        ),
        out_shape=jax.ShapeDtypeStruct((db, hd, ts), F32),
        compiler_params=_cparams(("arbitrary",)),
        name="attn_s",
    )(idx_flat, pt_flat, qt, knt, vnt, kc4, vc4)


def _post_body(x_ref, sh1_ref, sc1_ref, gt1_ref, sh2_ref, sc2_ref, oa_ref, ob_ref,
               g1_ref, g2_ref, wm_ref, bm_ref, wpa_ref, wpb_ref, wo_ref, wr_ref, br_ref,
               x1_ref, tok_ref, gate_ref, cnt_ref):
    x = x_ref[...]
    d = x.shape[1]
    h = _norm_mod(x, g1_ref[...], sh1_ref[0], sc1_ref[0])
    gates = _sigmoid(_dot(h, wm_ref[...]) + bm_ref[...])
    merged = gates[:, :d] * _dot(oa_ref[...], wpa_ref[...]) + gates[:, d:] * _dot(ob_ref[...], wpb_ref[...])
    x1 = x + gt1_ref[0] * _dot(merged, wo_ref[...])
    x1_ref[...] = x1
    h2 = _norm_mod(x1, g2_ref[...], sh2_ref[0], sc2_ref[0])
    tok_ref[...] = h2
    logits = _dot3(h2, wr_ref[...]) + br_ref[...]
    ne = logits.shape[1]
    lane = lax.broadcasted_iota(I32, logits.shape, 1).astype(F32)
    rem = logits
    vals, hots = [], []
    for _ in range(TOPK_E):
        m = jnp.max(rem, axis=1, keepdims=True)
        idx = jnp.min(jnp.where(rem == m, lane, float(ne)), axis=1, keepdims=True)
        hot = lane == idx
        vals.append(m)
        hots.append(hot)
        rem = jnp.where(hot, -jnp.inf, rem)
    es = [jnp.exp(v - vals[0]) for v in vals]
    den = es[0]
    for e in es[1:]:
        den = den + e
    gate = jnp.zeros_like(logits)
    sel = jnp.zeros_like(logits)
    for e, hot in zip(es, hots):
        gate = gate + jnp.where(hot, e / den, 0.0)
        sel = sel + jnp.where(hot, 1.0, 0.0)
    gate_ref[...] = jnp.concatenate([gate, sel], axis=1)
    cnt_ref[0] = jnp.concatenate(
        [jnp.sum(sel[r * ROW_TILE:(r + 1) * ROW_TILE], axis=0, keepdims=True) for r in range(cnt_ref.shape[1])],
        axis=0)


def _post(x2d, mods, oa, ob, g1, g2, wm, bm, wpa, wpb, wo, wr, br, *, nb, t):
    n, d = x2d.shape
    tm = min(TOK_TILE, t)
    nt = t // tm
    ne = wr.shape[1]
    mrows = mods[0].shape[1]
    if mrows == 1:
        mod_spec = pl.BlockSpec((1, 1, d), lambda b, j: (b, 0, 0))
    else:
        mod_spec = pl.BlockSpec((1, tm, d), lambda b, j: (b, j, 0))
    full = lambda a: pl.BlockSpec(a.shape, lambda b, j: (0,) * a.ndim)
    tok = lambda w: pl.BlockSpec((tm, w), lambda b, j: (b * nt + j, 0))
    ws = (g1, g2, wm, bm, wpa, wpb, wo, wr, br)
    return pl.pallas_call(
        _post_body,
        grid=(nb, nt),
        in_specs=[tok(d)] + [mod_spec] * 5 + [tok(oa.shape[1]), tok(ob.shape[1])] + [full(w) for w in ws],
        out_specs=[tok(d), tok(d), tok(2 * ne),
                   pl.BlockSpec((1, tm // ROW_TILE, ne), lambda b, j: (b * nt + j, 0, 0))],
        out_shape=[jax.ShapeDtypeStruct((n, d), F32),
                   jax.ShapeDtypeStruct((n, d), F32),
                   jax.ShapeDtypeStruct((n, 2 * ne), F32),
                   jax.ShapeDtypeStruct((nb * nt, tm // ROW_TILE, ne), F32)],
        compiler_params=_cparams(("arbitrary", "arbitrary")),
        name="post",
    )(x2d, *mods, oa, ob, *ws)


def _route_body(gs_ref, base_ref, pos_ref, g4_ref):
    ne = gs_ref.shape[1] // 2
    tm = ROW_TILE
    r = lax.broadcasted_iota(I32, (tm, tm), 0)
    c = lax.broadcasted_iota(I32, (tm, tm), 1)
    tri = jnp.where(c < r, 1.0, 0.0).astype(BF16)
    lane = lax.broadcasted_iota(I32, (tm, ne), 1).astype(F32)
    for rt in range(gs_ref.shape[0] // tm):
        rows = slice(rt * tm, (rt + 1) * tm)
        gs = gs_ref[rows]
        gate, sel = gs[:, :ne], gs[:, ne:]
        rank = jnp.dot(tri, sel.astype(BF16), preferred_element_type=F32)
        posm = base_ref[rt] + rank
        rem = sel > 0.0
        ps, gv = [], []
        for _ in range(TOPK_E):
            idx = jnp.min(jnp.where(rem, lane, float(ne)), axis=1, keepdims=True)
            hot = lane == idx
            ps.append(jnp.sum(jnp.where(hot, posm, 0.0), axis=1, keepdims=True))
            gv.append(jnp.sum(jnp.where(hot, gate, 0.0), axis=1, keepdims=True))
            rem = rem & jnp.logical_not(hot)
        pos_ref[rows] = jnp.concatenate(ps, axis=1).astype(I32)
        g4_ref[rows] = jnp.concatenate(gv, axis=1)


def _route(gs, base):
    n, ne2 = gs.shape
    per_step = 2 if n % (2 * ROW_TILE) == 0 else 1
    tm = per_step * ROW_TILE
    return pl.pallas_call(
        _route_body,
        grid=(n // tm,),
        in_specs=[pl.BlockSpec((tm, ne2), lambda j: (j, 0)),
                  pl.BlockSpec((per_step, 1, ne2 // 2), lambda j: (j, 0, 0))],
        out_specs=[pl.BlockSpec((tm, TOPK_E), lambda j: (j, 0)),
                   pl.BlockSpec((tm, TOPK_E), lambda j: (j, 0))],
        out_shape=[jax.ShapeDtypeStruct((n, TOPK_E), I32),
                   jax.ShapeDtypeStruct((n, TOPK_E), F32)],
        compiler_params=_cparams(("arbitrary",)),
        name="route",
    )(gs, base)


def _segment_copies(ls_ref, gb_ref, nc_ref, tile, n_exp, make, act):
    per_big = BIG_CHUNK // CHUNK

    def per_expert(e, c):
        i = tile * n_exp + e
        ls = ls_ref[i]
        gb = gb_ref[i]
        nbig = nc_ref[i] // per_big

        def chunk(rows, base):
            def f(k, c2):
                off = base + k * rows
                cp = make(pl.multiple_of(ls + off, CHUNK), pl.multiple_of(gb + off, CHUNK), rows)
                cp.start() if act == "start" else cp.wait()
                return c2
            return f

        lax.fori_loop(0, nbig, chunk(BIG_CHUNK, 0), 0)
        lax.fori_loop(0, nc_ref[i] - nbig * per_big, chunk(CHUNK, nbig * BIG_CHUNK), 0)
        return c

    lax.fori_loop(0, n_exp, per_expert, 0)


def _dispatch_body(ls_ref, gb_ref, nc_ref, ts_ref, tn_ref, lr_ref, tok_ref, *rest, tile0, n_exp, first):
    if first:
        xs_ref, loc, zrow, sem, zsem = rest
    else:
        _, xs_ref, loc, zrow, sem, zsem = rest
    j = pl.program_id(0)
    nj = pl.num_programs(0)
    slot = j % 2
    nloc, tm = loc.shape[1], tok_ref.shape[0]

    if first:
        @pl.when(j == 0)
        def _():
            zrow[...] = jnp.zeros_like(zrow)

            def tails(act):
                def per_expert(e, c):
                    def per_chunk(k, c2):
                        cp = pltpu.make_async_copy(
                            zrow, xs_ref.at[pl.ds(pl.multiple_of(ts_ref[e] + k * CHUNK, CHUNK), CHUNK)], zsem)
                        cp.start() if act == "start" else cp.wait()
                        return c2
                    lax.fori_loop(0, tn_ref[e], per_chunk, 0)
                    return c
                lax.fori_loop(0, n_exp, per_expert, 0)

            tails("start")
            tails("wait")

    r = lax.broadcasted_iota(I32, (nloc, tm), 0)
    perm = jnp.zeros((nloc, tm), F32)
    for k in range(TOPK_E):
        perm = perm + jnp.where(r == lr_ref[0, k:k + 1, :], 1.0, 0.0)
    loc[slot] = jnp.dot(perm.astype(BF16), tok_ref[...].astype(BF16), preferred_element_type=F32)

    def make(slot_):
        return lambda ls, gb, rows: pltpu.make_async_copy(
            loc.at[slot_, pl.ds(ls, rows)], xs_ref.at[pl.ds(gb, rows)], sem.at[slot_])

    _segment_copies(ls_ref, gb_ref, nc_ref, tile0 + j, n_exp, make(slot), "start")

    @pl.when(j > 0)
    def _():
        _segment_copies(ls_ref, gb_ref, nc_ref, tile0 + j - 1, n_exp, make(1 - slot), "wait")

    @pl.when(j == nj - 1)
    def _():
        _segment_copies(ls_ref, gb_ref, nc_ref, tile0 + j, n_exp, make(slot), "wait")


def _dispatch(seg, tails, lr_rows, tok, xs, *, tile0, rows):
    n, d = tok.shape
    tm = ROW_TILE
    n_exp = tails[0].shape[0]
    first = xs is None
    nloc = TOPK_E * tm + n_exp * CHUNK
    in_specs = [pl.BlockSpec((1, TOPK_E, tm), lambda j, *_: (tile0 + j, 0, 0)),
                pl.BlockSpec((tm, d), lambda j, *_: (j, 0))]
    args = [lr_rows, tok]
    if not first:
        in_specs.append(pl.BlockSpec(memory_space=pl.ANY))
        args.append(xs)
    return pl.pallas_call(
        functools.partial(_dispatch_body, tile0=tile0, n_exp=n_exp, first=first),
        grid_spec=pltpu.PrefetchScalarGridSpec(
            num_scalar_prefetch=5,
            grid=(n // tm,),
            in_specs=in_specs,
            out_specs=pl.BlockSpec(memory_space=pl.ANY),
            scratch_shapes=[pltpu.VMEM((2, nloc, d), F32), pltpu.VMEM((CHUNK, d), F32),
                            pltpu.SemaphoreType.DMA((2,)), pltpu.SemaphoreType.DMA(())],
        ),
        out_shape=jax.ShapeDtypeStruct((rows, d), F32),
        input_output_aliases={} if first else {7: 0},
        compiler_params=_cparams(("arbitrary",)),
        name="dispatch",
    )(*seg, *tails, *args)


def _moe_body(te_ref, nt_ref, x_ref, wu_ref, bg_ref, bl_ref, wd_ref, bd_ref, o_ref, wg_s, wl_s, wd_s):
    g = pl.program_id(0)

    @pl.when((g == 0) | (te_ref[g] != te_ref[jnp.maximum(g - 1, 0)]))
    def _():
        w2 = 2 * LANES
        r = lax.broadcasted_iota(I32, (w2, w2), 0)
        c = lax.broadcasted_iota(I32, (w2, w2), 1)
        src = jnp.where(c < LANES, 2 * c, 2 * (c - LANES) + 1)
        perm = jnp.where(r == src, 1.0, 0.0).astype(BF16)
        for k in range(wu_ref.shape[2] // w2):
            pair = jnp.dot(wu_ref[0, :, k * w2:(k + 1) * w2].astype(BF16), perm, preferred_element_type=F32)
            wg_s[:, k * LANES:(k + 1) * LANES] = pair[:, :LANES].astype(BF16)
            wl_s[:, k * LANES:(k + 1) * LANES] = pair[:, LANES:].astype(BF16)
        wd_s[...] = wd_ref[0].astype(BF16)

    @pl.when(g < nt_ref[0])
    def _():
        x = x_ref[...].astype(BF16)
        glu = jnp.minimum(jnp.dot(x, wg_s[...], preferred_element_type=F32) + bg_ref[0], SWIGLU_LIMIT)
        lin = jnp.clip(jnp.dot(x, wl_s[...], preferred_element_type=F32) + bl_ref[0], -SWIGLU_LIMIT, SWIGLU_LIMIT)
        act = glu * _sigmoid(SWIGLU_ALPHA * glu) * (lin + 1.0)
        o_ref[...] = jnp.dot(act.astype(BF16), wd_s[...], preferred_element_type=F32) + bd_ref[0]

    @pl.when(g >= nt_ref[0])
    def _():
        o_ref[...] = jnp.zeros_like(o_ref)


def _moe(tile_expert, n_tiles, xs, wu, bg, bl, wd, bd):
    r, d = xs.shape
    tm = MOE_TILE
    dff = wd.shape[1]
    row = lambda g, te, nt: (jnp.minimum(g, nt[0] - 1), 0)
    wsp = lambda a: pl.BlockSpec((1,) + a.shape[1:], lambda g, te, nt: (te[g], 0, 0))
    return pl.pallas_call(
        _moe_body,
        grid_spec=pltpu.PrefetchScalarGridSpec(
            num_scalar_prefetch=2,
            grid=(r // tm,),
            in_specs=[pl.BlockSpec((tm, d), row), wsp(wu), wsp(bg), wsp(bl), wsp(wd), wsp(bd)],
            out_specs=pl.BlockSpec((tm, d), lambda g, te, nt: (g, 0)),
            scratch_shapes=[pltpu.VMEM((d, dff), BF16), pltpu.VMEM((d, dff), BF16), pltpu.VMEM((dff, d), BF16)],
        ),
        out_shape=jax.ShapeDtypeStruct((r, d), F32),
        compiler_params=_cparams(("arbitrary",)),
        name="moe",
    )(tile_expert, n_tiles, xs, wu, bg, bl, wd, bd)


def _combine_body(ls_ref, gb_ref, nc_ref, lr_ref, g4_ref, x1_ref, gt2_ref, nf_ref, ys_ref, y_ref, loc, sem,
                  *, tile0, n_exp):
    j = pl.program_id(0)
    nj = pl.num_programs(0)
    slot = j % 2
    nloc, tm = loc.shape[1], x1_ref.shape[0]

    def make(slot_):
        return lambda ls, gb, rows: pltpu.make_async_copy(
            ys_ref.at[pl.ds(gb, rows)], loc.at[slot_, pl.ds(ls, rows)], sem.at[slot_])

    @pl.when(j == 0)
    def _():
        loc[...] = jnp.zeros_like(loc)
        _segment_copies(ls_ref, gb_ref, nc_ref, tile0, n_exp, make(0), "start")

    @pl.when(j + 1 < nj)
    def _():
        _segment_copies(ls_ref, gb_ref, nc_ref, tile0 + j + 1, n_exp, make(1 - slot), "start")

    _segment_copies(ls_ref, gb_ref, nc_ref, tile0 + j, n_exp, make(slot), "wait")

    lane = lax.broadcasted_iota(I32, (tm, nloc), 1)
    lr = lr_ref[...]
    g4 = g4_ref[...]
    w = jnp.zeros((tm, nloc), F32)
    for k in range(TOPK_E):
        w = w + jnp.where(lane == lr[:, k:k + 1], g4[:, k:k + 1], 0.0)
    wh, wl = _split(w)
    yb = loc[slot].astype(BF16)
    ff = jnp.dot(wh, yb, preferred_element_type=F32) + jnp.dot(wl, yb, preferred_element_type=F32)
    y_ref[...] = _rms(x1_ref[...] + gt2_ref[0] * ff, nf_ref[...])


def _combine(seg, lr, g4, x1, gt2, nf, ys, *, tile0, n_exp, t):
    n, d = x1.shape
    tm = min(ROW_TILE, t)
    nt = t // tm
    nloc = TOPK_E * tm + n_exp * CHUNK
    if gt2.shape[1] == 1:
        mod_spec = pl.BlockSpec((1, 1, d), lambda j, *_: (j // nt, 0, 0))
    else:
        mod_spec = pl.BlockSpec((1, tm, d), lambda j, *_: (j // nt, j % nt, 0))
    return pl.pallas_call(
        functools.partial(_combine_body, tile0=tile0, n_exp=n_exp),
        grid_spec=pltpu.PrefetchScalarGridSpec(
            num_scalar_prefetch=3,
            grid=(n // tm,),
            in_specs=[pl.BlockSpec((tm, TOPK_E), lambda j, *_: (tile0 + j, 0)),
                      pl.BlockSpec((tm, TOPK_E), lambda j, *_: (tile0 + j, 0)),
                      pl.BlockSpec((tm, d), lambda j, *_: (j, 0)),
                      mod_spec,
                      pl.BlockSpec((1, d), lambda j, *_: (0, 0)),
                      pl.BlockSpec(memory_space=pl.ANY)],
            out_specs=pl.BlockSpec((tm, d), lambda j, *_: (j, 0)),
            scratch_shapes=[pltpu.VMEM((2, nloc, d), F32), pltpu.SemaphoreType.DMA((2,))],
        ),
        out_shape=jax.ShapeDtypeStruct((n, d), F32),
        compiler_params=_cparams(("arbitrary",)),
        name="combine",
    )(*seg, lr, g4, x1, gt2, nf, ys)


def _block_diag(w):
    g, a, b = w.shape
    eye = jnp.eye(g, dtype=w.dtype)
    return (eye[:, None, :, None] * w[:, :, None, :]).reshape(g * a, g * b)


def _layer0(a):
    return a.reshape(a.shape[1:])

def kernel(x_prompt, x_sample, cache_k, cache_v, state_h, state_conv, page_table, c_prompt, c_sample,
           w_ada, b_ada, norm1, norm2, w_in, conv_w, conv_b, w_rg_a, b_rg_a, w_rg_i, b_rg_i, rg_lambda,
           w_proj_a, w_proj_b, w_merge, b_merge, w_out, w_router, b_router, w_up, b_up, w_down, b_down, norm_f):
    assert w_ada.shape[0] == 1, "single trunk layer"
    bp, t, d = x_prompt.shape
    db, ts, _ = x_sample.shape
    _, n_phys, ps, nh, _ = cache_k.shape
    hd = nh * HEAD_DIM
    da = hd
    dr = conv_w.shape[2]
    n_pages = page_table.shape[1]
    ne = w_router.shape[2]
    np_, ns_ = bp * t, db * ts
    nblk = t // MOBA_BLOCK
    assert t % TOK_TILE == 0 and ns_ == TOK_TILE and n_pages % 16 == 0 and (dr // 2) % w_rg_a.shape[2] == 0
    assert (n_pages * ps) % MOBA_BLOCK == 0 and (n_pages * ps) // MOBA_BLOCK >= MOBA_TOPK

    mod = _ada(jnp.concatenate([c_prompt, c_sample], axis=0), _layer0(w_ada), _layer0(b_ada))
    mod = mod.reshape(bp + db, N_MOD, d)
    mp = [mod[:bp, i][:, None, :] for i in range(N_MOD)]
    ms = [jnp.tile(mod[bp:, i], (ts, 1))[None] for i in range(N_MOD)]

    g1, g2, nf = norm1, norm2, norm_f[None]
    w_in0 = _layer0(w_in)
    wqkvt, wk, wug = w_in0[:, :3 * da].T, w_in0[:, da:2 * da], w_in0[:, 3 * da:]

    xp2 = x_prompt.reshape(np_, d)
    xs2 = jnp.swapaxes(x_sample, 0, 1).reshape(ns_, d)

    qt_p, kt_p, vt_p, u_p, g_p, kb_p, vtb_p, km_p = _proj(
        xp2, mp[0], mp[1], g1, wqkvt.astype(BF16), wug.astype(BF16), wk.astype(BF16), nb=bp, t=t)
    qt_s, kt_s, vt_s, u_s, g_s = _proj(xs2, ms[0], ms[1], g1, wqkvt, wug, nb=1, t=ns_)

    oa_p = _attn_p(qt_p, kb_p, vtb_p, km_p.reshape(bp, nblk, da), nb=bp, t=t)

    kc4 = jnp.transpose(cache_k.reshape(n_phys, ps, nh, HEAD_DIM), (0, 2, 3, 1))
    vc4 = jnp.transpose(cache_v.reshape(n_phys, ps, nh, HEAD_DIM), (0, 2, 3, 1))
    pt_flat = page_table.reshape(-1)
    km = _kmean_s(pt_flat, kc4.reshape(n_phys, hd, ps), db=db, n_pages=n_pages)
    km = jnp.transpose(km, (0, 1, 3, 2)).reshape(db, -1, hd)
    tmaj = lambda a: jnp.transpose(a.reshape(hd, ts, db), (2, 0, 1))
    qs3, kn3, vn3 = tmaj(qt_s), tmaj(kt_s), tmaj(vt_s)
    idx = _sel_s(jnp.transpose(qt_s.reshape(hd, ts, db), (2, 1, 0)), km)
    idx = jnp.swapaxes(idx, 1, 2)
    ot_s = _attn_s(idx.reshape(-1), pt_flat, qs3, kn3, vn3, kc4, vc4, n_pages=n_pages)
    oa_s = jnp.transpose(ot_s, (2, 0, 1)).reshape(ns_, hd)

    cw, cb = _layer0(conv_w), conv_b
    wa, wi = _block_diag(_layer0(w_rg_a)), _block_diag(_layer0(w_rg_i))
    ba, bi, lam = b_rg_a, b_rg_i, rg_lambda
    ob_p, hl_p, cn_p = _rg_p(u_p, g_p, cw, cb, wa, ba, wi, bi, lam, nb=bp, t=t)
    ob_s, hl_s, cn_s = _rg_s(u_s.reshape(ts, db, dr), g_s.reshape(ts, db, dr),
                             jnp.swapaxes(_layer0(state_conv), 0, 1), _layer0(state_h),
                             cw, cb, wa, ba, wi, bi, lam)

    wsm = (g1, g2, _layer0(w_merge).astype(BF16), b_merge, _layer0(w_proj_a).astype(BF16),
           _layer0(w_proj_b).astype(BF16), _layer0(w_out).astype(BF16), _layer0(w_router), b_router)
    x1_p, tok_p, gs_p, cnt_p = _post(xp2, mp[:5], oa_p, ob_p, *wsm, nb=bp, t=t)
    x1_s, tok_s, gs_s, cnt_s = _post(xs2, ms[:5], oa_s.astype(BF16), ob_s.reshape(ns_, dr).astype(BF16), *wsm,
                                     nb=1, t=ns_)

    gs = jnp.concatenate([gs_p, gs_s], axis=0)
    cnt = jnp.concatenate([cnt_p, cnt_s], axis=0).reshape(-1, ne).astype(I32)
    n_rt = cnt.shape[0]
    seg = (cnt + CHUNK - 1) // CHUNK * CHUNK
    lstart = jnp.cumsum(seg, axis=1) - seg
    tot = jnp.sum(seg, axis=0)
    tiles_e = (tot + MOE_TILE - 1) // MOE_TILE
    tile_end = jnp.cumsum(tiles_e)
    poff = (tile_end - tiles_e) * MOE_TILE
    gbase = poff[None, :] + jnp.cumsum(seg, axis=0) - seg
    seg_tabs = (lstart.reshape(-1), gbase.reshape(-1), (seg // CHUNK).reshape(-1))
    tail_tabs = (poff + tot, (tiles_e * MOE_TILE - tot) // CHUNK)
    n_all = np_ + ns_
    rows_max = n_all * TOPK_E + n_rt * ne * (CHUNK - 1) + ne * MOE_TILE
    g_max = -(-rows_max // MOE_TILE)
    n_tiles = tile_end[-1:].astype(I32)
    tile_id = jnp.minimum(jnp.arange(g_max, dtype=I32), n_tiles - 1)
    te = jnp.sum((tile_end[None, :] <= tile_id[:, None]).astype(I32), axis=1)
    lr, g4 = _route(gs, lstart.astype(F32)[:, None, :])
    lr_rows = jnp.swapaxes(lr.reshape(n_rt, ROW_TILE, TOPK_E), 1, 2)
    rt_p = np_ // ROW_TILE

    xs = _dispatch(seg_tabs, tail_tabs, lr_rows, tok_p, None, tile0=0, rows=g_max * MOE_TILE)
    xs = _dispatch(seg_tabs, tail_tabs, lr_rows, tok_s, xs, tile0=rt_p, rows=g_max * MOE_TILE)
    b_up0 = _layer0(b_up)
    ys = _moe(te, n_tiles, xs, _layer0(w_up), b_up0[:, None, 0::2], b_up0[:, None, 1::2],
              _layer0(w_down), _layer0(b_down)[:, None, :])
    y_p = _combine(seg_tabs, lr, g4, x1_p, mp[5], nf, ys, tile0=0, n_exp=ne, t=t)
    y_s = _combine(seg_tabs, lr, g4, x1_s, ms[5], nf, ys, tile0=rt_p, n_exp=ne, t=ns_)

    y_prompt = y_p.reshape(bp, t, d)
    y_sample = jnp.swapaxes(y_s.reshape(ts, db, d), 0, 1)
    to5 = lambda a: jnp.transpose(a.reshape(bp, nh, HEAD_DIM, t), (0, 3, 1, 2))[None]
    k_prompt, v_prompt = to5(kt_p), to5(vt_p)
    to5s = lambda a: jnp.transpose(a.reshape(nh, HEAD_DIM, ts, db), (3, 2, 0, 1))[None]
    k_sample, v_sample = to5s(kt_s), to5s(vt_s)
    h_prompt = hl_p.reshape(1, bp, dr)
    conv_prompt = cn_p[None]
    h_sample = hl_s[None]
    conv_sample = jnp.swapaxes(cn_s, 0, 1)[None]
    return (y_prompt, y_sample, k_prompt, v_prompt, h_prompt, conv_prompt,
            k_sample, v_sample, h_sample, conv_sample)
```
